```python
import math
import jax
import jax.numpy as jnp
from jax import lax
import numpy as np

D_MODEL = 1024
BATCH = 1
SEQ = 16384
DEPTH = 2
DEC_BATCH = 4
DEC_SEQ = 4096
PAST_LEN = 128

N_META = 16
BLOCK = 128
WINDOW = 128
ROPE_THETA = 10000.0
EPS = 1e-6
NEG_INF = -1e30

DA_HEADS = 4
DA_HEAD_DIM = 64
DA_V_DIM = 2 * DA_HEAD_DIM
MLA_HEADS = 4
MLA_Q_RANK = 384
MLA_KV_RANK = 256
MLA_NOPE = 128
MLA_ROPE = 64
MLA_QK_DIM = MLA_NOPE + MLA_ROPE
MLA_V = 128
GQA_HEADS = 8
GQA_KV_HEADS = 2
GQA_GROUP = GQA_HEADS // GQA_KV_HEADS
GQA_HEAD_DIM = 64

N_BRANCH = 3
BRANCH_WIDTH = 512
D_FF = 4 * D_MODEL

DA_Q_COLS = DA_HEADS * 2 * DA_HEAD_DIM
DA_K_COLS = DA_HEADS * 2 * DA_HEAD_DIM
DA_V_COLS = DA_HEADS * DA_V_DIM
MLA_CQ_COLS = MLA_Q_RANK
MLA_CKV_COLS = MLA_KV_RANK
MLA_KR_COLS = MLA_ROPE
GQA_Q_COLS = GQA_HEADS * GQA_HEAD_DIM
GQA_K_COLS = GQA_KV_HEADS * GQA_HEAD_DIM
GQA_V_COLS = GQA_KV_HEADS * GQA_HEAD_DIM
GATE_COLS = N_BRANCH * D_MODEL
IN_SIZES = (DA_Q_COLS, DA_K_COLS, DA_V_COLS, MLA_CQ_COLS, MLA_CKV_COLS, MLA_KR_COLS,
            GQA_Q_COLS, GQA_K_COLS, GQA_V_COLS, GATE_COLS)
IN_SPLITS = tuple(sum(IN_SIZES[:i + 1]) for i in range(len(IN_SIZES) - 1))
IN_WIDTH = sum(IN_SIZES)

kernel_name = "hybrid_diff_mla_swa_gated_encoder"


def rms_norm(x, g):
    xf = x.astype(jnp.float32)
    y = xf * lax.rsqrt(jnp.mean(xf * xf, axis=-1, keepdims=True) + EPS)
    return (y * g.astype(jnp.float32)).astype(x.dtype)


def rope_tables(length, dim):
    inv_freq = 1.0 / (ROPE_THETA ** (jnp.arange(0, dim, 2, dtype=jnp.float32) / dim))
    ang = jnp.arange(length, dtype=jnp.float32)[:, None] * inv_freq[None, :]
    return jnp.cos(ang), jnp.sin(ang)


def apply_rope(x, cos, sin):
    xf = x.astype(jnp.float32)
    x1, x2 = jnp.split(xf, 2, axis=-1)
    c = cos[None, :, None, :]
    s = sin[None, :, None, :]
    return jnp.concatenate([x1 * c - x2 * s, x1 * s + x2 * c], axis=-1).astype(x.dtype)


def sweep_query_blocks(attend, qs):
    B, L = qs[0].shape[:2]
    nb = (L - N_META) // BLOCK
    meta_out = attend(tuple(q[:, :N_META] for q in qs))
    blocks = tuple(jnp.moveaxis(q[:, N_META:].reshape((B, nb, BLOCK) + q.shape[2:]), 1, 0) for q in qs)
    real_out = lax.map(attend, blocks)
    real_out = jnp.moveaxis(real_out, 0, 1).reshape((B, nb * BLOCK) + real_out.shape[3:])
    return jnp.concatenate([meta_out, real_out], axis=1)


def diff_attention(q, k, v, q_norm_g, k_norm_g, lam_q1, lam_k1, lam_q2, lam_k2, subln_g, lam_init):
    B, L, _ = q.shape
    cos, sin = rope_tables(L, DA_HEAD_DIM)
    q = apply_rope(rms_norm(q.reshape(B, L, DA_HEADS * 2, DA_HEAD_DIM), q_norm_g), cos, sin)
    k = apply_rope(rms_norm(k.reshape(B, L, DA_HEADS * 2, DA_HEAD_DIM), k_norm_g), cos, sin)
    q = q.reshape(B, L, DA_HEADS, 2, DA_HEAD_DIM)
    k = k.reshape(B, L, DA_HEADS, 2, DA_HEAD_DIM)
    q1, q2 = q[..., 0, :], q[..., 1, :]
    k1, k2 = k[..., 0, :], k[..., 1, :]
    v = v.reshape(B, L, DA_HEADS, DA_V_DIM)
    lam = (jnp.exp(jnp.sum(lam_q1.astype(jnp.float32) * lam_k1.astype(jnp.float32)))
           - jnp.exp(jnp.sum(lam_q2.astype(jnp.float32) * lam_k2.astype(jnp.float32)))
           + lam_init)
    scale = DA_HEAD_DIM ** -0.5

    def attend(qb):
        qb1, qb2 = qb
        p1 = jax.nn.softmax(jnp.einsum('bqhd,bkhd->bhqk', qb1, k1).astype(jnp.float32) * scale, axis=-1)
        p2 = jax.nn.softmax(jnp.einsum('bqhd,bkhd->bhqk', qb2, k2).astype(jnp.float32) * scale, axis=-1)
        a = (p1 - lam * p2).astype(v.dtype)
        return jnp.einsum('bhqk,bkhd->bqhd', a, v)

    o = sweep_query_blocks(attend, (q1, q2))
    o = rms_norm(o, subln_g) * (1.0 - lam_init)
    return o.reshape(B, L, DA_HEADS * DA_V_DIM)


def mla_attention(c_q, c_kv, k_rope, cq_norm_g, ckv_norm_g, w_uq, w_ukv, q_norm_g, k_norm_g):
    B, L, _ = c_q.shape
    cos, sin = rope_tables(L, MLA_ROPE)
    q = (rms_norm(c_q, cq_norm_g) @ w_uq).reshape(B, L, MLA_HEADS, MLA_QK_DIM)
    kv = (rms_norm(c_kv, ckv_norm_g) @ w_ukv).reshape(B, L, MLA_HEADS, MLA_NOPE + MLA_V)
    k_nope, v = kv[..., :MLA_NOPE], kv[..., MLA_NOPE:]
    k_r = jnp.broadcast_to(k_rope[:, :, None, :], (B, L, MLA_HEADS, MLA_ROPE))
    k = jnp.concatenate([k_nope, k_r], axis=-1)
    q = rms_norm(q, q_norm_g)
    k = rms_norm(k, k_norm_g)
    q = jnp.concatenate([q[..., :MLA_NOPE], apply_rope(q[..., MLA_NOPE:], cos, sin)], axis=-1)
    k = jnp.concatenate([k[..., :MLA_NOPE], apply_rope(k[..., MLA_NOPE:], cos, sin)], axis=-1)
    scale = MLA_QK_DIM ** -0.5

    def attend(qbs):
        (qb,) = qbs
        p = jax.nn.softmax(jnp.einsum('bqhd,bkhd->bhqk', qb, k).astype(jnp.float32) * scale, axis=-1)
        return jnp.einsum('bhqk,bkhd->bqhd', p.astype(v.dtype), v)

    o = sweep_query_blocks(attend, (q,))
    return o.reshape(B, L, MLA_HEADS * MLA_V)


def window_gqa_attention(q, k, v, q_norm_g, k_norm_g, sink):
    B, L, _ = q.shape
    nb = (L - N_META) // BLOCK
    cos, sin = rope_tables(L, GQA_HEAD_DIM)
    q = apply_rope(rms_norm(q.reshape(B, L, GQA_HEADS, GQA_HEAD_DIM), q_norm_g), cos, sin)
    k = apply_rope(rms_norm(k.reshape(B, L, GQA_KV_HEADS, GQA_HEAD_DIM), k_norm_g), cos, sin)
    v = v.reshape(B, L, GQA_KV_HEADS, GQA_HEAD_DIM)
    q = q.reshape(B, L, GQA_KV_HEADS, GQA_GROUP, GQA_HEAD_DIM)
    scale = GQA_HEAD_DIM ** -0.5
    sink_f = sink.astype(jnp.float32).reshape(GQA_KV_HEADS, GQA_GROUP)

    def sink_softmax(s):
        sk = jnp.broadcast_to(sink_f[:, :, None, None], s.shape[:-1] + (1,))
        return jax.nn.softmax(jnp.concatenate([s, sk], axis=-1), axis=-1)[..., :-1]

    nk_meta = N_META + BLOCK
    s_m = jnp.einsum('bqkgd,bskd->bkgqs', q[:, :N_META], k[:, :nk_meta]).astype(jnp.float32) * scale
    qi_m = jnp.arange(N_META)[:, None]
    kj_m = jnp.arange(nk_meta)[None, :]
    s_m = jnp.where(kj_m <= qi_m + WINDOW, s_m, NEG_INF)
    o_meta = jnp.einsum('bkgqs,bskd->bqkgd', sink_softmax(s_m).astype(v.dtype), v[:, :nk_meta])

    k_meta, v_meta = k[:, :N_META], v[:, :N_META]
    qr = q[:, N_META:].reshape(B, nb, BLOCK, GQA_KV_HEADS, GQA_GROUP, GQA_HEAD_DIM)

    def band(t):
        t = t[:, N_META:].reshape(B, nb, BLOCK, GQA_KV_HEADS, GQA_HEAD_DIM)
        tp = jnp.pad(t, ((0, 0), (1, 1), (0, 0), (0, 0), (0, 0)))
        return jnp.concatenate([tp[:, :-2], tp[:, 1:-1], tp[:, 2:]], axis=2)

    kb, vb = band(k), band(v)
    s_band = jnp.einsum('bnqkgd,bnskd->bnkgqs', qr, kb).astype(jnp.float32) * scale
    s_mk = jnp.einsum('bnqkgd,bskd->bnkgqs', qr, k_meta).astype(jnp.float32) * scale
    blk = jnp.arange(nb)[:, None, None]
    qi = jnp.arange(BLOCK)[None, :, None]
    sj = jnp.arange(3 * BLOCK)[None, None, :]
    key_pos = (blk - 1) * BLOCK + sj
    q_pos = blk * BLOCK + qi
    visible = (jnp.abs(key_pos - q_pos) <= WINDOW) & (key_pos >= 0) & (key_pos < nb * BLOCK)
    s_band = jnp.where(visible[None, :, None, None], s_band, NEG_INF)
    p = sink_softmax(jnp.concatenate([s_mk, s_band], axis=-1)).astype(v.dtype)
    o_real = (jnp.einsum('bnkgqs,bnskd->bnqkgd', p[..., N_META:], vb)
              + jnp.einsum('bnkgqs,bskd->bnqkgd', p[..., :N_META], v_meta))
    o_real = o_real.reshape(B, nb * BLOCK, GQA_KV_HEADS, GQA_GROUP, GQA_HEAD_DIM)
    o = jnp.concatenate([o_meta, o_real], axis=1)
    return o.reshape(B, L, GQA_HEADS * GQA_HEAD_DIM)


def encoder_layer(h, lam_init, attn_norm_g, w_in, da_q_norm_g, da_k_norm_g, da_lam_q1, da_lam_k1,
                  da_lam_q2, da_lam_k2, da_subln_g, mla_cq_norm_g, mla_ckv_norm_g, mla_w_uq, mla_w_ukv,
                  mla_q_norm_g, mla_k_norm_g, gqa_q_norm_g, gqa_k_norm_g, gqa_sink, w_branch, w_out,
                  mlp_norm_g, w_up, w_down):
    B, L, _ = h.shape
    xp = rms_norm(h, attn_norm_g) @ w_in
    (da_q, da_k, da_v, mla_cq, mla_ckv, mla_kr,
     gqa_q, gqa_k, gqa_v, gate_logits) = jnp.split(xp, IN_SPLITS, axis=-1)
    o_da = diff_attention(da_q, da_k, da_v, da_q_norm_g, da_k_norm_g, da_lam_q1, da_lam_k1,
                          da_lam_q2, da_lam_k2, da_subln_g, lam_init)
    o_mla = mla_attention(mla_cq, mla_ckv, mla_kr, mla_cq_norm_g, mla_ckv_norm_g, mla_w_uq, mla_w_ukv,
                          mla_q_norm_g, mla_k_norm_g)
    o_gqa = window_gqa_attention(gqa_q, gqa_k, gqa_v, gqa_q_norm_g, gqa_k_norm_g, gqa_sink)
    branches = jnp.stack([o_da, o_mla, o_gqa], axis=2)
    proj = jnp.einsum('blgc,gcd->blgd', branches, w_branch)
    gates = jax.nn.sigmoid(gate_logits.astype(jnp.float32)).reshape(B, L, N_BRANCH, D_MODEL)
    merged = jnp.sum(gates * proj.astype(jnp.float32), axis=2).astype(h.dtype)
    h = h + merged @ w_out
    u = rms_norm(h, mlp_norm_g) @ w_up
    h = h + jnp.square(jax.nn.relu(u)) @ w_down
    return h


def encoder_trunk(x, meta_tokens, layer_weights):
    B = x.shape[0]
    meta = jnp.broadcast_to(meta_tokens[None].astype(x.dtype), (B, N_META, D_MODEL))
    h = jnp.concatenate([meta, x], axis=1)
    for l in range(DEPTH):
        lam_init = 0.8 - 0.6 * math.exp(-0.3 * l)
        h = encoder_layer(h, lam_init, *[w[l] for w in layer_weights])
    return h[:, N_META:]


def setup_inputs(seed: int = 0) -> dict:
    key = jax.random.key(seed)
    ks = jax.random.split(key, 32)
    f32 = jnp.float32

    def nrm(k, shape, scale):
        return jax.random.normal(k, shape, f32) * scale

    def gain(k, shape):
        return 1.0 + 0.02 * jax.random.normal(k, shape, f32)

    return {
        "x_prompt": nrm(ks[0], (BATCH, SEQ, D_MODEL), 1.0),
        "x_sample": nrm(ks[1], (DEC_BATCH, DEC_SEQ, D_MODEL), 1.0),
        "meta_tokens": nrm(ks[2], (N_META, D_MODEL), 1.0),
        "attn_norm_g": gain(ks[3], (DEPTH, D_MODEL)),
        "w_in": nrm(ks[4], (DEPTH, D_MODEL, IN_WIDTH), D_MODEL ** -0.5),
        "da_q_norm_g": gain(ks[5], (DEPTH, DA_HEAD_DIM)),
        "da_k_norm_g": gain(ks[6], (DEPTH, DA_HEAD_DIM)),
        "da_lam_q1": nrm(ks[7], (DEPTH, DA_HEAD_DIM), 0.1),
        "da_lam_k1": nrm(ks[8], (DEPTH, DA_HEAD_DIM), 0.1),
        "da_lam_q2": nrm(ks[9], (DEPTH, DA_HEAD_DIM), 0.1),
        "da_lam_k2": nrm(ks[10], (DEPTH, DA_HEAD_DIM), 0.1),
        "da_subln_g": gain(ks[11], (DEPTH, DA_V_DIM)),
        "mla_cq_norm_g": gain(ks[12], (DEPTH, MLA_Q_RANK)),
        "mla_ckv_norm_g": gain(ks[13], (DEPTH, MLA_KV_RANK)),
        "mla_w_uq": nrm(ks[14], (DEPTH, MLA_Q_RANK, MLA_HEADS * MLA_QK_DIM), MLA_Q_RANK ** -0.5),
        "mla_w_ukv": nrm(ks[15], (DEPTH, MLA_KV_RANK, MLA_HEADS * (MLA_NOPE + MLA_V)), MLA_KV_RANK ** -0.5),
        "mla_q_norm_g": gain(ks[16], (DEPTH, MLA_QK_DIM)),
        "mla_k_norm_g": gain(ks[17], (DEPTH, MLA_QK_DIM)),
        "gqa_q_norm_g": gain(ks[18], (DEPTH, GQA_HEAD_DIM)),
        "gqa_k_norm_g": gain(ks[19], (DEPTH, GQA_HEAD_DIM)),
        "gqa_sink": nrm(ks[20], (DEPTH, GQA_HEADS), 0.5),
        "w_branch": nrm(ks[21], (DEPTH, N_BRANCH, BRANCH_WIDTH, D_MODEL), BRANCH_WIDTH ** -0.5),
        "w_out": nrm(ks[22], (DEPTH, D_MODEL, D_MODEL), D_MODEL ** -0.5),
        "mlp_norm_g": gain(ks[23], (DEPTH, D_MODEL)),
        "w_up": nrm(ks[24], (DEPTH, D_MODEL, D_FF), D_MODEL ** -0.5),
        "w_down": nrm(ks[25], (DEPTH, D_FF, D_MODEL), D_FF ** -0.5),
    }


def reference(x_prompt, x_sample, meta_tokens, attn_norm_g, w_in, da_q_norm_g, da_k_norm_g,
              da_lam_q1, da_lam_k1, da_lam_q2, da_lam_k2, da_subln_g, mla_cq_norm_g, mla_ckv_norm_g,
              mla_w_uq, mla_w_ukv, mla_q_norm_g, mla_k_norm_g, gqa_q_norm_g, gqa_k_norm_g, gqa_sink,
              w_branch, w_out, mlp_norm_g, w_up, w_down):
    layer_weights = (attn_norm_g, w_in, da_q_norm_g, da_k_norm_g, da_lam_q1, da_lam_k1, da_lam_q2,
                     da_lam_k2, da_subln_g, mla_cq_norm_g, mla_ckv_norm_g, mla_w_uq, mla_w_ukv,
                     mla_q_norm_g, mla_k_norm_g, gqa_q_norm_g, gqa_k_norm_g, gqa_sink, w_branch, w_out,
                     mlp_norm_g, w_up, w_down)
    y_prompt = encoder_trunk(x_prompt, meta_tokens, layer_weights)
    y_sample = encoder_trunk(x_sample, meta_tokens, layer_weights)
    return (y_prompt, y_sample)
```

```python
import functools
import math

import jax
import jax.numpy as jnp
from jax import lax
from jax.experimental import pallas as pl
from jax.experimental.pallas import tpu as pltpu

F32 = jnp.float32
BF16 = jnp.bfloat16

D_MODEL = 1024
N_META = 16
BLOCK = 128
WINDOW = 128
ROPE_THETA = 10000.0
EPS = 1e-6
NEG = -1e30
LOG2E = 1.4426950408889634

HEAD64 = 64
MLA_HEADS = 4
MLA_NOPE = 128
MLA_QK = 192
MLA_Q_RANK = 384
MLA_KV_RANK = 256
GQA_HEADS = 8
GQA_GROUP = 4
BRANCH_WIDTH = 512
N_BRANCH = 3
D_FF = 4 * D_MODEL
LANES = 128
TAIL = 128

A_DAQ, A_DAK, A_DAV, A_CQ, A_CKV, A_KR, A_GQ, A_GK, A_GV, A_END = (
    0, 512, 1024, 1536, 1920, 2176, 2304, 2816, 3072, 3328)
IN_OFF = (0, 512, 1024, 1536, 1920, 2176, 2240, 2752, 2880, 3008, 6080)

VMEM_LIMIT = 56 * 1024 * 1024


def _pick(n, candidates):
    for c in candidates:
        if n % c == 0:
            return c
    raise ValueError(f"no tile in {candidates} divides {n}")


def _cparams(sem):
    return pltpu.CompilerParams(dimension_semantics=sem, vmem_limit_bytes=VMEM_LIMIT)


def _resident(shape):
    zeros = (0,) * len(shape)
    return pl.BlockSpec(shape, lambda *_: zeros, pipeline_mode=pl.Buffered(1))


def _rms_scale(x, width):
    return lax.rsqrt(jnp.sum(x * x, axis=-1, keepdims=True) * (1.0 / width) + EPS)


def _pair_rms_scale(xc, lo_mask):
    sq = xc * xc
    lo = jnp.sum(jnp.where(lo_mask, sq, 0.0), axis=-1, keepdims=True)
    hi = jnp.sum(jnp.where(lo_mask, 0.0, sq), axis=-1, keepdims=True)
    return jnp.where(lo_mask, lax.rsqrt(lo * (1.0 / HEAD64) + EPS), lax.rsqrt(hi * (1.0 / HEAD64) + EPS))


def _rope(xc, cos, sin_signed, first_half):
    swapped = jnp.where(first_half, pltpu.roll(xc, LANES - 32, 1), pltpu.roll(xc, 32, 1))
    return xc * cos + swapped * sin_signed


def _prep_kernel(h_ref, cos_ref, sin_ref, ng_ref, wa_ref, wuq_ref, wuk_ref, wuv_ref, gt_ref, gcq_ref, gckv_ref,
                 qda_ref, ktda_ref, vda_ref, qm_ref, ktm_ref, vm_ref, qg_ref, ktg_ref, vg_ref):
    x = h_ref[0]
    xn = ((x * _rms_scale(x, D_MODEL)) * ng_ref[...]).astype(BF16)
    xp = jnp.dot(xn, wa_ref[...], preferred_element_type=F32)
    cos = cos_ref[...]
    sin_s = sin_ref[...]
    lane = lax.broadcasted_iota(jnp.int32, cos.shape, 1)
    lo64 = lane < HEAD64
    first_half = (lane & 32) == 0
    s64 = (HEAD64 ** -0.5) * LOG2E
    s192 = (MLA_QK ** -0.5) * LOG2E

    def head_pair(col, gain):
        xc = xp[:, col:col + LANES]
        return _rope((xc * _pair_rms_scale(xc, lo64)) * gain, cos, sin_s, first_half)

    g_q, g_k = gt_ref[0:1, :], gt_ref[1:2, :]
    for c in range(4):
        sl = slice(c * LANES, (c + 1) * LANES)
        qda_ref[0, :, sl] = (head_pair(A_DAQ + c * LANES, g_q) * s64).astype(BF16)
        ktda_ref[0, sl, :] = head_pair(A_DAK + c * LANES, g_k).T.astype(BF16)
    vda_ref[0] = xp[:, A_DAV:A_DAV + 512].astype(BF16)

    cq = xp[:, A_CQ:A_CQ + MLA_Q_RANK]
    cqn = ((cq * _rms_scale(cq, MLA_Q_RANK)) * gcq_ref[...]).astype(BF16)
    qm = jnp.dot(cqn, wuq_ref[...], preferred_element_type=F32)
    ckv = xp[:, A_CKV:A_CKV + MLA_KV_RANK]
    ckvn = ((ckv * _rms_scale(ckv, MLA_KV_RANK)) * gckv_ref[...]).astype(BF16)
    kn_all = jnp.dot(ckvn, wuk_ref[...], preferred_element_type=F32)
    vm_ref[0] = jnp.dot(ckvn, wuv_ref[...], preferred_element_type=F32).astype(BF16)
    kr = xp[:, A_KR:A_KR + LANES]
    kr_ss = jnp.sum(kr * kr, axis=-1, keepdims=True)
    gq_n, gq_r, gk_n, gk_r = gt_ref[4:5, :], gt_ref[5:6, :], gt_ref[6:7, :], gt_ref[7:8, :]
    for hd in range(MLA_HEADS):
        b = hd * 2 * LANES
        qn = qm[:, b:b + LANES]
        qr = qm[:, b + LANES:b + 2 * LANES]
        r = lax.rsqrt((jnp.sum(qn * qn, axis=-1, keepdims=True) + jnp.sum(qr * qr, axis=-1, keepdims=True))
                      * (1.0 / MLA_QK) + EPS)
        qm_ref[0, :, b:b + LANES] = (((qn * r) * gq_n) * s192).astype(BF16)
        qm_ref[0, :, b + LANES:b + 2 * LANES] = (_rope((qr * r) * gq_r, cos, sin_s, first_half) * s192).astype(BF16)
        kn = kn_all[:, hd * LANES:(hd + 1) * LANES]
        rk = lax.rsqrt((jnp.sum(kn * kn, axis=-1, keepdims=True) + kr_ss) * (1.0 / MLA_QK) + EPS)
        ktm_ref[0, b:b + LANES, :] = ((kn * rk) * gk_n).T.astype(BF16)
        ktm_ref[0, b + LANES:b + 2 * LANES, :] = _rope((kr * rk) * gk_r, cos, sin_s, first_half).T.astype(BF16)

    g_q, g_k = gt_ref[2:3, :], gt_ref[3:4, :]
    for c in range(4):
        sl = slice(c * LANES, (c + 1) * LANES)
        qg_ref[0, :, sl] = (head_pair(A_GQ + c * LANES, g_q) * s64).astype(BF16)
    for c in range(2):
        sl = slice(c * LANES, (c + 1) * LANES)
        ktg_ref[0, sl, :] = head_pair(A_GK + c * LANES, g_k).T.astype(BF16)
    vg_ref[0] = xp[:, A_GV:A_GV + 256].astype(BF16)


def _prep_call(h, cos, sin, ng, wa, wuq, wuk, wuv, gt, gcq, gckv):
    B, Lp, D = h.shape
    tm = _pick(Lp, (384, 128))
    row = lambda b, j: (b, j, 0)
    colt = lambda b, j: (b, 0, j)
    tab = pl.BlockSpec((tm, LANES), lambda b, j: (j, 0))
    sds = lambda *s: jax.ShapeDtypeStruct(s, BF16)
    out_shape = [sds(B, Lp, 512), sds(B, 512, Lp), sds(B, Lp, 512),
                 sds(B, Lp, 1024), sds(B, 1024, Lp), sds(B, Lp, 512),
                 sds(B, Lp, 512), sds(B, 256, Lp), sds(B, Lp, 256)]
    out_specs = [pl.BlockSpec((1, tm, 512), row), pl.BlockSpec((1, 512, tm), colt), pl.BlockSpec((1, tm, 512), row),
                 pl.BlockSpec((1, tm, 1024), row), pl.BlockSpec((1, 1024, tm), colt), pl.BlockSpec((1, tm, 512), row),
                 pl.BlockSpec((1, tm, 512), row), pl.BlockSpec((1, 256, tm), colt), pl.BlockSpec((1, tm, 256), row)]
    in_specs = [pl.BlockSpec((1, tm, D), row), tab, tab, _resident(ng.shape), _resident(wa.shape),
                _resident(wuq.shape), _resident(wuk.shape), _resident(wuv.shape), _resident(gt.shape),
                _resident(gcq.shape), _resident(gckv.shape)]
    return pl.pallas_call(
        _prep_kernel, out_shape=out_shape, grid=(B, Lp // tm), in_specs=in_specs, out_specs=out_specs,
        compiler_params=_cparams(("parallel", "parallel")), name="prep",
    )(h, cos, sin, ng, wa, wuq, wuk, wuv, gt, gcq, gckv)


def _dense_attn_kernel(*refs, n_real, tk, differential, lam_init):
    if differential:
        q_ref, kt_ref, v_ref, lam_ref, sg_ref, o_ref = refs
    else:
        q_ref, kt_ref, v_ref, o_ref = refs
    tq = q_ref.shape[1]
    dv = v_ref.shape[2]
    q = q_ref[0]
    if differential:
        qf = q.astype(F32)
        lane = lax.broadcasted_iota(jnp.int32, qf.shape, 1)
        qs = (jnp.where(lane < HEAD64, qf, 0.0).astype(BF16), jnp.where(lane < HEAD64, 0.0, qf).astype(BF16))
    else:
        qs = (q,)

    def update(carry, kc, vc, mask):
        out = []
        for qj, (m, l, acc) in zip(qs, carry):
            s = jnp.dot(qj, kc, preferred_element_type=F32)
            if mask is not None:
                s = jnp.where(mask, s, NEG)
            m_new = jnp.maximum(m, jnp.max(s, axis=-1, keepdims=True))
            p = jnp.exp2(s - m_new)
            alpha = jnp.exp2(m - m_new)
            l = alpha * l + jnp.sum(p, axis=-1, keepdims=True)
            acc = alpha * acc + jnp.dot(p.astype(BF16), vc, preferred_element_type=F32)
            out.append((m_new, l, acc))
        return tuple(out)

    def body(c, carry):
        st = pl.multiple_of(c * tk, tk)
        return update(carry, kt_ref[0, :, pl.ds(st, tk)], v_ref[0, pl.ds(st, tk), :], None)

    init = tuple((jnp.full((tq, 1), NEG, F32), jnp.zeros((tq, 1), F32), jnp.zeros((tq, dv), F32)) for _ in qs)
    carry = lax.fori_loop(0, n_real // tk, body, init)
    meta = lax.broadcasted_iota(jnp.int32, (tq, TAIL), 1) < N_META
    carry = update(carry, kt_ref[0, :, n_real:n_real + TAIL], v_ref[0, n_real:n_real + TAIL, :], meta)

    if differential:
        (_, l1, a1), (_, l2, a2) = carry
        lt = lam_ref[...]
        lam = (jnp.exp(jnp.sum(lt[0:1, :] * lt[1:2, :], axis=-1, keepdims=True))
               - jnp.exp(jnp.sum(lt[2:3, :] * lt[3:4, :], axis=-1, keepdims=True)) + lam_init)
        o = a1 * (1.0 / l1) - lam * (a2 * (1.0 / l2))
        o = ((o * _rms_scale(o, dv)) * sg_ref[...]) * (1.0 - lam_init)
    else:
        (_, l1, a1), = carry
        o = a1 * (1.0 / l1)
    o_ref[0] = o.astype(o_ref.dtype)


def _dense_attn_call(q, kt, v, n_real, *, differential, lam_tab=None, sub_g=None, lam_init=0.0, name):
    B, Lp, qw = q.shape
    heads = v.shape[2] // LANES
    dq = qw // heads
    tq = _pick(Lp, (384, 128))
    tk = _pick(n_real, (512, 256, 128))
    in_specs = [pl.BlockSpec((1, tq, dq), lambda b, h, i: (b, i, h)),
                pl.BlockSpec((1, dq, Lp), lambda b, h, i: (b, h, 0)),
                pl.BlockSpec((1, Lp, LANES), lambda b, h, i: (b, 0, h))]
    args = [q, kt, v]
    if differential:
        in_specs += [_resident(lam_tab.shape), _resident(sub_g.shape)]
        args += [lam_tab, sub_g]
    kern = functools.partial(_dense_attn_kernel, n_real=n_real, tk=tk, differential=differential, lam_init=lam_init)
    return pl.pallas_call(
        kern, out_shape=jax.ShapeDtypeStruct((B, Lp, heads * LANES), BF16), grid=(B, heads, Lp // tq),
        in_specs=in_specs, out_specs=pl.BlockSpec((1, tq, LANES), lambda b, h, i: (b, i, h)),
        compiler_params=_cparams(("parallel", "parallel", "parallel")), name=name,
    )(*args)


def _window_attn_kernel(sink_ref, q_ref, kp_ref, ko_ref, kn_ref, km_ref, vp_ref, vo_ref, vn_ref, vm_ref, o_ref, *, n_real):
    n = pl.program_id(1)
    n_blocks = n_real // BLOCK
    is_meta = n == n_blocks
    kcat = jnp.concatenate([kp_ref[0], ko_ref[0], kn_ref[0], km_ref[0]], axis=1)
    vcat = jnp.concatenate([vp_ref[0], vo_ref[0], vn_ref[0], vm_ref[0]], axis=0)
    shape = (BLOCK, 4 * BLOCK)
    qi = lax.broadcasted_iota(jnp.int32, shape, 0)
    kj = lax.broadcasted_iota(jnp.int32, shape, 1)
    seg = kj // BLOCK
    j = kj - seg * BLOCK
    tail_ok = jnp.logical_and(seg == 3, j < N_META)
    kpos = (n - 1) * BLOCK + kj
    dist = kpos - (n * BLOCK + qi)
    band = jnp.logical_and(jnp.logical_and(dist <= WINDOW, dist >= -WINDOW),
                           jnp.logical_and(kpos >= 0, kpos < n_real))
    vis_real = jnp.logical_or(jnp.logical_and(seg < 3, band), tail_ok)
    vis_meta = jnp.logical_or(jnp.logical_and(seg == 2, j <= qi + (WINDOW - N_META)), tail_ok)
    bias = jnp.where(is_meta, jnp.where(vis_meta, 0.0, NEG), jnp.where(vis_real, 0.0, NEG))
    lane = lax.broadcasted_iota(jnp.int32, (BLOCK, LANES), 1)
    lo64 = lane < HEAD64
    for c in range(GQA_HEADS // 2):
        qc = q_ref[0, :, c * LANES:(c + 1) * LANES].astype(F32)
        halves = []
        for half in range(2):
            hd = 2 * c + half
            g = hd // GQA_GROUP
            qz = (jnp.where(lo64, qc, 0.0) if half == 0 else jnp.where(lo64, 0.0, qc)).astype(BF16)
            s = jnp.dot(qz, kcat[g * LANES:(g + 1) * LANES, :], preferred_element_type=F32)
            s = s + bias
            sink = sink_ref[hd] * LOG2E
            m = jnp.maximum(jnp.max(s, axis=-1, keepdims=True), sink)
            p = jnp.exp2(s - m)
            denom = jnp.sum(p, axis=-1, keepdims=True) + jnp.exp2(sink - m)
            o = jnp.dot(p.astype(BF16), vcat[:, g * LANES:(g + 1) * LANES], preferred_element_type=F32)
            halves.append(o * (1.0 / denom))
        o_ref[0, :, c * LANES:(c + 1) * LANES] = jnp.where(lo64, halves[0], halves[1]).astype(o_ref.dtype)


def _window_attn_call(q, kt, v, sink, n_real):
    B, Lp, _ = q.shape
    nb = n_real // BLOCK
    prev = lambda b, n, s: jnp.maximum(n - 1, 0)
    nxt = lambda b, n, s: jnp.where(n == nb, 0, n + 1)
    kspec = lambda f: pl.BlockSpec((1, 256, BLOCK), lambda b, n, s: (b, 0, f(b, n, s)))
    vspec = lambda f: pl.BlockSpec((1, BLOCK, 256), lambda b, n, s: (b, f(b, n, s), 0))
    own = lambda b, n, s: n
    tail = lambda b, n, s: nb
    grid_spec = pltpu.PrefetchScalarGridSpec(
        num_scalar_prefetch=1, grid=(B, nb + 1),
        in_specs=[pl.BlockSpec((1, BLOCK, 512), lambda b, n, s: (b, n, 0)),
                  kspec(prev), kspec(own), kspec(nxt), kspec(tail),
                  vspec(prev), vspec(own), vspec(nxt), vspec(tail)],
        out_specs=pl.BlockSpec((1, BLOCK, 512), lambda b, n, s: (b, n, 0)))
    return pl.pallas_call(
        functools.partial(_window_attn_kernel, n_real=n_real),
        out_shape=jax.ShapeDtypeStruct((B, Lp, 512), BF16), grid_spec=grid_spec,
        compiler_params=_cparams(("parallel", "parallel")), name="window_attn",
    )(sink, q, kt, kt, kt, kt, v, v, v, v)


def _merge_kernel(h_ref, oda_ref, om_ref, og_ref, ng_ref, wg_ref, wb_ref, wo_ref, out_ref):
    x = h_ref[0]
    xn = ((x * _rms_scale(x, D_MODEL)) * ng_ref[...]).astype(BF16)
    gl = jnp.dot(xn, wg_ref[...], preferred_element_type=F32)
    merged = None
    for g, o_ref in enumerate((oda_ref, om_ref, og_ref)):
        proj = jnp.dot(o_ref[0], wb_ref[g], preferred_element_type=F32)
        term = jax.nn.sigmoid(gl[:, g * D_MODEL:(g + 1) * D_MODEL]) * proj
        merged = term if merged is None else merged + term
    out_ref[0] = x + jnp.dot(merged.astype(BF16), wo_ref[...], preferred_element_type=F32)


def _merge_call(h, oda, om, og, ng, wg, wb, wo):
    B, Lp, D = h.shape
    tm = _pick(Lp, (384, 128))
    row = lambda b, j: (b, j, 0)
    in_specs = [pl.BlockSpec((1, tm, D), row)] + [pl.BlockSpec((1, tm, BRANCH_WIDTH), row)] * 3 + [
        _resident(ng.shape), _resident(wg.shape), _resident(wb.shape), _resident(wo.shape)]
    return pl.pallas_call(
        _merge_kernel, out_shape=jax.ShapeDtypeStruct(h.shape, F32), grid=(B, Lp // tm),
        in_specs=in_specs, out_specs=pl.BlockSpec((1, tm, D), row),
        compiler_params=_cparams(("parallel", "parallel")), name="merge",
    )(h, oda, om, og, ng, wg, wb, wo)


def _mlp_kernel(h_ref, ng_ref, wu_ref, wd_ref, out_ref):
    x = h_ref[0]
    xn = ((x * _rms_scale(x, D_MODEL)) * ng_ref[...]).astype(BF16)
    u = jnp.dot(xn, wu_ref[...], preferred_element_type=F32)
    a = jnp.square(jnp.maximum(u, 0.0)).astype(BF16)
    out_ref[0] = x + jnp.dot(a, wd_ref[...], preferred_element_type=F32)


def _mlp_call(h, ng, wu, wd):
    B, Lp, D = h.shape
    tm = _pick(Lp, (384, 128))
    row = lambda b, j: (b, j, 0)
    return pl.pallas_call(
        _mlp_kernel, out_shape=jax.ShapeDtypeStruct(h.shape, F32), grid=(B, Lp // tm),
        in_specs=[pl.BlockSpec((1, tm, D), row), _resident(ng.shape), _resident(wu.shape), _resident(wd.shape)],
        out_specs=pl.BlockSpec((1, tm, D), row),
        compiler_params=_cparams(("parallel", "parallel")), name="mlp",
    )(h, ng, wu, wd)


def _pair_gain(g):
    return jnp.tile(g.astype(F32), 2)[None, :]


def _pad_gain(g):
    return jnp.concatenate([g.astype(F32), jnp.zeros((LANES - g.shape[0],), F32)])[None, :]


def _layer_params(l, attn_norm_g, w_in, da_q_norm_g, da_k_norm_g, da_lam_q1, da_lam_k1, da_lam_q2, da_lam_k2,
                  da_subln_g, mla_cq_norm_g, mla_ckv_norm_g, mla_w_uq, mla_w_ukv, mla_q_norm_g, mla_k_norm_g,
                  gqa_q_norm_g, gqa_k_norm_g, gqa_sink, w_branch, w_out, mlp_norm_g, w_up, w_down):
    w = w_in[l]
    sec = [w[:, a:b] for a, b in zip(IN_OFF[:-1], IN_OFF[1:])]
    da_q, da_k, da_v, cq, ckv, kr, gq, gk, gv, gate = sec
    dup = lambda m: jnp.concatenate([m[:, :HEAD64], m[:, :HEAD64], m[:, HEAD64:], m[:, HEAD64:]], axis=1)
    wa = jnp.concatenate([da_q, da_k, da_v, cq, ckv, kr, jnp.zeros((D_MODEL, LANES - HEAD64), w.dtype),
                          gq, dup(gk), dup(gv)], axis=1).astype(BF16)
    wuq = mla_w_uq[l].reshape(MLA_Q_RANK, MLA_HEADS, MLA_QK)
    wuq = jnp.pad(wuq, ((0, 0), (0, 0), (0, 2 * LANES - MLA_QK))).reshape(MLA_Q_RANK, MLA_HEADS * 2 * LANES).astype(BF16)
    wukv = mla_w_ukv[l].reshape(MLA_KV_RANK, MLA_HEADS, 2 * LANES)
    wuk = wukv[:, :, :MLA_NOPE].reshape(MLA_KV_RANK, MLA_HEADS * LANES).astype(BF16)
    wuv = wukv[:, :, MLA_NOPE:].reshape(MLA_KV_RANK, MLA_HEADS * LANES).astype(BF16)
    mq, mk = mla_q_norm_g[l], mla_k_norm_g[l]
    gt = jnp.concatenate([_pair_gain(da_q_norm_g[l]), _pair_gain(da_k_norm_g[l]),
                          _pair_gain(gqa_q_norm_g[l]), _pair_gain(gqa_k_norm_g[l]),
                          mq[None, :MLA_NOPE].astype(F32), _pad_gain(mq[MLA_NOPE:]),
                          mk[None, :MLA_NOPE].astype(F32), _pad_gain(mk[MLA_NOPE:])], axis=0)
    lam_tab = jnp.concatenate([_pad_gain(v[l]) for v in (da_lam_q1, da_lam_k1, da_lam_q2, da_lam_k2)], axis=0)
    return dict(
        ng=attn_norm_g[l][None, :].astype(F32), wa=wa, wuq=wuq, wuk=wuk, wuv=wuv, gt=gt,
        gcq=mla_cq_norm_g[l][None, :].astype(F32), gckv=mla_ckv_norm_g[l][None, :].astype(F32),
        lam_tab=lam_tab, sub_g=da_subln_g[l][None, :].astype(F32), sink=gqa_sink[l].astype(F32),
        wg=gate.astype(BF16), wb=w_branch[l].astype(BF16), wo=w_out[l].astype(BF16),
        mg=mlp_norm_g[l][None, :].astype(F32), wu=w_up[l].astype(BF16), wd=w_down[l].astype(BF16))


def _rope_tables(n_real):
    r = jnp.arange(n_real + TAIL, dtype=jnp.int32)
    pos = jnp.where(r < n_real, r + N_META, jnp.where(r < n_real + N_META, r - n_real, 0)).astype(F32)
    inv_freq = 1.0 / (ROPE_THETA ** (jnp.arange(0, HEAD64, 2, dtype=F32) / HEAD64))
    ang = pos[:, None] * inv_freq[None, :]
    cos, sin = jnp.cos(ang), jnp.sin(ang)
    return jnp.tile(cos, (1, 4)), jnp.tile(jnp.concatenate([-sin, sin], axis=1), (1, 2))


def _trunk(x, meta_tokens, layers):
    B, S, D = x.shape
    assert D == D_MODEL and S % BLOCK == 0
    meta = jnp.broadcast_to(meta_tokens[None].astype(x.dtype), (B, N_META, D))
    h = jnp.concatenate([x, meta, jnp.zeros((B, TAIL - N_META, D), x.dtype)], axis=1)
    cos, sin = _rope_tables(S)
    for l, p in enumerate(layers):
        lam_init = 0.8 - 0.6 * math.exp(-0.3 * l)
        qda, ktda, vda, qm, ktm, vm, qg, ktg, vg = _prep_call(
            h, cos, sin, p["ng"], p["wa"], p["wuq"], p["wuk"], p["wuv"], p["gt"], p["gcq"], p["gckv"])
        oda = _dense_attn_call(qda, ktda, vda, S, differential=True, lam_tab=p["lam_tab"], sub_g=p["sub_g"],
                               lam_init=lam_init, name="diff_attn")
        om = _dense_attn_call(qm, ktm, vm, S, differential=False, name="latent_attn")
        og = _window_attn_call(qg, ktg, vg, p["sink"], S)
        h = _merge_call(h, oda, om, og, p["ng"], p["wg"], p["wb"], p["wo"])
        h = _mlp_call(h, p["mg"], p["wu"], p["wd"])
    return h[:, :S]


def kernel(x_prompt, x_sample, meta_tokens, attn_norm_g, w_in, da_q_norm_g, da_k_norm_g, da_lam_q1, da_lam_k1, da_lam_q2, da_lam_k2, da_subln_g, mla_cq_norm_g, mla_ckv_norm_g, mla_w_uq, mla_w_ukv, mla_q_norm_g, mla_k_norm_g, gqa_q_norm_g, gqa_k_norm_g, gqa_sink, w_branch, w_out, mlp_norm_g, w_up, w_down):
    depth = w_in.shape[0]
    layers = [_layer_params(l, attn_norm_g, w_in, da_q_norm_g, da_k_norm_g, da_lam_q1, da_lam_k1, da_lam_q2,
                            da_lam_k2, da_subln_g, mla_cq_norm_g, mla_ckv_norm_g, mla_w_uq, mla_w_ukv,
                            mla_q_norm_g, mla_k_norm_g, gqa_q_norm_g, gqa_k_norm_g, gqa_sink, w_branch, w_out,
                            mlp_norm_g, w_up, w_down) for l in range(depth)]
    return (_trunk(x_prompt, meta_tokens, layers), _trunk(x_sample, meta_tokens, layers))
```

```python
import functools
import math

import jax
import jax.numpy as jnp
from jax import lax
from jax.experimental import pallas as pl
from jax.experimental.pallas import tpu as pltpu

F32 = jnp.float32
BF16 = jnp.bfloat16

D_MODEL = 1024
N_META = 16
BLOCK = 128
WINDOW = 128
ROPE_THETA = 10000.0
EPS = 1e-6
NEG = -1e30
LOG2E = 1.4426950408889634

HEAD64 = 64
MLA_HEADS = 4
MLA_NOPE = 128
MLA_QK = 192
MLA_Q_RANK = 384
MLA_KV_RANK = 256
GQA_HEADS = 8
GQA_GROUP = 4
BRANCH_WIDTH = 512
N_BRANCH = 3
D_FF = 4 * D_MODEL
LANES = 128
TAIL = 128
ONES_ROWS = 16
VT_ROWS = LANES + ONES_ROWS

A_DAQ, A_DAK, A_DAV, A_CQ, A_CKV, A_KR, A_GQ, A_GK, A_GV, A_END = (
    0, 512, 1024, 1536, 1920, 2176, 2304, 2816, 3072, 3328)
IN_OFF = (0, 512, 1024, 1536, 1920, 2176, 2240, 2752, 2880, 3008, 6080)

VMEM_LIMIT = 56 * 1024 * 1024


def _pick(n, candidates):
    for c in candidates:
        if n % c == 0:
            return c
    raise ValueError(f"no tile in {candidates} divides {n}")


def _cparams(sem):
    return pltpu.CompilerParams(dimension_semantics=sem, vmem_limit_bytes=VMEM_LIMIT)


def _resident(shape):
    zeros = (0,) * len(shape)
    return pl.BlockSpec(shape, lambda *_: zeros, pipeline_mode=pl.Buffered(1))


def _rms_scale(x, width):
    return lax.rsqrt(jnp.sum(x * x, axis=-1, keepdims=True) * (1.0 / width) + EPS)


def _pair_rms_scale(xc, lo_mask):
    sq = xc * xc
    lo = jnp.sum(jnp.where(lo_mask, sq, 0.0), axis=-1, keepdims=True)
    hi = jnp.sum(jnp.where(lo_mask, 0.0, sq), axis=-1, keepdims=True)
    return jnp.where(lo_mask, lax.rsqrt(lo * (1.0 / HEAD64) + EPS), lax.rsqrt(hi * (1.0 / HEAD64) + EPS))


def _rope(xc, cos, sin_signed, first_half):
    swapped = jnp.where(first_half, pltpu.roll(xc, LANES - 32, 1), pltpu.roll(xc, 32, 1))
    return xc * cos + swapped * sin_signed


def _prep_kernel(h_ref, cos_ref, sin_ref, ng_ref, wa_ref, wuq_ref, wuk_ref, wuv_ref, gt_ref, gcq_ref, gckv_ref,
                 qtda_ref, kda_ref, vtda_ref, qtm_ref, km_ref, vtm_ref, qg_ref, ktg_ref, vg_ref):
    x = h_ref[0]
    xn = ((x * _rms_scale(x, D_MODEL)) * ng_ref[...]).astype(BF16)
    xp = jnp.dot(xn, wa_ref[...], preferred_element_type=F32)
    cos = cos_ref[...]
    sin_s = sin_ref[...]
    lane = lax.broadcasted_iota(jnp.int32, cos.shape, 1)
    lo64 = lane < HEAD64
    first_half = (lane & 32) == 0
    s64 = (HEAD64 ** -0.5) * LOG2E
    s192 = (MLA_QK ** -0.5) * LOG2E

    def head_pair(col, gain):
        xc = xp[:, col:col + LANES]
        return _rope((xc * _pair_rms_scale(xc, lo64)) * gain, cos, sin_s, first_half)

    ones_rows = jnp.ones((ONES_ROWS, x.shape[0]), BF16)

    def store_vt(vt_ref, hd, v_slab):
        vt_ref[0, hd * VT_ROWS:hd * VT_ROWS + LANES, :] = v_slab.T.astype(BF16)
        vt_ref[0, hd * VT_ROWS + LANES:(hd + 1) * VT_ROWS, :] = ones_rows

    g_q, g_k = gt_ref[0:1, :], gt_ref[1:2, :]
    for c in range(4):
        sl = slice(c * LANES, (c + 1) * LANES)
        qtda_ref[0, sl, :] = (head_pair(A_DAQ + c * LANES, g_q) * s64).T.astype(BF16)
        kda_ref[0, :, sl] = head_pair(A_DAK + c * LANES, g_k).astype(BF16)
        store_vt(vtda_ref, c, xp[:, A_DAV + c * LANES:A_DAV + (c + 1) * LANES])

    cq = xp[:, A_CQ:A_CQ + MLA_Q_RANK]
    cqn = ((cq * _rms_scale(cq, MLA_Q_RANK)) * gcq_ref[...]).astype(BF16)
    qm = jnp.dot(cqn, wuq_ref[...], preferred_element_type=F32)
    ckv = xp[:, A_CKV:A_CKV + MLA_KV_RANK]
    ckvn = ((ckv * _rms_scale(ckv, MLA_KV_RANK)) * gckv_ref[...]).astype(BF16)
    kn_all = jnp.dot(ckvn, wuk_ref[...], preferred_element_type=F32)
    vm_all = jnp.dot(ckvn, wuv_ref[...], preferred_element_type=F32)
    kr = xp[:, A_KR:A_KR + LANES]
    kr_ss = jnp.sum(kr * kr, axis=-1, keepdims=True)
    gq_n, gq_r, gk_n, gk_r = gt_ref[4:5, :], gt_ref[5:6, :], gt_ref[6:7, :], gt_ref[7:8, :]
    for hd in range(MLA_HEADS):
        b = hd * 2 * LANES
        qn = qm[:, b:b + LANES]
        qr = qm[:, b + LANES:b + 2 * LANES]
        r = lax.rsqrt((jnp.sum(qn * qn, axis=-1, keepdims=True) + jnp.sum(qr * qr, axis=-1, keepdims=True))
                      * (1.0 / MLA_QK) + EPS)
        qtm_ref[0, b:b + LANES, :] = (((qn * r) * gq_n) * s192).T.astype(BF16)
        qtm_ref[0, b + LANES:b + 2 * LANES, :] = (_rope((qr * r) * gq_r, cos, sin_s, first_half) * s192).T.astype(BF16)
        kn = kn_all[:, hd * LANES:(hd + 1) * LANES]
        rk = lax.rsqrt((jnp.sum(kn * kn, axis=-1, keepdims=True) + kr_ss) * (1.0 / MLA_QK) + EPS)
        km_ref[0, :, b:b + LANES] = ((kn * rk) * gk_n).astype(BF16)
        km_ref[0, :, b + LANES:b + 2 * LANES] = _rope((kr * rk) * gk_r, cos, sin_s, first_half).astype(BF16)
        store_vt(vtm_ref, hd, vm_all[:, hd * LANES:(hd + 1) * LANES])

    g_q, g_k = gt_ref[2:3, :], gt_ref[3:4, :]
    for c in range(4):
        sl = slice(c * LANES, (c + 1) * LANES)
        qg_ref[0, :, sl] = (head_pair(A_GQ + c * LANES, g_q) * s64).astype(BF16)
    for c in range(2):
        sl = slice(c * LANES, (c + 1) * LANES)
        ktg_ref[0, sl, :] = head_pair(A_GK + c * LANES, g_k).T.astype(BF16)
    vg_ref[0] = xp[:, A_GV:A_GV + 256].astype(BF16)


def _prep_call(h, cos, sin, ng, wa, wuq, wuk, wuv, gt, gcq, gckv):
    B, Lp, D = h.shape
    tm = _pick(Lp, (384, 128))
    row = lambda b, j: (b, j, 0)
    colt = lambda b, j: (b, 0, j)
    tab = pl.BlockSpec((tm, LANES), lambda b, j: (j, 0))
    sds = lambda *s: jax.ShapeDtypeStruct(s, BF16)
    vt_rows = 4 * VT_ROWS
    out_shape = [sds(B, 512, Lp), sds(B, Lp, 512), sds(B, vt_rows, Lp),
                 sds(B, 1024, Lp), sds(B, Lp, 1024), sds(B, vt_rows, Lp),
                 sds(B, Lp, 512), sds(B, 256, Lp), sds(B, Lp, 256)]
    out_specs = [pl.BlockSpec((1, 512, tm), colt), pl.BlockSpec((1, tm, 512), row), pl.BlockSpec((1, vt_rows, tm), colt),
                 pl.BlockSpec((1, 1024, tm), colt), pl.BlockSpec((1, tm, 1024), row), pl.BlockSpec((1, vt_rows, tm), colt),
                 pl.BlockSpec((1, tm, 512), row), pl.BlockSpec((1, 256, tm), colt), pl.BlockSpec((1, tm, 256), row)]
    in_specs = [pl.BlockSpec((1, tm, D), row), tab, tab, _resident(ng.shape), _resident(wa.shape),
                _resident(wuq.shape), _resident(wuk.shape), _resident(wuv.shape), _resident(gt.shape),
                _resident(gcq.shape), _resident(gckv.shape)]
    return pl.pallas_call(
        _prep_kernel, out_shape=out_shape, grid=(B, Lp // tm), in_specs=in_specs, out_specs=out_specs,
        compiler_params=_cparams(("parallel", "parallel")), name="prep",
    )(h, cos, sin, ng, wa, wuq, wuk, wuv, gt, gcq, gckv)


def _dense_attn_kernel(*refs, n_real, tk, differential, lam_init):
    if differential:
        qt_ref, k_ref, vt_ref, lam_ref, sg_ref, o_ref, s_ref = refs
    else:
        qt_ref, k_ref, vt_ref, o_ref, s_ref = refs
    tq = qt_ref.shape[2]
    dv = o_ref.shape[2]
    qt = qt_ref[0]
    if differential:
        z = jnp.zeros((HEAD64, tq), BF16)
        wq = jnp.concatenate([jnp.concatenate([qt[:HEAD64], z], axis=1),
                              jnp.concatenate([z, qt[HEAD64:]], axis=1)], axis=0)
    else:
        wq = qt
    width = wq.shape[1]

    def scores(kc):
        return jnp.dot(kc, wq, preferred_element_type=F32)

    def absorb(m, acc, s, vtc):
        m_new = jnp.maximum(m, jnp.max(s, axis=0, keepdims=True))
        p = jnp.exp2(s - m_new).astype(BF16)
        alpha = jnp.exp2(m - m_new)
        return m_new, alpha * acc + jnp.dot(vtc, p, preferred_element_type=F32)

    def kchunk(c):
        return k_ref[0, pl.ds(pl.multiple_of(c * tk, tk), tk), :]

    def vchunk(c):
        return vt_ref[0, :, pl.ds(pl.multiple_of(c * tk, tk), tk)]

    def pair(j, carry):
        m, acc = carry
        c = 2 * j
        s_ref[1] = scores(kchunk(c + 1))
        m, acc = absorb(m, acc, s_ref[0], vchunk(c))
        s_ref[0] = scores(kchunk(c + 2))
        return absorb(m, acc, s_ref[1], vchunk(c + 1))

    n_chunks = n_real // tk
    n_pairs = (n_chunks - 1) // 2
    s_ref[0] = scores(kchunk(0))
    carry = (jnp.full((1, width), NEG, F32), jnp.zeros((dv + ONES_ROWS, width), F32))
    m, acc = lax.fori_loop(0, n_pairs, pair, carry)
    c = 2 * n_pairs
    if c + 1 < n_chunks:
        s_ref[1] = scores(kchunk(c + 1))
    valid = lax.broadcasted_iota(jnp.int32, (TAIL, width), 0) < N_META
    s_tail = jnp.where(valid, scores(k_ref[0, n_real:n_real + TAIL, :]), NEG)
    m, acc = absorb(m, acc, s_ref[0], vchunk(c))
    if c + 1 < n_chunks:
        m, acc = absorb(m, acc, s_ref[1], vchunk(c + 1))
    _, acc = absorb(m, acc, s_tail, vt_ref[0, :, n_real:n_real + TAIL])

    ot = acc[:dv] * (1.0 / acc[dv:dv + 1])
    if differential:
        lt = lam_ref[...]
        lam = (jnp.exp(jnp.sum(lt[0:1, :] * lt[1:2, :], axis=-1, keepdims=True))
               - jnp.exp(jnp.sum(lt[2:3, :] * lt[3:4, :], axis=-1, keepdims=True)) + lam_init)
        o = (ot[:, :tq] - lam * ot[:, tq:]).T
        o = ((o * _rms_scale(o, dv)) * sg_ref[...]) * (1.0 - lam_init)
    else:
        o = ot.T
    o_ref[0] = o.astype(o_ref.dtype)


def _dense_attn_call(qt, k, vt, n_real, *, differential, lam_tab=None, sub_g=None, lam_init=0.0, name):
    B, Lp, kw = k.shape
    heads = vt.shape[1] // VT_ROWS
    dq = kw // heads
    tq = _pick(n_real, (512, 256, 128))
    tk = _pick(n_real, (512, 256, 128))
    in_specs = [pl.BlockSpec((1, dq, tq), lambda b, h, i: (b, h, i)),
                pl.BlockSpec((1, Lp, dq), lambda b, h, i: (b, 0, h)),
                pl.BlockSpec((1, VT_ROWS, Lp), lambda b, h, i: (b, h, 0))]
    args = [qt, k, vt]
    if differential:
        in_specs += [_resident(lam_tab.shape), _resident(sub_g.shape)]
        args += [lam_tab, sub_g]
    width = (2 if differential else 1) * tq
    kern = functools.partial(_dense_attn_kernel, n_real=n_real, tk=tk, differential=differential, lam_init=lam_init)
    return pl.pallas_call(
        kern, out_shape=jax.ShapeDtypeStruct((B, Lp, heads * LANES), BF16), grid=(B, heads, pl.cdiv(Lp, tq)),
        in_specs=in_specs, out_specs=pl.BlockSpec((1, tq, LANES), lambda b, h, i: (b, i, h)),
        scratch_shapes=[pltpu.VMEM((2, tk, width), F32)],
        compiler_params=_cparams(("parallel", "parallel", "parallel")), name=name,
    )(*args)


def _window_attn_kernel(sink_ref, q_ref, kp_ref, ko_ref, kn_ref, km_ref, vp_ref, vo_ref, vn_ref, vm_ref, o_ref, *, n_real):
    n = pl.program_id(1)
    n_blocks = n_real // BLOCK
    is_meta = n == n_blocks
    kcat = jnp.concatenate([kp_ref[0], ko_ref[0], kn_ref[0], km_ref[0]], axis=1)
    vcat = jnp.concatenate([vp_ref[0], vo_ref[0], vn_ref[0], vm_ref[0]], axis=0)
    shape = (BLOCK, 4 * BLOCK)
    qi = lax.broadcasted_iota(jnp.int32, shape, 0)
    kj = lax.broadcasted_iota(jnp.int32, shape, 1)
    seg = kj // BLOCK
    j = kj - seg * BLOCK
    tail_ok = jnp.logical_and(seg == 3, j < N_META)
    kpos = (n - 1) * BLOCK + kj
    dist = kpos - (n * BLOCK + qi)
    band = jnp.logical_and(jnp.logical_and(dist <= WINDOW, dist >= -WINDOW),
                           jnp.logical_and(kpos >= 0, kpos < n_real))
    vis_real = jnp.logical_or(jnp.logical_and(seg < 3, band), tail_ok)
    vis_meta = jnp.logical_or(jnp.logical_and(seg == 2, j <= qi + (WINDOW - N_META)), tail_ok)
    bias = jnp.where(is_meta, jnp.where(vis_meta, 0.0, NEG), jnp.where(vis_real, 0.0, NEG))
    lane = lax.broadcasted_iota(jnp.int32, (BLOCK, LANES), 1)
    lo64 = lane < HEAD64
    for c in range(GQA_HEADS // 2):
        qc = q_ref[0, :, c * LANES:(c + 1) * LANES].astype(F32)
        halves = []
        for half in range(2):
            hd = 2 * c + half
            g = hd // GQA_GROUP
            qz = (jnp.where(lo64, qc, 0.0) if half == 0 else jnp.where(lo64, 0.0, qc)).astype(BF16)
            s = jnp.dot(qz, kcat[g * LANES:(g + 1) * LANES, :], preferred_element_type=F32)
            s = s + bias
            sink = sink_ref[hd] * LOG2E
            m = jnp.maximum(jnp.max(s, axis=-1, keepdims=True), sink)
            p = jnp.exp2(s - m)
            denom = jnp.sum(p, axis=-1, keepdims=True) + jnp.exp2(sink - m)
            o = jnp.dot(p.astype(BF16), vcat[:, g * LANES:(g + 1) * LANES], preferred_element_type=F32)
            halves.append(o * (1.0 / denom))
        o_ref[0, :, c * LANES:(c + 1) * LANES] = jnp.where(lo64, halves[0], halves[1]).astype(o_ref.dtype)


def _window_attn_call(q, kt, v, sink, n_real):
    B, Lp, _ = q.shape
    nb = n_real // BLOCK
    prev = lambda b, n, s: jnp.maximum(n - 1, 0)
    nxt = lambda b, n, s: jnp.where(n == nb, 0, n + 1)
    kspec = lambda f: pl.BlockSpec((1, 256, BLOCK), lambda b, n, s: (b, 0, f(b, n, s)))
    vspec = lambda f: pl.BlockSpec((1, BLOCK, 256), lambda b, n, s: (b, f(b, n, s), 0))
    own = lambda b, n, s: n
    tail = lambda b, n, s: nb
    grid_spec = pltpu.PrefetchScalarGridSpec(
        num_scalar_prefetch=1, grid=(B, nb + 1),
        in_specs=[pl.BlockSpec((1, BLOCK, 512), lambda b, n, s: (b, n, 0)),
                  kspec(prev), kspec(own), kspec(nxt), kspec(tail),
                  vspec(prev), vspec(own), vspec(nxt), vspec(tail)],
        out_specs=pl.BlockSpec((1, BLOCK, 512), lambda b, n, s: (b, n, 0)))
    return pl.pallas_call(
        functools.partial(_window_attn_kernel, n_real=n_real),
        out_shape=jax.ShapeDtypeStruct((B, Lp, 512), BF16), grid_spec=grid_spec,
        compiler_params=_cparams(("parallel", "parallel")), name="window_attn",
    )(sink, q, kt, kt, kt, kt, v, v, v, v)


def _merge_kernel(h_ref, oda_ref, om_ref, og_ref, ng_ref, wg_ref, wb_ref, wo_ref, out_ref):
    x = h_ref[0]
    xn = ((x * _rms_scale(x, D_MODEL)) * ng_ref[...]).astype(BF16)
    gl = jnp.dot(xn, wg_ref[...], preferred_element_type=F32)
    merged = None
    for g, o_ref in enumerate((oda_ref, om_ref, og_ref)):
        proj = jnp.dot(o_ref[0], wb_ref[g], preferred_element_type=F32)
        term = jax.nn.sigmoid(gl[:, g * D_MODEL:(g + 1) * D_MODEL]) * proj
        merged = term if merged is None else merged + term
    out_ref[0] = x + jnp.dot(merged.astype(BF16), wo_ref[...], preferred_element_type=F32)


def _merge_call(h, oda, om, og, ng, wg, wb, wo):
    B, Lp, D = h.shape
    tm = _pick(Lp, (384, 128))
    row = lambda b, j: (b, j, 0)
    in_specs = [pl.BlockSpec((1, tm, D), row)] + [pl.BlockSpec((1, tm, BRANCH_WIDTH), row)] * 3 + [
        _resident(ng.shape), _resident(wg.shape), _resident(wb.shape), _resident(wo.shape)]
    return pl.pallas_call(
        _merge_kernel, out_shape=jax.ShapeDtypeStruct(h.shape, F32), grid=(B, Lp // tm),
        in_specs=in_specs, out_specs=pl.BlockSpec((1, tm, D), row),
        compiler_params=_cparams(("parallel", "parallel")), name="merge",
    )(h, oda, om, og, ng, wg, wb, wo)


def _mlp_kernel(h_ref, ng_ref, wu_ref, wd_ref, out_ref):
    x = h_ref[0]
    xn = ((x * _rms_scale(x, D_MODEL)) * ng_ref[...]).astype(BF16)
    u = jnp.dot(xn, wu_ref[...], preferred_element_type=F32)
    a = jnp.square(jnp.maximum(u, 0.0)).astype(BF16)
    out_ref[0] = x + jnp.dot(a, wd_ref[...], preferred_element_type=F32)


def _mlp_call(h, ng, wu, wd):
    B, Lp, D = h.shape
    tm = _pick(Lp, (384, 128))
    row = lambda b, j: (b, j, 0)
    return pl.pallas_call(
        _mlp_kernel, out_shape=jax.ShapeDtypeStruct(h.shape, F32), grid=(B, Lp // tm),
        in_specs=[pl.BlockSpec((1, tm, D), row), _resident(ng.shape), _resident(wu.shape), _resident(wd.shape)],
        out_specs=pl.BlockSpec((1, tm, D), row),
        compiler_params=_cparams(("parallel", "parallel")), name="mlp",
    )(h, ng, wu, wd)


def _pair_gain(g):
    return jnp.tile(g.astype(F32), 2)[None, :]


def _pad_gain(g):
    return jnp.concatenate([g.astype(F32), jnp.zeros((LANES - g.shape[0],), F32)])[None, :]


def _layer_params(l, attn_norm_g, w_in, da_q_norm_g, da_k_norm_g, da_lam_q1, da_lam_k1, da_lam_q2, da_lam_k2,
                  da_subln_g, mla_cq_norm_g, mla_ckv_norm_g, mla_w_uq, mla_w_ukv, mla_q_norm_g, mla_k_norm_g,
                  gqa_q_norm_g, gqa_k_norm_g, gqa_sink, w_branch, w_out, mlp_norm_g, w_up, w_down):
    w = w_in[l]
    sec = [w[:, a:b] for a, b in zip(IN_OFF[:-1], IN_OFF[1:])]
    da_q, da_k, da_v, cq, ckv, kr, gq, gk, gv, gate = sec
    dup = lambda m: jnp.concatenate([m[:, :HEAD64], m[:, :HEAD64], m[:, HEAD64:], m[:, HEAD64:]], axis=1)
    wa = jnp.concatenate([da_q, da_k, da_v, cq, ckv, kr, jnp.zeros((D_MODEL, LANES - HEAD64), w.dtype),
                          gq, dup(gk), dup(gv)], axis=1).astype(BF16)
    wuq = mla_w_uq[l].reshape(MLA_Q_RANK, MLA_HEADS, MLA_QK)
    wuq = jnp.pad(wuq, ((0, 0), (0, 0), (0, 2 * LANES - MLA_QK))).reshape(MLA_Q_RANK, MLA_HEADS * 2 * LANES).astype(BF16)
    wukv = mla_w_ukv[l].reshape(MLA_KV_RANK, MLA_HEADS, 2 * LANES)
    wuk = wukv[:, :, :MLA_NOPE].reshape(MLA_KV_RANK, MLA_HEADS * LANES).astype(BF16)
    wuv = wukv[:, :, MLA_NOPE:].reshape(MLA_KV_RANK, MLA_HEADS * LANES).astype(BF16)
    mq, mk = mla_q_norm_g[l], mla_k_norm_g[l]
    gt = jnp.concatenate([_pair_gain(da_q_norm_g[l]), _pair_gain(da_k_norm_g[l]),
                          _pair_gain(gqa_q_norm_g[l]), _pair_gain(gqa_k_norm_g[l]),
                          mq[None, :MLA_NOPE].astype(F32), _pad_gain(mq[MLA_NOPE:]),
                          mk[None, :MLA_NOPE].astype(F32), _pad_gain(mk[MLA_NOPE:])], axis=0)
    lam_tab = jnp.concatenate([_pad_gain(v[l]) for v in (da_lam_q1, da_lam_k1, da_lam_q2, da_lam_k2)], axis=0)
    return dict(
        ng=attn_norm_g[l][None, :].astype(F32), wa=wa, wuq=wuq, wuk=wuk, wuv=wuv, gt=gt,
        gcq=mla_cq_norm_g[l][None, :].astype(F32), gckv=mla_ckv_norm_g[l][None, :].astype(F32),
        lam_tab=lam_tab, sub_g=da_subln_g[l][None, :].astype(F32), sink=gqa_sink[l].astype(F32),
        wg=gate.astype(BF16), wb=w_branch[l].astype(BF16), wo=w_out[l].astype(BF16),
        mg=mlp_norm_g[l][None, :].astype(F32), wu=w_up[l].astype(BF16), wd=w_down[l].astype(BF16))


def _rope_tables(n_real):
    r = jnp.arange(n_real + TAIL, dtype=jnp.int32)
    pos = jnp.where(r < n_real, r + N_META, jnp.where(r < n_real + N_META, r - n_real, 0)).astype(F32)
    inv_freq = 1.0 / (ROPE_THETA ** (jnp.arange(0, HEAD64, 2, dtype=F32) / HEAD64))
    ang = pos[:, None] * inv_freq[None, :]
    cos, sin = jnp.cos(ang), jnp.sin(ang)
    return jnp.tile(cos, (1, 4)), jnp.tile(jnp.concatenate([-sin, sin], axis=1), (1, 2))


def _trunk(x, meta_tokens, layers):
    B, S, D = x.shape
    assert D == D_MODEL and S % BLOCK == 0
    meta = jnp.broadcast_to(meta_tokens[None].astype(x.dtype), (B, N_META, D))
    h = jnp.concatenate([x, meta, jnp.zeros((B, TAIL - N_META, D), x.dtype)], axis=1)
    cos, sin = _rope_tables(S)
    for l, p in enumerate(layers):
        lam_init = 0.8 - 0.6 * math.exp(-0.3 * l)
        qtda, kda, vtda, qtm, km, vtm, qg, ktg, vg = _prep_call(
            h, cos, sin, p["ng"], p["wa"], p["wuq"], p["wuk"], p["wuv"], p["gt"], p["gcq"], p["gckv"])
        oda = _dense_attn_call(qtda, kda, vtda, S, differential=True, lam_tab=p["lam_tab"], sub_g=p["sub_g"],
                               lam_init=lam_init, name="diff_attn")
        om = _dense_attn_call(qtm, km, vtm, S, differential=False, name="latent_attn")
        og = _window_attn_call(qg, ktg, vg, p["sink"], S)
        h = _merge_call(h, oda, om, og, p["ng"], p["wg"], p["wb"], p["wo"])
        h = _mlp_call(h, p["mg"], p["wu"], p["wd"])
    return h[:, :S]


def kernel(x_prompt, x_sample, meta_tokens, attn_norm_g, w_in, da_q_norm_g, da_k_norm_g, da_lam_q1, da_lam_k1, da_lam_q2, da_lam_k2, da_subln_g, mla_cq_norm_g, mla_ckv_norm_g, mla_w_uq, mla_w_ukv, mla_q_norm_g, mla_k_norm_g, gqa_q_norm_g, gqa_k_norm_g, gqa_sink, w_branch, w_out, mlp_norm_g, w_up, w_down):
    depth = w_in.shape[0]
    layers = [_layer_params(l, attn_norm_g, w_in, da_q_norm_g, da_k_norm_g, da_lam_q1, da_lam_k1, da_lam_q2,
                            da_lam_k2, da_subln_g, mla_cq_norm_g, mla_ckv_norm_g, mla_w_uq, mla_w_ukv,
                            mla_q_norm_g, mla_k_norm_g, gqa_q_norm_g, gqa_k_norm_g, gqa_sink, w_branch, w_out,
                            mlp_norm_g, w_up, w_down) for l in range(depth)]
    return (_trunk(x_prompt, meta_tokens, layers), _trunk(x_sample, meta_tokens, layers))
```

```python
import functools
import math

import jax
import jax.numpy as jnp
from jax import lax
from jax.experimental import pallas as pl
from jax.experimental.pallas import tpu as pltpu

F32 = jnp.float32
BF16 = jnp.bfloat16

D_MODEL = 1024
N_META = 16
BLOCK = 128
WINDOW = 128
ROPE_THETA = 10000.0
EPS = 1e-6
NEG = -1e30
LOG2E = 1.4426950408889634

HEAD64 = 64
MLA_HEADS = 4
MLA_NOPE = 128
MLA_QK = 192
MLA_Q_RANK = 384
MLA_KV_RANK = 256
GQA_HEADS = 8
GQA_GROUP = 4
BRANCH_WIDTH = 512
N_BRANCH = 3
D_FF = 4 * D_MODEL
LANES = 128
TAIL = 128
ONES_ROWS = 16
VT_ROWS = LANES + ONES_ROWS
CHUNK_GROUP = 8

A_DAQ, A_DAK, A_DAV, A_CQ, A_CKV, A_KR, A_GQ, A_GK, A_GV, A_END = (
    0, 512, 1024, 1536, 1920, 2176, 2304, 2816, 3072, 3328)
IN_OFF = (0, 512, 1024, 1536, 1920, 2176, 2240, 2752, 2880, 3008, 6080)

VMEM_LIMIT = 56 * 1024 * 1024


def _pick(n, candidates):
    for c in candidates:
        if n % c == 0:
            return c
    raise ValueError(f"no tile in {candidates} divides {n}")


def _cparams(sem):
    return pltpu.CompilerParams(dimension_semantics=sem, vmem_limit_bytes=VMEM_LIMIT)


def _resident(shape):
    zeros = (0,) * len(shape)
    return pl.BlockSpec(shape, lambda *_: zeros, pipeline_mode=pl.Buffered(1))


def _rms_scale(x, width):
    return lax.rsqrt(jnp.sum(x * x, axis=-1, keepdims=True) * (1.0 / width) + EPS)


def _pair_rms_scale(xc, lo_mask):
    sq = xc * xc
    lo = jnp.sum(jnp.where(lo_mask, sq, 0.0), axis=-1, keepdims=True)
    hi = jnp.sum(jnp.where(lo_mask, 0.0, sq), axis=-1, keepdims=True)
    return jnp.where(lo_mask, lax.rsqrt(lo * (1.0 / HEAD64) + EPS), lax.rsqrt(hi * (1.0 / HEAD64) + EPS))


def _rope(xc, cos, sin_signed, first_half):
    swapped = jnp.where(first_half, pltpu.roll(xc, LANES - 32, 1), pltpu.roll(xc, 32, 1))
    return xc * cos + swapped * sin_signed


def _prep_kernel(h_ref, cos_ref, sin_ref, ng_ref, wa_ref, wuq_ref, wuk_ref, wuv_ref, gt_ref, gcq_ref, gckv_ref,
                 qtda_ref, kda_ref, vtda_ref, qtm_ref, km_ref, vtm_ref, qg_ref, ktg_ref, vg_ref):
    x = h_ref[0]
    xn = ((x * _rms_scale(x, D_MODEL)) * ng_ref[...]).astype(BF16)
    xp = jnp.dot(xn, wa_ref[...], preferred_element_type=F32)
    cos = cos_ref[...]
    sin_s = sin_ref[...]
    lane = lax.broadcasted_iota(jnp.int32, cos.shape, 1)
    lo64 = lane < HEAD64
    first_half = (lane & 32) == 0
    s64 = (HEAD64 ** -0.5) * LOG2E
    s192 = (MLA_QK ** -0.5) * LOG2E

    def head_pair(col, gain):
        xc = xp[:, col:col + LANES]
        return _rope((xc * _pair_rms_scale(xc, lo64)) * gain, cos, sin_s, first_half)

    ones_rows = jnp.ones((ONES_ROWS, x.shape[0]), BF16)

    def store_vt(vt_ref, hd, v_slab):
        vt_ref[0, hd * VT_ROWS:hd * VT_ROWS + LANES, :] = v_slab.T.astype(BF16)
        vt_ref[0, hd * VT_ROWS + LANES:(hd + 1) * VT_ROWS, :] = ones_rows

    g_q, g_k = gt_ref[0:1, :], gt_ref[1:2, :]
    for c in range(4):
        sl = slice(c * LANES, (c + 1) * LANES)
        qtda_ref[0, sl, :] = (head_pair(A_DAQ + c * LANES, g_q) * s64).T.astype(BF16)
        kda_ref[0, :, sl] = head_pair(A_DAK + c * LANES, g_k).astype(BF16)
        store_vt(vtda_ref, c, xp[:, A_DAV + c * LANES:A_DAV + (c + 1) * LANES])

    cq = xp[:, A_CQ:A_CQ + MLA_Q_RANK]
    cqn = ((cq * _rms_scale(cq, MLA_Q_RANK)) * gcq_ref[...]).astype(BF16)
    qm = jnp.dot(cqn, wuq_ref[...], preferred_element_type=F32)
    ckv = xp[:, A_CKV:A_CKV + MLA_KV_RANK]
    ckvn = ((ckv * _rms_scale(ckv, MLA_KV_RANK)) * gckv_ref[...]).astype(BF16)
    kn_all = jnp.dot(ckvn, wuk_ref[...], preferred_element_type=F32)
    vm_all = jnp.dot(ckvn, wuv_ref[...], preferred_element_type=F32)
    kr = xp[:, A_KR:A_KR + LANES]
    kr_ss = jnp.sum(kr * kr, axis=-1, keepdims=True)
    gq_n, gq_r, gk_n, gk_r = gt_ref[4:5, :], gt_ref[5:6, :], gt_ref[6:7, :], gt_ref[7:8, :]
    for hd in range(MLA_HEADS):
        b = hd * 2 * LANES
        qn = qm[:, b:b + LANES]
        qr = qm[:, b + LANES:b + 2 * LANES]
        r = lax.rsqrt((jnp.sum(qn * qn, axis=-1, keepdims=True) + jnp.sum(qr * qr, axis=-1, keepdims=True))
                      * (1.0 / MLA_QK) + EPS)
        qtm_ref[0, b:b + LANES, :] = (((qn * r) * gq_n) * s192).T.astype(BF16)
        qtm_ref[0, b + LANES:b + 2 * LANES, :] = (_rope((qr * r) * gq_r, cos, sin_s, first_half) * s192).T.astype(BF16)
        kn = kn_all[:, hd * LANES:(hd + 1) * LANES]
        rk = lax.rsqrt((jnp.sum(kn * kn, axis=-1, keepdims=True) + kr_ss) * (1.0 / MLA_QK) + EPS)
        km_ref[0, :, b:b + LANES] = ((kn * rk) * gk_n).astype(BF16)
        km_ref[0, :, b + LANES:b + 2 * LANES] = _rope((kr * rk) * gk_r, cos, sin_s, first_half).astype(BF16)
        store_vt(vtm_ref, hd, vm_all[:, hd * LANES:(hd + 1) * LANES])

    g_q, g_k = gt_ref[2:3, :], gt_ref[3:4, :]
    for c in range(4):
        sl = slice(c * LANES, (c + 1) * LANES)
        qg_ref[0, :, sl] = (head_pair(A_GQ + c * LANES, g_q) * s64).astype(BF16)
    for c in range(2):
        sl = slice(c * LANES, (c + 1) * LANES)
        ktg_ref[0, sl, :] = head_pair(A_GK + c * LANES, g_k).T.astype(BF16)
    vg_ref[0] = xp[:, A_GV:A_GV + 256].astype(BF16)


def _prep_call(h, cos, sin, ng, wa, wuq, wuk, wuv, gt, gcq, gckv):
    B, Lp, D = h.shape
    tm = _pick(Lp, (384, 128))
    row = lambda b, j: (b, j, 0)
    colt = lambda b, j: (b, 0, j)
    tab = pl.BlockSpec((tm, LANES), lambda b, j: (j, 0))
    sds = lambda *s: jax.ShapeDtypeStruct(s, BF16)
    vt_rows = 4 * VT_ROWS
    out_shape = [sds(B, 512, Lp), sds(B, Lp, 512), sds(B, vt_rows, Lp),
                 sds(B, 1024, Lp), sds(B, Lp, 1024), sds(B, vt_rows, Lp),
                 sds(B, Lp, 512), sds(B, 256, Lp), sds(B, Lp, 256)]
    out_specs = [pl.BlockSpec((1, 512, tm), colt), pl.BlockSpec((1, tm, 512), row), pl.BlockSpec((1, vt_rows, tm), colt),
                 pl.BlockSpec((1, 1024, tm), colt), pl.BlockSpec((1, tm, 1024), row), pl.BlockSpec((1, vt_rows, tm), colt),
                 pl.BlockSpec((1, tm, 512), row), pl.BlockSpec((1, 256, tm), colt), pl.BlockSpec((1, tm, 256), row)]
    in_specs = [pl.BlockSpec((1, tm, D), row), tab, tab, _resident(ng.shape), _resident(wa.shape),
                _resident(wuq.shape), _resident(wuk.shape), _resident(wuv.shape), _resident(gt.shape),
                _resident(gcq.shape), _resident(gckv.shape)]
    return pl.pallas_call(
        _prep_kernel, out_shape=out_shape, grid=(B, Lp // tm), in_specs=in_specs, out_specs=out_specs,
        compiler_params=_cparams(("parallel", "parallel")), name="prep",
    )(h, cos, sin, ng, wa, wuq, wuk, wuv, gt, gcq, gckv)


def _dense_attn_kernel(*refs, n_real, tk, differential, lam_init):
    if differential:
        qt_ref, k_ref, vt_ref, lam_ref, sg_ref, o_ref, s_ref = refs
    else:
        qt_ref, k_ref, vt_ref, o_ref, s_ref = refs
    tq = qt_ref.shape[2]
    dv = o_ref.shape[2]
    qt = qt_ref[0]
    if differential:
        z = jnp.zeros((HEAD64, tq), BF16)
        wq = jnp.concatenate([jnp.concatenate([qt[:HEAD64], z], axis=1),
                              jnp.concatenate([z, qt[HEAD64:]], axis=1)], axis=0)
    else:
        wq = qt
    width = wq.shape[1]

    def scores(kc):
        return jnp.dot(kc, wq, preferred_element_type=F32)

    def absorb(m, acc, s, vtc):
        m_new = jnp.maximum(m, jnp.max(s, axis=0, keepdims=True))
        p = jnp.exp2(s - m_new).astype(BF16)
        alpha = jnp.exp2(m - m_new)
        return m_new, alpha * acc + jnp.dot(vtc, p, preferred_element_type=F32)

    def kchunk(c):
        return k_ref[0, pl.ds(pl.multiple_of(c * tk, tk), tk), :]

    def vchunk(c):
        return vt_ref[0, :, pl.ds(pl.multiple_of(c * tk, tk), tk)]

    def group(j, carry):
        m, acc = carry
        for g in range(CHUNK_GROUP):
            c = CHUNK_GROUP * j + g
            s_ref[(g + 1) % 2] = scores(kchunk(c + 1))
            m, acc = absorb(m, acc, s_ref[g % 2], vchunk(c))
        return m, acc

    n_chunks = n_real // tk
    n_groups = (n_chunks - 1) // CHUNK_GROUP
    s_ref[0] = scores(kchunk(0))
    carry = (jnp.full((1, width), NEG, F32), jnp.zeros((dv + ONES_ROWS, width), F32))
    m, acc = lax.fori_loop(0, n_groups, group, carry)
    valid = lax.broadcasted_iota(jnp.int32, (TAIL, width), 0) < N_META
    first = CHUNK_GROUP * n_groups
    for c in range(first, n_chunks):
        if c + 1 < n_chunks:
            s_ref[(c - first + 1) % 2] = scores(kchunk(c + 1))
        else:
            s_tail = jnp.where(valid, scores(k_ref[0, n_real:n_real + TAIL, :]), NEG)
        m, acc = absorb(m, acc, s_ref[(c - first) % 2], vchunk(c))
    _, acc = absorb(m, acc, s_tail, vt_ref[0, :, n_real:n_real + TAIL])

    ot = acc[:dv] * (1.0 / acc[dv:dv + 1])
    if differential:
        lt = lam_ref[...]
        lam = (jnp.exp(jnp.sum(lt[0:1, :] * lt[1:2, :], axis=-1, keepdims=True))
               - jnp.exp(jnp.sum(lt[2:3, :] * lt[3:4, :], axis=-1, keepdims=True)) + lam_init)
        o = (ot[:, :tq] - lam * ot[:, tq:]).T
        o = ((o * _rms_scale(o, dv)) * sg_ref[...]) * (1.0 - lam_init)
    else:
        o = ot.T
    o_ref[0] = o.astype(o_ref.dtype)


def _dense_attn_call(qt, k, vt, n_real, *, differential, lam_tab=None, sub_g=None, lam_init=0.0, name):
    B, Lp, kw = k.shape
    heads = vt.shape[1] // VT_ROWS
    dq = kw // heads
    tq = _pick(n_real, (512, 256, 128))
    tk = _pick(n_real, (512, 256, 128))
    in_specs = [pl.BlockSpec((1, dq, tq), lambda b, h, i: (b, h, i)),
                pl.BlockSpec((1, Lp, dq), lambda b, h, i: (b, 0, h)),
                pl.BlockSpec((1, VT_ROWS, Lp), lambda b, h, i: (b, h, 0))]
    args = [qt, k, vt]
    if differential:
        in_specs += [_resident(lam_tab.shape), _resident(sub_g.shape)]
        args += [lam_tab, sub_g]
    width = (2 if differential else 1) * tq
    kern = functools.partial(_dense_attn_kernel, n_real=n_real, tk=tk, differential=differential, lam_init=lam_init)
    return pl.pallas_call(
        kern, out_shape=jax.ShapeDtypeStruct((B, Lp, heads * LANES), BF16), grid=(B, heads, pl.cdiv(Lp, tq)),
        in_specs=in_specs, out_specs=pl.BlockSpec((1, tq, LANES), lambda b, h, i: (b, i, h)),
        scratch_shapes=[pltpu.VMEM((2, tk, width), F32)],
        compiler_params=_cparams(("parallel", "parallel", "parallel")), name=name,
    )(*args)


def _window_attn_kernel(sink_ref, q_ref, kp_ref, ko_ref, kn_ref, km_ref, vp_ref, vo_ref, vn_ref, vm_ref, o_ref, *, n_real):
    n = pl.program_id(1)
    n_blocks = n_real // BLOCK
    is_meta = n == n_blocks
    kcat = jnp.concatenate([kp_ref[0], ko_ref[0], kn_ref[0], km_ref[0]], axis=1)
    vcat = jnp.concatenate([vp_ref[0], vo_ref[0], vn_ref[0], vm_ref[0]], axis=0)
    shape = (BLOCK, 4 * BLOCK)
    qi = lax.broadcasted_iota(jnp.int32, shape, 0)
    kj = lax.broadcasted_iota(jnp.int32, shape, 1)
    seg = kj // BLOCK
    j = kj - seg * BLOCK
    tail_ok = jnp.logical_and(seg == 3, j < N_META)
    kpos = (n - 1) * BLOCK + kj
    dist = kpos - (n * BLOCK + qi)
    band = jnp.logical_and(jnp.logical_and(dist <= WINDOW, dist >= -WINDOW),
                           jnp.logical_and(kpos >= 0, kpos < n_real))
    vis_real = jnp.logical_or(jnp.logical_and(seg < 3, band), tail_ok)
    vis_meta = jnp.logical_or(jnp.logical_and(seg == 2, j <= qi + (WINDOW - N_META)), tail_ok)
    bias = jnp.where(is_meta, jnp.where(vis_meta, 0.0, NEG), jnp.where(vis_real, 0.0, NEG))
    lane = lax.broadcasted_iota(jnp.int32, (BLOCK, LANES), 1)
    lo64 = lane < HEAD64
    bias4 = jnp.concatenate([bias] * GQA_GROUP, axis=0)
    for g in range(GQA_HEADS // GQA_GROUP):
        rows, sinks = [], []
        for c in (2 * g, 2 * g + 1):
            qc = q_ref[0, :, c * LANES:(c + 1) * LANES].astype(F32)
            rows += [jnp.where(lo64, qc, 0.0), jnp.where(lo64, 0.0, qc)]
            sinks += [jnp.full((BLOCK, 1), sink_ref[2 * c] * LOG2E, F32),
                      jnp.full((BLOCK, 1), sink_ref[2 * c + 1] * LOG2E, F32)]
        qz = jnp.concatenate(rows, axis=0).astype(BF16)
        sink = jnp.concatenate(sinks, axis=0)
        s = jnp.dot(qz, kcat[g * LANES:(g + 1) * LANES, :], preferred_element_type=F32) + bias4
        m = jnp.maximum(jnp.max(s, axis=-1, keepdims=True), sink)
        p = jnp.exp2(s - m)
        denom = jnp.sum(p, axis=-1, keepdims=True) + jnp.exp2(sink - m)
        o = jnp.dot(p.astype(BF16), vcat[:, g * LANES:(g + 1) * LANES], preferred_element_type=F32)
        o = o * (1.0 / denom)
        for i, c in enumerate((2 * g, 2 * g + 1)):
            lo, hi = o[2 * i * BLOCK:(2 * i + 1) * BLOCK], o[(2 * i + 1) * BLOCK:(2 * i + 2) * BLOCK]
            o_ref[0, :, c * LANES:(c + 1) * LANES] = jnp.where(lo64, lo, hi).astype(o_ref.dtype)


def _window_attn_call(q, kt, v, sink, n_real):
    B, Lp, _ = q.shape
    nb = n_real // BLOCK
    prev = lambda b, n, s: jnp.maximum(n - 1, 0)
    nxt = lambda b, n, s: jnp.where(n == nb, 0, n + 1)
    kspec = lambda f: pl.BlockSpec((1, 256, BLOCK), lambda b, n, s: (b, 0, f(b, n, s)))
    vspec = lambda f: pl.BlockSpec((1, BLOCK, 256), lambda b, n, s: (b, f(b, n, s), 0))
    own = lambda b, n, s: n
    tail = lambda b, n, s: nb
    grid_spec = pltpu.PrefetchScalarGridSpec(
        num_scalar_prefetch=1, grid=(B, nb + 1),
        in_specs=[pl.BlockSpec((1, BLOCK, 512), lambda b, n, s: (b, n, 0)),
                  kspec(prev), kspec(own), kspec(nxt), kspec(tail),
                  vspec(prev), vspec(own), vspec(nxt), vspec(tail)],
        out_specs=pl.BlockSpec((1, BLOCK, 512), lambda b, n, s: (b, n, 0)))
    return pl.pallas_call(
        functools.partial(_window_attn_kernel, n_real=n_real),
        out_shape=jax.ShapeDtypeStruct((B, Lp, 512), BF16), grid_spec=grid_spec,
        compiler_params=_cparams(("parallel", "parallel")), name="window_attn",
    )(sink, q, kt, kt, kt, kt, v, v, v, v)


def _merge_kernel(h_ref, oda_ref, om_ref, og_ref, ng_ref, wg_ref, wb_ref, wo_ref, out_ref):
    x = h_ref[0]
    xn = ((x * _rms_scale(x, D_MODEL)) * ng_ref[...]).astype(BF16)
    gl = jnp.dot(xn, wg_ref[...], preferred_element_type=F32)
    merged = None
    for g, o_ref in enumerate((oda_ref, om_ref, og_ref)):
        proj = jnp.dot(o_ref[0], wb_ref[g], preferred_element_type=F32)
        term = jax.nn.sigmoid(gl[:, g * D_MODEL:(g + 1) * D_MODEL]) * proj
        merged = term if merged is None else merged + term
    out_ref[0] = x + jnp.dot(merged.astype(BF16), wo_ref[...], preferred_element_type=F32)


def _merge_call(h, oda, om, og, ng, wg, wb, wo):
    B, Lp, D = h.shape
    tm = _pick(Lp, (384, 128))
    row = lambda b, j: (b, j, 0)
    in_specs = [pl.BlockSpec((1, tm, D), row)] + [pl.BlockSpec((1, tm, BRANCH_WIDTH), row)] * 3 + [
        _resident(ng.shape), _resident(wg.shape), _resident(wb.shape), _resident(wo.shape)]
    return pl.pallas_call(
        _merge_kernel, out_shape=jax.ShapeDtypeStruct(h.shape, F32), grid=(B, Lp // tm),
        in_specs=in_specs, out_specs=pl.BlockSpec((1, tm, D), row),
        compiler_params=_cparams(("parallel", "parallel")), name="merge",
    )(h, oda, om, og, ng, wg, wb, wo)


def _mlp_kernel(h_ref, ng_ref, wu_ref, wd_ref, out_ref):
    x = h_ref[0]
    xn = ((x * _rms_scale(x, D_MODEL)) * ng_ref[...]).astype(BF16)
    u = jnp.dot(xn, wu_ref[...], preferred_element_type=F32)
    a = jnp.square(jnp.maximum(u, 0.0)).astype(BF16)
    out_ref[0] = x + jnp.dot(a, wd_ref[...], preferred_element_type=F32)


def _mlp_call(h, ng, wu, wd):
    B, Lp, D = h.shape
    tm = _pick(Lp, (384, 128))
    row = lambda b, j: (b, j, 0)
    return pl.pallas_call(
        _mlp_kernel, out_shape=jax.ShapeDtypeStruct(h.shape, F32), grid=(B, Lp // tm),
        in_specs=[pl.BlockSpec((1, tm, D), row), _resident(ng.shape), _resident(wu.shape), _resident(wd.shape)],
        out_specs=pl.BlockSpec((1, tm, D), row),
        compiler_params=_cparams(("parallel", "parallel")), name="mlp",
    )(h, ng, wu, wd)


def _pair_gain(g):
    return jnp.tile(g.astype(F32), 2)[None, :]


def _pad_gain(g):
    return jnp.concatenate([g.astype(F32), jnp.zeros((LANES - g.shape[0],), F32)])[None, :]


def _layer_params(l, attn_norm_g, w_in, da_q_norm_g, da_k_norm_g, da_lam_q1, da_lam_k1, da_lam_q2, da_lam_k2,
                  da_subln_g, mla_cq_norm_g, mla_ckv_norm_g, mla_w_uq, mla_w_ukv, mla_q_norm_g, mla_k_norm_g,
                  gqa_q_norm_g, gqa_k_norm_g, gqa_sink, w_branch, w_out, mlp_norm_g, w_up, w_down):
    w = w_in[l]
    sec = [w[:, a:b] for a, b in zip(IN_OFF[:-1], IN_OFF[1:])]
    da_q, da_k, da_v, cq, ckv, kr, gq, gk, gv, gate = sec
    dup = lambda m: jnp.concatenate([m[:, :HEAD64], m[:, :HEAD64], m[:, HEAD64:], m[:, HEAD64:]], axis=1)
    wa = jnp.concatenate([da_q, da_k, da_v, cq, ckv, kr, jnp.zeros((D_MODEL, LANES - HEAD64), w.dtype),
                          gq, dup(gk), dup(gv)], axis=1).astype(BF16)
    wuq = mla_w_uq[l].reshape(MLA_Q_RANK, MLA_HEADS, MLA_QK)
    wuq = jnp.pad(wuq, ((0, 0), (0, 0), (0, 2 * LANES - MLA_QK))).reshape(MLA_Q_RANK, MLA_HEADS * 2 * LANES).astype(BF16)
    wukv = mla_w_ukv[l].reshape(MLA_KV_RANK, MLA_HEADS, 2 * LANES)
    wuk = wukv[:, :, :MLA_NOPE].reshape(MLA_KV_RANK, MLA_HEADS * LANES).astype(BF16)
    wuv = wukv[:, :, MLA_NOPE:].reshape(MLA_KV_RANK, MLA_HEADS * LANES).astype(BF16)
    mq, mk = mla_q_norm_g[l], mla_k_norm_g[l]
    gt = jnp.concatenate([_pair_gain(da_q_norm_g[l]), _pair_gain(da_k_norm_g[l]),
                          _pair_gain(gqa_q_norm_g[l]), _pair_gain(gqa_k_norm_g[l]),
                          mq[None, :MLA_NOPE].astype(F32), _pad_gain(mq[MLA_NOPE:]),
                          mk[None, :MLA_NOPE].astype(F32), _pad_gain(mk[MLA_NOPE:])], axis=0)
    lam_tab = jnp.concatenate([_pad_gain(v[l]) for v in (da_lam_q1, da_lam_k1, da_lam_q2, da_lam_k2)], axis=0)
    return dict(
        ng=attn_norm_g[l][None, :].astype(F32), wa=wa, wuq=wuq, wuk=wuk, wuv=wuv, gt=gt,
        gcq=mla_cq_norm_g[l][None, :].astype(F32), gckv=mla_ckv_norm_g[l][None, :].astype(F32),
        lam_tab=lam_tab, sub_g=da_subln_g[l][None, :].astype(F32), sink=gqa_sink[l].astype(F32),
        wg=gate.astype(BF16), wb=w_branch[l].astype(BF16), wo=w_out[l].astype(BF16),
        mg=mlp_norm_g[l][None, :].astype(F32), wu=w_up[l].astype(BF16), wd=w_down[l].astype(BF16))


def _rope_tables(n_real):
    r = jnp.arange(n_real + TAIL, dtype=jnp.int32)
    pos = jnp.where(r < n_real, r + N_META, jnp.where(r < n_real + N_META, r - n_real, 0)).astype(F32)
    inv_freq = 1.0 / (ROPE_THETA ** (jnp.arange(0, HEAD64, 2, dtype=F32) / HEAD64))
    ang = pos[:, None] * inv_freq[None, :]
    cos, sin = jnp.cos(ang), jnp.sin(ang)
    return jnp.tile(cos, (1, 4)), jnp.tile(jnp.concatenate([-sin, sin], axis=1), (1, 2))


def _trunk(x, meta_tokens, layers):
    B, S, D = x.shape
    assert D == D_MODEL and S % BLOCK == 0
    meta = jnp.broadcast_to(meta_tokens[None].astype(x.dtype), (B, N_META, D))
    h = jnp.concatenate([x, meta, jnp.zeros((B, TAIL - N_META, D), x.dtype)], axis=1)
    cos, sin = _rope_tables(S)
    for l, p in enumerate(layers):
        lam_init = 0.8 - 0.6 * math.exp(-0.3 * l)
        qtda, kda, vtda, qtm, km, vtm, qg, ktg, vg = _prep_call(
            h, cos, sin, p["ng"], p["wa"], p["wuq"], p["wuk"], p["wuv"], p["gt"], p["gcq"], p["gckv"])
        oda = _dense_attn_call(qtda, kda, vtda, S, differential=True, lam_tab=p["lam_tab"], sub_g=p["sub_g"],
                               lam_init=lam_init, name="diff_attn")
        om = _dense_attn_call(qtm, km, vtm, S, differential=False, name="latent_attn")
        og = _window_attn_call(qg, ktg, vg, p["sink"], S)
        h = _merge_call(h, oda, om, og, p["ng"], p["wg"], p["wb"], p["wo"])
        h = _mlp_call(h, p["mg"], p["wu"], p["wd"])
    return h[:, :S]


def kernel(x_prompt, x_sample, meta_tokens, attn_norm_g, w_in, da_q_norm_g, da_k_norm_g, da_lam_q1, da_lam_k1, da_lam_q2, da_lam_k2, da_subln_g, mla_cq_norm_g, mla_ckv_norm_g, mla_w_uq, mla_w_ukv, mla_q_norm_g, mla_k_norm_g, gqa_q_norm_g, gqa_k_norm_g, gqa_sink, w_branch, w_out, mlp_norm_g, w_up, w_down):
    depth = w_in.shape[0]
    layers = [_layer_params(l, attn_norm_g, w_in, da_q_norm_g, da_k_norm_g, da_lam_q1, da_lam_k1, da_lam_q2,
                            da_lam_k2, da_subln_g, mla_cq_norm_g, mla_ckv_norm_g, mla_w_uq, mla_w_ukv,
                            mla_q_norm_g, mla_k_norm_g, gqa_q_norm_g, gqa_k_norm_g, gqa_sink, w_branch, w_out,
                            mlp_norm_g, w_up, w_down) for l in range(depth)]
    return (_trunk(x_prompt, meta_tokens, layers), _trunk(x_sample, meta_tokens, layers))
```

```python
import functools
import math

import jax
import jax.numpy as jnp
from jax import lax
from jax.experimental import pallas as pl
from jax.experimental.pallas import tpu as pltpu

F32 = jnp.float32
BF16 = jnp.bfloat16

D_MODEL = 1024
N_META = 16
BLOCK = 128
WINDOW = 128
ROPE_THETA = 10000.0
EPS = 1e-6
NEG = -1e30
LOG2E = 1.4426950408889634

HEAD64 = 64
MLA_HEADS = 4
MLA_NOPE = 128
MLA_QK = 192
MLA_Q_RANK = 384
MLA_KV_RANK = 256
GQA_HEADS = 8
GQA_GROUP = 4
BRANCH_WIDTH = 512
N_BRANCH = 3
D_FF = 4 * D_MODEL
LANES = 128
TAIL = 128
ONES_ROWS = 16
VT_ROWS = LANES + ONES_ROWS
CHUNK_GROUP = 8

A_DAQ, A_DAQS, A_DAK, A_DAKS, A_DAV, A_CQ, A_CKV, A_KR, A_KRS, A_GQ, A_GQS, A_GK, A_GKS, A_GV, A_END = (
    0, 512, 1024, 1536, 2048, 2560, 2944, 3200, 3328, 3456, 3968, 4480, 4736, 4992, 5248)
IN_OFF = (0, 512, 1024, 1536, 1920, 2176, 2240, 2752, 2880, 3008, 6080)

VMEM_LIMIT = 56 * 1024 * 1024


def _pick(n, candidates):
    for c in candidates:
        if n % c == 0:
            return c
    raise ValueError(f"no tile in {candidates} divides {n}")


def _cparams(sem):
    return pltpu.CompilerParams(dimension_semantics=sem, vmem_limit_bytes=VMEM_LIMIT)


def _resident(shape):
    zeros = (0,) * len(shape)
    return pl.BlockSpec(shape, lambda *_: zeros, pipeline_mode=pl.Buffered(1))


def _rms_scale(x, width):
    return lax.rsqrt(jnp.sum(x * x, axis=-1, keepdims=True) * (1.0 / width) + EPS)


def _pair_rms_scale(xc, lo_mask):
    sq = xc * xc
    lo = jnp.sum(jnp.where(lo_mask, sq, 0.0), axis=-1, keepdims=True)
    hi = jnp.sum(jnp.where(lo_mask, 0.0, sq), axis=-1, keepdims=True)
    return jnp.where(lo_mask, lax.rsqrt(lo * (1.0 / HEAD64) + EPS), lax.rsqrt(hi * (1.0 / HEAD64) + EPS))


def _rope(xc, swapped, cos, sin_signed):
    return xc * cos + swapped * sin_signed


def _prep_kernel(h_ref, cos_ref, sin_ref, ng_ref, wa_ref, wuq_ref, wuk_ref, wuv_ref, gt_ref, gcq_ref, gckv_ref,
                 qtda_ref, kda_ref, vtda_ref, qtm_ref, km_ref, vtm_ref, qg_ref, ktg_ref, vg_ref):
    x = h_ref[0]
    xn = ((x * _rms_scale(x, D_MODEL)) * ng_ref[...]).astype(BF16)
    xp = jnp.dot(xn, wa_ref[...], preferred_element_type=F32)
    cos = cos_ref[...]
    sin_s = sin_ref[...]
    lane = lax.broadcasted_iota(jnp.int32, cos.shape, 1)
    lo64 = lane < HEAD64
    s64 = (HEAD64 ** -0.5) * LOG2E
    s192 = (MLA_QK ** -0.5) * LOG2E

    def head_pair(col, col_swapped, gains):
        xc = xp[:, col:col + LANES]
        xs = xp[:, col_swapped:col_swapped + LANES]
        r = _pair_rms_scale(xc, lo64)
        return _rope((xc * r) * gains[0:1, :], (xs * r) * gains[1:2, :], cos, sin_s)

    ones_rows = jnp.ones((ONES_ROWS, x.shape[0]), BF16)

    def store_vt(vt_ref, hd, v_slab):
        vt_ref[0, hd * VT_ROWS:hd * VT_ROWS + LANES, :] = v_slab.T.astype(BF16)
        vt_ref[0, hd * VT_ROWS + LANES:(hd + 1) * VT_ROWS, :] = ones_rows

    g_q, g_k = gt_ref[0:2, :], gt_ref[2:4, :]
    for c in range(4):
        sl = slice(c * LANES, (c + 1) * LANES)
        qtda_ref[0, sl, :] = (head_pair(A_DAQ + c * LANES, A_DAQS + c * LANES, g_q) * s64).T.astype(BF16)
        kda_ref[0, :, sl] = head_pair(A_DAK + c * LANES, A_DAKS + c * LANES, g_k).astype(BF16)
        store_vt(vtda_ref, c, xp[:, A_DAV + c * LANES:A_DAV + (c + 1) * LANES])

    cq = xp[:, A_CQ:A_CQ + MLA_Q_RANK]
    cqn = ((cq * _rms_scale(cq, MLA_Q_RANK)) * gcq_ref[...]).astype(BF16)
    qm = jnp.dot(cqn, wuq_ref[...], preferred_element_type=F32)
    ckv = xp[:, A_CKV:A_CKV + MLA_KV_RANK]
    ckvn = ((ckv * _rms_scale(ckv, MLA_KV_RANK)) * gckv_ref[...]).astype(BF16)
    kn_all = jnp.dot(ckvn, wuk_ref[...], preferred_element_type=F32)
    vm_all = jnp.dot(ckvn, wuv_ref[...], preferred_element_type=F32)
    kr = xp[:, A_KR:A_KR + LANES]
    krs = xp[:, A_KRS:A_KRS + LANES]
    kr_ss = jnp.sum(kr * kr, axis=-1, keepdims=True)
    gq_n, gq_r, gq_rs = gt_ref[8:9, :], gt_ref[9:10, :], gt_ref[10:11, :]
    gk_n, gk_r, gk_rs = gt_ref[11:12, :], gt_ref[12:13, :], gt_ref[13:14, :]
    for hd in range(MLA_HEADS):
        b = hd * 2 * LANES
        qb = hd * 3 * LANES
        qn = qm[:, qb:qb + LANES]
        qr = qm[:, qb + LANES:qb + 2 * LANES]
        qrs = qm[:, qb + 2 * LANES:qb + 3 * LANES]
        r = lax.rsqrt((jnp.sum(qn * qn, axis=-1, keepdims=True) + jnp.sum(qr * qr, axis=-1, keepdims=True))
                      * (1.0 / MLA_QK) + EPS)
        qtm_ref[0, b:b + LANES, :] = (((qn * r) * gq_n) * s192).T.astype(BF16)
        qtm_ref[0, b + LANES:b + 2 * LANES, :] = (
            _rope((qr * r) * gq_r, (qrs * r) * gq_rs, cos, sin_s) * s192).T.astype(BF16)
        kn = kn_all[:, hd * LANES:(hd + 1) * LANES]
        rk = lax.rsqrt((jnp.sum(kn * kn, axis=-1, keepdims=True) + kr_ss) * (1.0 / MLA_QK) + EPS)
        km_ref[0, :, b:b + LANES] = ((kn * rk) * gk_n).astype(BF16)
        km_ref[0, :, b + LANES:b + 2 * LANES] = _rope((kr * rk) * gk_r, (krs * rk) * gk_rs, cos, sin_s).astype(BF16)
        store_vt(vtm_ref, hd, vm_all[:, hd * LANES:(hd + 1) * LANES])

    g_q, g_k = gt_ref[4:6, :], gt_ref[6:8, :]
    for c in range(4):
        sl = slice(c * LANES, (c + 1) * LANES)
        qg_ref[0, :, sl] = (head_pair(A_GQ + c * LANES, A_GQS + c * LANES, g_q) * s64).astype(BF16)
    for c in range(2):
        sl = slice(c * LANES, (c + 1) * LANES)
        ktg_ref[0, sl, :] = head_pair(A_GK + c * LANES, A_GKS + c * LANES, g_k).T.astype(BF16)
    vg_ref[0] = xp[:, A_GV:A_GV + 256].astype(BF16)


def _prep_call(h, cos, sin, ng, wa, wuq, wuk, wuv, gt, gcq, gckv):
    B, Lp, D = h.shape
    tm = _pick(Lp, (384, 128))
    row = lambda b, j: (b, j, 0)
    colt = lambda b, j: (b, 0, j)
    tab = pl.BlockSpec((tm, LANES), lambda b, j: (j, 0))
    sds = lambda *s: jax.ShapeDtypeStruct(s, BF16)
    vt_rows = 4 * VT_ROWS
    out_shape = [sds(B, 512, Lp), sds(B, Lp, 512), sds(B, vt_rows, Lp),
                 sds(B, 1024, Lp), sds(B, Lp, 1024), sds(B, vt_rows, Lp),
                 sds(B, Lp, 512), sds(B, 256, Lp), sds(B, Lp, 256)]
    out_specs = [pl.BlockSpec((1, 512, tm), colt), pl.BlockSpec((1, tm, 512), row), pl.BlockSpec((1, vt_rows, tm), colt),
                 pl.BlockSpec((1, 1024, tm), colt), pl.BlockSpec((1, tm, 1024), row), pl.BlockSpec((1, vt_rows, tm), colt),
                 pl.BlockSpec((1, tm, 512), row), pl.BlockSpec((1, 256, tm), colt), pl.BlockSpec((1, tm, 256), row)]
    in_specs = [pl.BlockSpec((1, tm, D), row), tab, tab, _resident(ng.shape), _resident(wa.shape),
                _resident(wuq.shape), _resident(wuk.shape), _resident(wuv.shape), _resident(gt.shape),
                _resident(gcq.shape), _resident(gckv.shape)]
    return pl.pallas_call(
        _prep_kernel, out_shape=out_shape, grid=(B, Lp // tm), in_specs=in_specs, out_specs=out_specs,
        compiler_params=_cparams(("parallel", "parallel")), name="prep",
    )(h, cos, sin, ng, wa, wuq, wuk, wuv, gt, gcq, gckv)


def _dense_attn_kernel(*refs, n_real, tk, differential, lam_init):
    if differential:
        qt_ref, k_ref, vt_ref, lam_ref, sg_ref, o_ref, s_ref = refs
    else:
        qt_ref, k_ref, vt_ref, o_ref, s_ref = refs
    tq = qt_ref.shape[2]
    dv = o_ref.shape[2]
    qt = qt_ref[0]
    if differential:
        z = jnp.zeros((HEAD64, tq), BF16)
        wq = jnp.concatenate([jnp.concatenate([qt[:HEAD64], z], axis=1),
                              jnp.concatenate([z, qt[HEAD64:]], axis=1)], axis=0)
    else:
        wq = qt
    width = wq.shape[1]

    def scores(kc):
        return jnp.dot(kc, wq, preferred_element_type=F32)

    def absorb(m, acc, s, vtc):
        m_new = jnp.maximum(m, jnp.max(s, axis=0, keepdims=True))
        p = jnp.exp2(s - m_new).astype(BF16)
        alpha = jnp.exp2(m - m_new)
        return m_new, alpha * acc + jnp.dot(vtc, p, preferred_element_type=F32)

    def kchunk(c):
        return k_ref[0, pl.ds(pl.multiple_of(c * tk, tk), tk), :]

    def vchunk(c):
        return vt_ref[0, :, pl.ds(pl.multiple_of(c * tk, tk), tk)]

    def group(j, carry):
        m, acc = carry
        for g in range(CHUNK_GROUP):
            c = CHUNK_GROUP * j + g
            s_ref[(g + 1) % 2] = scores(kchunk(c + 1))
            m, acc = absorb(m, acc, s_ref[g % 2], vchunk(c))
        return m, acc

    n_chunks = n_real // tk
    n_groups = (n_chunks - 1) // CHUNK_GROUP
    s_ref[0] = scores(kchunk(0))
    carry = (jnp.full((1, width), NEG, F32), jnp.zeros((dv + ONES_ROWS, width), F32))
    m, acc = lax.fori_loop(0, n_groups, group, carry)
    valid = lax.broadcasted_iota(jnp.int32, (TAIL, width), 0) < N_META
    first = CHUNK_GROUP * n_groups
    for c in range(first, n_chunks):
        if c + 1 < n_chunks:
            s_ref[(c - first + 1) % 2] = scores(kchunk(c + 1))
        else:
            s_tail = jnp.where(valid, scores(k_ref[0, n_real:n_real + TAIL, :]), NEG)
        m, acc = absorb(m, acc, s_ref[(c - first) % 2], vchunk(c))
    _, acc = absorb(m, acc, s_tail, vt_ref[0, :, n_real:n_real + TAIL])

    ot = acc[:dv] * (1.0 / acc[dv:dv + 1])
    if differential:
        lt = lam_ref[...]
        lam = (jnp.exp(jnp.sum(lt[0:1, :] * lt[1:2, :], axis=-1, keepdims=True))
               - jnp.exp(jnp.sum(lt[2:3, :] * lt[3:4, :], axis=-1, keepdims=True)) + lam_init)
        o = (ot[:, :tq] - lam * ot[:, tq:]).T
        o = ((o * _rms_scale(o, dv)) * sg_ref[...]) * (1.0 - lam_init)
    else:
        o = ot.T
    o_ref[0] = o.astype(o_ref.dtype)


def _dense_attn_call(qt, k, vt, n_real, *, differential, lam_tab=None, sub_g=None, lam_init=0.0, name):
    B, Lp, kw = k.shape
    heads = vt.shape[1] // VT_ROWS
    dq = kw // heads
    tq = _pick(n_real, (512, 256, 128))
    tk = _pick(n_real, (512, 256, 128))
    in_specs = [pl.BlockSpec((1, dq, tq), lambda b, h, i: (b, h, i)),
                pl.BlockSpec((1, Lp, dq), lambda b, h, i: (b, 0, h)),
                pl.BlockSpec((1, VT_ROWS, Lp), lambda b, h, i: (b, h, 0))]
    args = [qt, k, vt]
    if differential:
        in_specs += [_resident(lam_tab.shape), _resident(sub_g.shape)]
        args += [lam_tab, sub_g]
    width = (2 if differential else 1) * tq
    kern = functools.partial(_dense_attn_kernel, n_real=n_real, tk=tk, differential=differential, lam_init=lam_init)
    return pl.pallas_call(
        kern, out_shape=jax.ShapeDtypeStruct((B, Lp, heads * LANES), BF16), grid=(B, heads, pl.cdiv(Lp, tq)),
        in_specs=in_specs, out_specs=pl.BlockSpec((1, tq, LANES), lambda b, h, i: (b, i, h)),
        scratch_shapes=[pltpu.VMEM((2, tk, width), F32)],
        compiler_params=_cparams(("parallel", "parallel", "parallel")), name=name,
    )(*args)


def _window_attn_kernel(sink_ref, q_ref, kp_ref, ko_ref, kn_ref, km_ref, vp_ref, vo_ref, vn_ref, vm_ref, o_ref, *, n_real):
    n = pl.program_id(1)
    n_blocks = n_real // BLOCK
    is_meta = n == n_blocks
    kcat = jnp.concatenate([kp_ref[0], ko_ref[0], kn_ref[0], km_ref[0]], axis=1)
    vcat = jnp.concatenate([vp_ref[0], vo_ref[0], vn_ref[0], vm_ref[0]], axis=0)
    shape = (BLOCK, 4 * BLOCK)
    qi = lax.broadcasted_iota(jnp.int32, shape, 0)
    kj = lax.broadcasted_iota(jnp.int32, shape, 1)
    seg = kj // BLOCK
    j = kj - seg * BLOCK
    tail_ok = jnp.logical_and(seg == 3, j < N_META)
    kpos = (n - 1) * BLOCK + kj
    dist = kpos - (n * BLOCK + qi)
    band = jnp.logical_and(jnp.logical_and(dist <= WINDOW, dist >= -WINDOW),
                           jnp.logical_and(kpos >= 0, kpos < n_real))
    vis_real = jnp.logical_or(jnp.logical_and(seg < 3, band), tail_ok)
    vis_meta = jnp.logical_or(jnp.logical_and(seg == 2, j <= qi + (WINDOW - N_META)), tail_ok)
    bias = jnp.where(is_meta, jnp.where(vis_meta, 0.0, NEG), jnp.where(vis_real, 0.0, NEG))
    lane = lax.broadcasted_iota(jnp.int32, (BLOCK, LANES), 1)
    lo64 = lane < HEAD64
    bias4 = jnp.concatenate([bias] * GQA_GROUP, axis=0)
    for g in range(GQA_HEADS // GQA_GROUP):
        rows, sinks = [], []
        for c in (2 * g, 2 * g + 1):
            qc = q_ref[0, :, c * LANES:(c + 1) * LANES].astype(F32)
            rows += [jnp.where(lo64, qc, 0.0), jnp.where(lo64, 0.0, qc)]
            sinks += [jnp.full((BLOCK, 1), sink_ref[2 * c] * LOG2E, F32),
                      jnp.full((BLOCK, 1), sink_ref[2 * c + 1] * LOG2E, F32)]
        qz = jnp.concatenate(rows, axis=0).astype(BF16)
        sink = jnp.concatenate(sinks, axis=0)
        s = jnp.dot(qz, kcat[g * LANES:(g + 1) * LANES, :], preferred_element_type=F32) + bias4
        m = jnp.maximum(jnp.max(s, axis=-1, keepdims=True), sink)
        p = jnp.exp2(s - m)
        denom = jnp.sum(p, axis=-1, keepdims=True) + jnp.exp2(sink - m)
        o = jnp.dot(p.astype(BF16), vcat[:, g * LANES:(g + 1) * LANES], preferred_element_type=F32)
        o = o * (1.0 / denom)
        for i, c in enumerate((2 * g, 2 * g + 1)):
            lo, hi = o[2 * i * BLOCK:(2 * i + 1) * BLOCK], o[(2 * i + 1) * BLOCK:(2 * i + 2) * BLOCK]
            o_ref[0, :, c * LANES:(c + 1) * LANES] = jnp.where(lo64, lo, hi).astype(o_ref.dtype)


def _window_attn_call(q, kt, v, sink, n_real):
    B, Lp, _ = q.shape
    nb = n_real // BLOCK
    prev = lambda b, n, s: jnp.maximum(n - 1, 0)
    nxt = lambda b, n, s: jnp.where(n == nb, 0, n + 1)
    kspec = lambda f: pl.BlockSpec((1, 256, BLOCK), lambda b, n, s: (b, 0, f(b, n, s)))
    vspec = lambda f: pl.BlockSpec((1, BLOCK, 256), lambda b, n, s: (b, f(b, n, s), 0))
    own = lambda b, n, s: n
    tail = lambda b, n, s: nb
    grid_spec = pltpu.PrefetchScalarGridSpec(
        num_scalar_prefetch=1, grid=(B, nb + 1),
        in_specs=[pl.BlockSpec((1, BLOCK, 512), lambda b, n, s: (b, n, 0)),
                  kspec(prev), kspec(own), kspec(nxt), kspec(tail),
                  vspec(prev), vspec(own), vspec(nxt), vspec(tail)],
        out_specs=pl.BlockSpec((1, BLOCK, 512), lambda b, n, s: (b, n, 0)))
    return pl.pallas_call(
        functools.partial(_window_attn_kernel, n_real=n_real),
        out_shape=jax.ShapeDtypeStruct((B, Lp, 512), BF16), grid_spec=grid_spec,
        compiler_params=_cparams(("parallel", "parallel")), name="window_attn",
    )(sink, q, kt, kt, kt, kt, v, v, v, v)


def _merge_kernel(h_ref, oda_ref, om_ref, og_ref, ng_ref, wg_ref, wb_ref, wo_ref, out_ref):
    x = h_ref[0]
    xn = ((x * _rms_scale(x, D_MODEL)) * ng_ref[...]).astype(BF16)
    gl = jnp.dot(xn, wg_ref[...], preferred_element_type=F32)
    merged = None
    for g, o_ref in enumerate((oda_ref, om_ref, og_ref)):
        proj = jnp.dot(o_ref[0], wb_ref[g], preferred_element_type=F32)
        term = jax.nn.sigmoid(gl[:, g * D_MODEL:(g + 1) * D_MODEL]) * proj
        merged = term if merged is None else merged + term
    out_ref[0] = x + jnp.dot(merged.astype(BF16), wo_ref[...], preferred_element_type=F32)


def _merge_call(h, oda, om, og, ng, wg, wb, wo):
    B, Lp, D = h.shape
    tm = _pick(Lp, (384, 128))
    row = lambda b, j: (b, j, 0)
    in_specs = [pl.BlockSpec((1, tm, D), row)] + [pl.BlockSpec((1, tm, BRANCH_WIDTH), row)] * 3 + [
        _resident(ng.shape), _resident(wg.shape), _resident(wb.shape), _resident(wo.shape)]
    return pl.pallas_call(
        _merge_kernel, out_shape=jax.ShapeDtypeStruct(h.shape, F32), grid=(B, Lp // tm),
        in_specs=in_specs, out_specs=pl.BlockSpec((1, tm, D), row),
        compiler_params=_cparams(("parallel", "parallel")), name="merge",
    )(h, oda, om, og, ng, wg, wb, wo)


def _mlp_kernel(h_ref, ng_ref, wu_ref, wd_ref, out_ref):
    x = h_ref[0]
    xn = ((x * _rms_scale(x, D_MODEL)) * ng_ref[...]).astype(BF16)
    u = jnp.dot(xn, wu_ref[...], preferred_element_type=F32)
    a = jnp.square(jnp.maximum(u, 0.0)).astype(BF16)
    out_ref[0] = x + jnp.dot(a, wd_ref[...], preferred_element_type=F32)


def _mlp_call(h, ng, wu, wd, out_rows):
    B, Lp, D = h.shape
    tm = _pick(Lp, (384, 128))
    assert Lp - out_rows < tm
    row = lambda b, j: (b, j, 0)
    return pl.pallas_call(
        _mlp_kernel, out_shape=jax.ShapeDtypeStruct((B, out_rows, D), F32), grid=(B, Lp // tm),
        in_specs=[pl.BlockSpec((1, tm, D), row), _resident(ng.shape), _resident(wu.shape), _resident(wd.shape)],
        out_specs=pl.BlockSpec((1, tm, D), row),
        compiler_params=_cparams(("parallel", "parallel")), name="mlp",
    )(h, ng, wu, wd)


def _swap_halves(m):
    return jnp.flip(m.reshape(m.shape[:-1] + (m.shape[-1] // HEAD64, 2, HEAD64 // 2)), axis=-2).reshape(m.shape)


def _pair_gain(g):
    g = jnp.tile(g.astype(F32), 2)[None, :]
    return jnp.concatenate([g, _swap_halves(g)], axis=0)


def _pad_gain(g):
    return jnp.concatenate([g.astype(F32), jnp.zeros((LANES - g.shape[0],), F32)])[None, :]


def _layer_params(l, attn_norm_g, w_in, da_q_norm_g, da_k_norm_g, da_lam_q1, da_lam_k1, da_lam_q2, da_lam_k2,
                  da_subln_g, mla_cq_norm_g, mla_ckv_norm_g, mla_w_uq, mla_w_ukv, mla_q_norm_g, mla_k_norm_g,
                  gqa_q_norm_g, gqa_k_norm_g, gqa_sink, w_branch, w_out, mlp_norm_g, w_up, w_down):
    w = w_in[l]
    sec = [w[:, a:b] for a, b in zip(IN_OFF[:-1], IN_OFF[1:])]
    da_q, da_k, da_v, cq, ckv, kr, gq, gk, gv, gate = sec
    dup = lambda m: jnp.concatenate([m[:, :HEAD64], m[:, :HEAD64], m[:, HEAD64:], m[:, HEAD64:]], axis=1)
    pad64 = lambda m: jnp.pad(m, ((0, 0),) * (m.ndim - 1) + ((0, LANES - HEAD64),))
    wa = jnp.concatenate([da_q, _swap_halves(da_q), da_k, _swap_halves(da_k), da_v, cq, ckv,
                          pad64(kr), pad64(_swap_halves(kr)), gq, _swap_halves(gq),
                          dup(gk), _swap_halves(dup(gk)), dup(gv)], axis=1).astype(BF16)
    wuq = mla_w_uq[l].reshape(MLA_Q_RANK, MLA_HEADS, MLA_QK)
    wuq_r = wuq[:, :, MLA_NOPE:]
    wuq = jnp.concatenate([wuq[:, :, :MLA_NOPE], pad64(wuq_r), pad64(_swap_halves(wuq_r))], axis=-1)
    wuq = wuq.reshape(MLA_Q_RANK, MLA_HEADS * 3 * LANES).astype(BF16)
    wukv = mla_w_ukv[l].reshape(MLA_KV_RANK, MLA_HEADS, 2 * LANES)
    wuk = wukv[:, :, :MLA_NOPE].reshape(MLA_KV_RANK, MLA_HEADS * LANES).astype(BF16)
    wuv = wukv[:, :, MLA_NOPE:].reshape(MLA_KV_RANK, MLA_HEADS * LANES).astype(BF16)
    mq, mk = mla_q_norm_g[l], mla_k_norm_g[l]
    gt = jnp.concatenate([_pair_gain(da_q_norm_g[l]), _pair_gain(da_k_norm_g[l]),
                          _pair_gain(gqa_q_norm_g[l]), _pair_gain(gqa_k_norm_g[l]),
                          mq[None, :MLA_NOPE].astype(F32), _pad_gain(mq[MLA_NOPE:]), _pad_gain(_swap_halves(mq[MLA_NOPE:])),
                          mk[None, :MLA_NOPE].astype(F32), _pad_gain(mk[MLA_NOPE:]), _pad_gain(_swap_halves(mk[MLA_NOPE:])),
                          jnp.zeros((2, LANES), F32)], axis=0)
    lam_tab = jnp.concatenate([_pad_gain(v[l]) for v in (da_lam_q1, da_lam_k1, da_lam_q2, da_lam_k2)], axis=0)
    return dict(
        ng=attn_norm_g[l][None, :].astype(F32), wa=wa, wuq=wuq, wuk=wuk, wuv=wuv, gt=gt,
        gcq=mla_cq_norm_g[l][None, :].astype(F32), gckv=mla_ckv_norm_g[l][None, :].astype(F32),
        lam_tab=lam_tab, sub_g=da_subln_g[l][None, :].astype(F32), sink=gqa_sink[l].astype(F32),
        wg=gate.astype(BF16), wb=w_branch[l].astype(BF16), wo=w_out[l].astype(BF16),
        mg=mlp_norm_g[l][None, :].astype(F32), wu=w_up[l].astype(BF16), wd=w_down[l].astype(BF16))


def _rope_tables(n_real):
    r = jnp.arange(n_real + TAIL, dtype=jnp.int32)
    pos = jnp.where(r < n_real, r + N_META, jnp.where(r < n_real + N_META, r - n_real, 0)).astype(F32)
    inv_freq = 1.0 / (ROPE_THETA ** (jnp.arange(0, HEAD64, 2, dtype=F32) / HEAD64))
    ang = pos[:, None] * inv_freq[None, :]
    cos, sin = jnp.cos(ang), jnp.sin(ang)
    return jnp.tile(cos, (1, 4)), jnp.tile(jnp.concatenate([-sin, sin], axis=1), (1, 2))


def _trunk(x, meta_tokens, layers):
    B, S, D = x.shape
    assert D == D_MODEL and S % BLOCK == 0
    meta = jnp.broadcast_to(meta_tokens[None].astype(x.dtype), (B, N_META, D))
    h = jnp.concatenate([x, meta, jnp.zeros((B, TAIL - N_META, D), x.dtype)], axis=1)
    cos, sin = _rope_tables(S)
    for l, p in enumerate(layers):
        lam_init = 0.8 - 0.6 * math.exp(-0.3 * l)
        qtda, kda, vtda, qtm, km, vtm, qg, ktg, vg = _prep_call(
            h, cos, sin, p["ng"], p["wa"], p["wuq"], p["wuk"], p["wuv"], p["gt"], p["gcq"], p["gckv"])
        oda = _dense_attn_call(qtda, kda, vtda, S, differential=True, lam_tab=p["lam_tab"], sub_g=p["sub_g"],
                               lam_init=lam_init, name="diff_attn")
        om = _dense_attn_call(qtm, km, vtm, S, differential=False, name="latent_attn")
        og = _window_attn_call(qg, ktg, vg, p["sink"], S)
        h = _merge_call(h, oda, om, og, p["ng"], p["wg"], p["wb"], p["wo"])
        h = _mlp_call(h, p["mg"], p["wu"], p["wd"], out_rows=S if l == len(layers) - 1 else S + TAIL)
    return h


def kernel(x_prompt, x_sample, meta_tokens, attn_norm_g, w_in, da_q_norm_g, da_k_norm_g, da_lam_q1, da_lam_k1, da_lam_q2, da_lam_k2, da_subln_g, mla_cq_norm_g, mla_ckv_norm_g, mla_w_uq, mla_w_ukv, mla_q_norm_g, mla_k_norm_g, gqa_q_norm_g, gqa_k_norm_g, gqa_sink, w_branch, w_out, mlp_norm_g, w_up, w_down):
    depth = w_in.shape[0]
    layers = [_layer_params(l, attn_norm_g, w_in, da_q_norm_g, da_k_norm_g, da_lam_q1, da_lam_k1, da_lam_q2,
                            da_lam_k2, da_subln_g, mla_cq_norm_g, mla_ckv_norm_g, mla_w_uq, mla_w_ukv,
                            mla_q_norm_g, mla_k_norm_g, gqa_q_norm_g, gqa_k_norm_g, gqa_sink, w_branch, w_out,
                            mlp_norm_g, w_up, w_down) for l in range(depth)]
    return (_trunk(x_prompt, meta_tokens, layers), _trunk(x_sample, meta_tokens, layers))
```

```python
import functools
import math

import jax
import jax.numpy as jnp
from jax import lax
from jax.experimental import pallas as pl
from jax.experimental.pallas import tpu as pltpu

F32 = jnp.float32
BF16 = jnp.bfloat16

D_MODEL = 1024
N_META = 16
BLOCK = 128
WINDOW = 128
ROPE_THETA = 10000.0
EPS = 1e-6
NEG = -1e30
LOG2E = 1.4426950408889634

HEAD64 = 64
MLA_HEADS = 4
MLA_NOPE = 128
MLA_QK = 192
MLA_Q_RANK = 384
MLA_KV_RANK = 256
GQA_HEADS = 8
GQA_GROUP = 4
BRANCH_WIDTH = 512
N_BRANCH = 3
D_FF = 4 * D_MODEL
LANES = 128
TAIL = 128
ONES_ROWS = 16
VT_ROWS = LANES + ONES_ROWS
CHUNK_GROUP = 8

A_DAQ, A_DAQS, A_DAK, A_DAKS, A_DAV, A_CQ, A_CKV, A_KR, A_KRS, A_GQ, A_GQS, A_GK, A_GKS, A_GV, A_END = (
    0, 512, 1024, 1536, 2048, 2560, 2944, 3200, 3328, 3456, 3968, 4480, 4736, 4992, 5248)
IN_OFF = (0, 512, 1024, 1536, 1920, 2176, 2240, 2752, 2880, 3008, 6080)

VMEM_LIMIT = 56 * 1024 * 1024


def _pick(n, candidates):
    for c in candidates:
        if n % c == 0:
            return c
    raise ValueError(f"no tile in {candidates} divides {n}")


def _cparams(sem):
    return pltpu.CompilerParams(dimension_semantics=sem, vmem_limit_bytes=VMEM_LIMIT)


def _resident(shape):
    zeros = (0,) * len(shape)
    return pl.BlockSpec(shape, lambda *_: zeros, pipeline_mode=pl.Buffered(1))


def _rms_scale(x, width):
    return lax.rsqrt(jnp.sum(x * x, axis=-1, keepdims=True) * (1.0 / width) + EPS)


def _pair_rms_scale(xc, lo_mask):
    sq = xc * xc
    lo = jnp.sum(jnp.where(lo_mask, sq, 0.0), axis=-1, keepdims=True)
    hi = jnp.sum(jnp.where(lo_mask, 0.0, sq), axis=-1, keepdims=True)
    return jnp.where(lo_mask, lax.rsqrt(lo * (1.0 / HEAD64) + EPS), lax.rsqrt(hi * (1.0 / HEAD64) + EPS))


def _rope(xc, swapped, cos, sin_signed):
    return xc * cos + swapped * sin_signed


def _prep_kernel(h_ref, cos_ref, sin_ref, ng_ref, wa_ref, wuq_ref, wuk_ref, wuv_ref, gt_ref, gcq_ref, gckv_ref,
                 qtda_ref, kda_ref, vtda_ref, qtm_ref, km_ref, vtm_ref, qg_ref, ktg_ref, vg_ref):
    x = h_ref[0]
    xn = ((x * _rms_scale(x, D_MODEL)) * ng_ref[...]).astype(BF16)
    xp = jnp.dot(xn, wa_ref[...], preferred_element_type=F32)
    cos = cos_ref[...]
    sin_s = sin_ref[...]
    lane = lax.broadcasted_iota(jnp.int32, cos.shape, 1)
    lo64 = lane < HEAD64
    s64 = (HEAD64 ** -0.5) * LOG2E
    s192 = (MLA_QK ** -0.5) * LOG2E

    def head_pair(col, col_swapped, gains):
        xc = xp[:, col:col + LANES]
        xs = xp[:, col_swapped:col_swapped + LANES]
        r = _pair_rms_scale(xc, lo64)
        return _rope((xc * r) * gains[0:1, :], (xs * r) * gains[1:2, :], cos, sin_s)

    ones_rows = jnp.ones((ONES_ROWS, x.shape[0]), BF16)

    def store_vt(vt_ref, hd, v_slab):
        vt_ref[0, hd * VT_ROWS:hd * VT_ROWS + LANES, :] = v_slab.T.astype(BF16)
        vt_ref[0, hd * VT_ROWS + LANES:(hd + 1) * VT_ROWS, :] = ones_rows

    g_q, g_k = gt_ref[0:2, :], gt_ref[2:4, :]
    for c in range(4):
        sl = slice(c * LANES, (c + 1) * LANES)
        qtda_ref[0, sl, :] = (head_pair(A_DAQ + c * LANES, A_DAQS + c * LANES, g_q) * s64).T.astype(BF16)
        kda_ref[0, c] = head_pair(A_DAK + c * LANES, A_DAKS + c * LANES, g_k).astype(BF16)
        store_vt(vtda_ref, c, xp[:, A_DAV + c * LANES:A_DAV + (c + 1) * LANES])

    cq = xp[:, A_CQ:A_CQ + MLA_Q_RANK]
    cqn = ((cq * _rms_scale(cq, MLA_Q_RANK)) * gcq_ref[...]).astype(BF16)
    qm = jnp.dot(cqn, wuq_ref[...], preferred_element_type=F32)
    ckv = xp[:, A_CKV:A_CKV + MLA_KV_RANK]
    ckvn = ((ckv * _rms_scale(ckv, MLA_KV_RANK)) * gckv_ref[...]).astype(BF16)
    kn_all = jnp.dot(ckvn, wuk_ref[...], preferred_element_type=F32)
    vm_all = jnp.dot(ckvn, wuv_ref[...], preferred_element_type=F32)
    kr = xp[:, A_KR:A_KR + LANES]
    krs = xp[:, A_KRS:A_KRS + LANES]
    kr_ss = jnp.sum(kr * kr, axis=-1, keepdims=True)
    gq_n, gq_r, gq_rs = gt_ref[8:9, :], gt_ref[9:10, :], gt_ref[10:11, :]
    gk_n, gk_r, gk_rs = gt_ref[11:12, :], gt_ref[12:13, :], gt_ref[13:14, :]
    for hd in range(MLA_HEADS):
        b = hd * 2 * LANES
        qb = hd * 3 * LANES
        qn = qm[:, qb:qb + LANES]
        qr = qm[:, qb + LANES:qb + 2 * LANES]
        qrs = qm[:, qb + 2 * LANES:qb + 3 * LANES]
        r = lax.rsqrt((jnp.sum(qn * qn, axis=-1, keepdims=True) + jnp.sum(qr * qr, axis=-1, keepdims=True))
                      * (1.0 / MLA_QK) + EPS)
        qtm_ref[0, b:b + LANES, :] = (((qn * r) * gq_n) * s192).T.astype(BF16)
        qtm_ref[0, b + LANES:b + 2 * LANES, :] = (
            _rope((qr * r) * gq_r, (qrs * r) * gq_rs, cos, sin_s) * s192).T.astype(BF16)
        kn = kn_all[:, hd * LANES:(hd + 1) * LANES]
        rk = lax.rsqrt((jnp.sum(kn * kn, axis=-1, keepdims=True) + kr_ss) * (1.0 / MLA_QK) + EPS)
        km_ref[0, hd, :, :LANES] = ((kn * rk) * gk_n).astype(BF16)
        km_ref[0, hd, :, LANES:] = _rope((kr * rk) * gk_r, (krs * rk) * gk_rs, cos, sin_s).astype(BF16)
        store_vt(vtm_ref, hd, vm_all[:, hd * LANES:(hd + 1) * LANES])

    g_q, g_k = gt_ref[4:6, :], gt_ref[6:8, :]
    for c in range(4):
        sl = slice(c * LANES, (c + 1) * LANES)
        qg_ref[0, :, sl] = (head_pair(A_GQ + c * LANES, A_GQS + c * LANES, g_q) * s64).astype(BF16)
    for c in range(2):
        sl = slice(c * LANES, (c + 1) * LANES)
        ktg_ref[0, sl, :] = head_pair(A_GK + c * LANES, A_GKS + c * LANES, g_k).T.astype(BF16)
    vg_ref[0] = xp[:, A_GV:A_GV + 256].astype(BF16)


def _prep_call(h, cos, sin, ng, wa, wuq, wuk, wuv, gt, gcq, gckv):
    B, Lp, D = h.shape
    tm = _pick(Lp, (384, 128))
    row = lambda b, j: (b, j, 0)
    colt = lambda b, j: (b, 0, j)
    tab = pl.BlockSpec((tm, LANES), lambda b, j: (j, 0))
    sds = lambda *s: jax.ShapeDtypeStruct(s, BF16)
    vt_rows = 4 * VT_ROWS
    out_shape = [sds(B, 512, Lp), sds(B, 4, Lp, LANES), sds(B, vt_rows, Lp),
                 sds(B, 1024, Lp), sds(B, MLA_HEADS, Lp, 2 * LANES), sds(B, vt_rows, Lp),
                 sds(B, Lp, 512), sds(B, 256, Lp), sds(B, Lp, 256)]
    hrow = lambda b, j: (b, 0, j, 0)
    out_specs = [pl.BlockSpec((1, 512, tm), colt), pl.BlockSpec((1, 4, tm, LANES), hrow), pl.BlockSpec((1, vt_rows, tm), colt),
                 pl.BlockSpec((1, 1024, tm), colt), pl.BlockSpec((1, MLA_HEADS, tm, 2 * LANES), hrow),
                 pl.BlockSpec((1, vt_rows, tm), colt),
                 pl.BlockSpec((1, tm, 512), row), pl.BlockSpec((1, 256, tm), colt), pl.BlockSpec((1, tm, 256), row)]
    in_specs = [pl.BlockSpec((1, tm, D), row), tab, tab, _resident(ng.shape), _resident(wa.shape),
                _resident(wuq.shape), _resident(wuk.shape), _resident(wuv.shape), _resident(gt.shape),
                _resident(gcq.shape), _resident(gckv.shape)]
    return pl.pallas_call(
        _prep_kernel, out_shape=out_shape, grid=(B, Lp // tm), in_specs=in_specs, out_specs=out_specs,
        compiler_params=_cparams(("parallel", "parallel")), name="prep",
    )(h, cos, sin, ng, wa, wuq, wuk, wuv, gt, gcq, gckv)


def _dense_attn_kernel(*refs, n_real, tk, differential, lam_init):
    if differential:
        qt_ref, k_ref, vt_ref, lam_ref, sg_ref, o_ref, s_ref = refs
    else:
        qt_ref, k_ref, vt_ref, o_ref, s_ref = refs
    tq = qt_ref.shape[2]
    dv = o_ref.shape[2]
    qt = qt_ref[0]
    if differential:
        z = jnp.zeros((HEAD64, tq), BF16)
        wq = jnp.concatenate([jnp.concatenate([qt[:HEAD64], z], axis=1),
                              jnp.concatenate([z, qt[HEAD64:]], axis=1)], axis=0)
    else:
        wq = qt
    width = wq.shape[1]

    def scores(kc):
        return jnp.dot(kc, wq, preferred_element_type=F32)

    def absorb(m, acc, s, vtc):
        m_new = jnp.maximum(m, jnp.max(s, axis=0, keepdims=True))
        p = jnp.exp2(s - m_new).astype(BF16)
        alpha = jnp.exp2(m - m_new)
        return m_new, alpha * acc + jnp.dot(vtc, p, preferred_element_type=F32)

    def kchunk(c):
        return k_ref[0, 0, pl.ds(pl.multiple_of(c * tk, tk), tk), :]

    def vchunk(c):
        return vt_ref[0, :, pl.ds(pl.multiple_of(c * tk, tk), tk)]

    def group(j, carry):
        m, acc = carry
        for g in range(CHUNK_GROUP):
            c = CHUNK_GROUP * j + g
            s_ref[(g + 1) % 2] = scores(kchunk(c + 1))
            m, acc = absorb(m, acc, s_ref[g % 2], vchunk(c))
        return m, acc

    n_chunks = n_real // tk
    n_groups = (n_chunks - 1) // CHUNK_GROUP
    s_ref[0] = scores(kchunk(0))
    carry = (jnp.full((1, width), NEG, F32), jnp.zeros((dv + ONES_ROWS, width), F32))
    m, acc = lax.fori_loop(0, n_groups, group, carry)
    valid = lax.broadcasted_iota(jnp.int32, (TAIL, width), 0) < N_META
    first = CHUNK_GROUP * n_groups
    for c in range(first, n_chunks):
        if c + 1 < n_chunks:
            s_ref[(c - first + 1) % 2] = scores(kchunk(c + 1))
        else:
            s_tail = jnp.where(valid, scores(k_ref[0, 0, n_real:n_real + TAIL, :]), NEG)
        m, acc = absorb(m, acc, s_ref[(c - first) % 2], vchunk(c))
    _, acc = absorb(m, acc, s_tail, vt_ref[0, :, n_real:n_real + TAIL])

    ot = acc[:dv] * (1.0 / acc[dv:dv + 1])
    if differential:
        lt = lam_ref[...]
        lam = (jnp.exp(jnp.sum(lt[0:1, :] * lt[1:2, :], axis=-1, keepdims=True))
               - jnp.exp(jnp.sum(lt[2:3, :] * lt[3:4, :], axis=-1, keepdims=True)) + lam_init)
        o = (ot[:, :tq] - lam * ot[:, tq:]).T
        o = ((o * _rms_scale(o, dv)) * sg_ref[...]) * (1.0 - lam_init)
    else:
        o = ot.T
    o_ref[0] = o.astype(o_ref.dtype)


def _dense_attn_call(qt, k, vt, n_real, *, differential, lam_tab=None, sub_g=None, lam_init=0.0, name):
    B, heads, Lp, dq = k.shape
    if differential:
        tq = _pick(Lp, (384, 128))
    else:
        tq = _pick(n_real, (512, 256, 128))
    tk = _pick(n_real, (512, 256, 128))
    in_specs = [pl.BlockSpec((1, dq, tq), lambda b, h, i: (b, h, i)),
                pl.BlockSpec((1, 1, Lp, dq), lambda b, h, i: (b, h, 0, 0)),
                pl.BlockSpec((1, VT_ROWS, Lp), lambda b, h, i: (b, h, 0))]
    args = [qt, k, vt]
    if differential:
        in_specs += [_resident(lam_tab.shape), _resident(sub_g.shape)]
        args += [lam_tab, sub_g]
    width = (2 if differential else 1) * tq
    kern = functools.partial(_dense_attn_kernel, n_real=n_real, tk=tk, differential=differential, lam_init=lam_init)
    return pl.pallas_call(
        kern, out_shape=jax.ShapeDtypeStruct((B, Lp, heads * LANES), BF16), grid=(B, heads, pl.cdiv(Lp, tq)),
        in_specs=in_specs, out_specs=pl.BlockSpec((1, tq, LANES), lambda b, h, i: (b, i, h)),
        scratch_shapes=[pltpu.VMEM((2, tk, width), F32)],
        compiler_params=_cparams(("parallel", "parallel", "parallel")), name=name,
    )(*args)


def _window_attn_kernel(sink_ref, q_ref, kp_ref, ko_ref, kn_ref, km_ref, vp_ref, vo_ref, vn_ref, vm_ref, o_ref, *, n_real):
    n = pl.program_id(1)
    n_blocks = n_real // BLOCK
    is_meta = n == n_blocks
    kcat = jnp.concatenate([kp_ref[0], ko_ref[0], kn_ref[0], km_ref[0]], axis=1)
    vcat = jnp.concatenate([vp_ref[0], vo_ref[0], vn_ref[0], vm_ref[0]], axis=0)
    shape = (BLOCK, 4 * BLOCK)
    qi = lax.broadcasted_iota(jnp.int32, shape, 0)
    kj = lax.broadcasted_iota(jnp.int32, shape, 1)
    seg = kj // BLOCK
    j = kj - seg * BLOCK
    tail_ok = jnp.logical_and(seg == 3, j < N_META)
    kpos = (n - 1) * BLOCK + kj
    dist = kpos - (n * BLOCK + qi)
    band = jnp.logical_and(jnp.logical_and(dist <= WINDOW, dist >= -WINDOW),
                           jnp.logical_and(kpos >= 0, kpos < n_real))
    vis_real = jnp.logical_or(jnp.logical_and(seg < 3, band), tail_ok)
    vis_meta = jnp.logical_or(jnp.logical_and(seg == 2, j <= qi + (WINDOW - N_META)), tail_ok)
    bias = jnp.where(is_meta, jnp.where(vis_meta, 0.0, NEG), jnp.where(vis_real, 0.0, NEG))
    lane = lax.broadcasted_iota(jnp.int32, (BLOCK, LANES), 1)
    lo64 = lane < HEAD64
    bias4 = jnp.concatenate([bias] * GQA_GROUP, axis=0)
    for g in range(GQA_HEADS // GQA_GROUP):
        rows, sinks = [], []
        for c in (2 * g, 2 * g + 1):
            qc = q_ref[0, :, c * LANES:(c + 1) * LANES].astype(F32)
            rows += [jnp.where(lo64, qc, 0.0), jnp.where(lo64, 0.0, qc)]
            sinks += [jnp.full((BLOCK, 1), sink_ref[2 * c] * LOG2E, F32),
                      jnp.full((BLOCK, 1), sink_ref[2 * c + 1] * LOG2E, F32)]
        qz = jnp.concatenate(rows, axis=0).astype(BF16)
        sink = jnp.concatenate(sinks, axis=0)
        s = jnp.dot(qz, kcat[g * LANES:(g + 1) * LANES, :], preferred_element_type=F32) + bias4
        m = jnp.maximum(jnp.max(s, axis=-1, keepdims=True), sink)
        p = jnp.exp2(s - m)
        denom = jnp.sum(p, axis=-1, keepdims=True) + jnp.exp2(sink - m)
        o = jnp.dot(p.astype(BF16), vcat[:, g * LANES:(g + 1) * LANES], preferred_element_type=F32)
        o = o * (1.0 / denom)
        for i, c in enumerate((2 * g, 2 * g + 1)):
            lo, hi = o[2 * i * BLOCK:(2 * i + 1) * BLOCK], o[(2 * i + 1) * BLOCK:(2 * i + 2) * BLOCK]
            o_ref[0, :, c * LANES:(c + 1) * LANES] = jnp.where(lo64, lo, hi).astype(o_ref.dtype)


def _window_attn_call(q, kt, v, sink, n_real):
    B, Lp, _ = q.shape
    nb = n_real // BLOCK
    prev = lambda b, n, s: jnp.maximum(n - 1, 0)
    nxt = lambda b, n, s: jnp.where(n == nb, 0, n + 1)
    kspec = lambda f: pl.BlockSpec((1, 256, BLOCK), lambda b, n, s: (b, 0, f(b, n, s)))
    vspec = lambda f: pl.BlockSpec((1, BLOCK, 256), lambda b, n, s: (b, f(b, n, s), 0))
    own = lambda b, n, s: n
    tail = lambda b, n, s: nb
    grid_spec = pltpu.PrefetchScalarGridSpec(
        num_scalar_prefetch=1, grid=(B, nb + 1),
        in_specs=[pl.BlockSpec((1, BLOCK, 512), lambda b, n, s: (b, n, 0)),
                  kspec(prev), kspec(own), kspec(nxt), kspec(tail),
                  vspec(prev), vspec(own), vspec(nxt), vspec(tail)],
        out_specs=pl.BlockSpec((1, BLOCK, 512), lambda b, n, s: (b, n, 0)))
    return pl.pallas_call(
        functools.partial(_window_attn_kernel, n_real=n_real),
        out_shape=jax.ShapeDtypeStruct((B, Lp, 512), BF16), grid_spec=grid_spec,
        compiler_params=_cparams(("parallel", "parallel")), name="window_attn",
    )(sink, q, kt, kt, kt, kt, v, v, v, v)


def _merge_kernel(h_ref, oda_ref, om_ref, og_ref, ng_ref, wg_ref, wb_ref, wo_ref, out_ref):
    x = h_ref[0]
    xn = ((x * _rms_scale(x, D_MODEL)) * ng_ref[...]).astype(BF16)
    gl = jnp.dot(xn, wg_ref[...], preferred_element_type=F32)
    merged = None
    for g, o_ref in enumerate((oda_ref, om_ref, og_ref)):
        proj = jnp.dot(o_ref[0], wb_ref[g], preferred_element_type=F32)
        term = jax.nn.sigmoid(gl[:, g * D_MODEL:(g + 1) * D_MODEL]) * proj
        merged = term if merged is None else merged + term
    out_ref[0] = x + jnp.dot(merged.astype(BF16), wo_ref[...], preferred_element_type=F32)


def _merge_call(h, oda, om, og, ng, wg, wb, wo):
    B, Lp, D = h.shape
    tm = _pick(Lp, (384, 128))
    row = lambda b, j: (b, j, 0)
    in_specs = [pl.BlockSpec((1, tm, D), row)] + [pl.BlockSpec((1, tm, BRANCH_WIDTH), row)] * 3 + [
        _resident(ng.shape), _resident(wg.shape), _resident(wb.shape), _resident(wo.shape)]
    return pl.pallas_call(
        _merge_kernel, out_shape=jax.ShapeDtypeStruct(h.shape, F32), grid=(B, Lp // tm),
        in_specs=in_specs, out_specs=pl.BlockSpec((1, tm, D), row),
        compiler_params=_cparams(("parallel", "parallel")), name="merge",
    )(h, oda, om, og, ng, wg, wb, wo)


def _mlp_kernel(h_ref, ng_ref, wu_ref, wd_ref, out_ref):
    x = h_ref[0]
    xn = ((x * _rms_scale(x, D_MODEL)) * ng_ref[...]).astype(BF16)
    u = jnp.dot(xn, wu_ref[...], preferred_element_type=F32)
    a = jnp.square(jnp.maximum(u, 0.0)).astype(BF16)
    out_ref[0] = x + jnp.dot(a, wd_ref[...], preferred_element_type=F32)


def _mlp_call(h, ng, wu, wd, out_rows):
    B, Lp, D = h.shape
    tm = _pick(Lp, (384, 128))
    row = lambda b, j: (b, j, 0)
    return pl.pallas_call(
        _mlp_kernel, out_shape=jax.ShapeDtypeStruct((B, out_rows, D), F32), grid=(B, pl.cdiv(out_rows, tm)),
        in_specs=[pl.BlockSpec((1, tm, D), row), _resident(ng.shape), _resident(wu.shape), _resident(wd.shape)],
        out_specs=pl.BlockSpec((1, tm, D), row),
        compiler_params=_cparams(("parallel", "parallel")), name="mlp",
    )(h, ng, wu, wd)


def _swap_halves(m):
    return jnp.flip(m.reshape(m.shape[:-1] + (m.shape[-1] // HEAD64, 2, HEAD64 // 2)), axis=-2).reshape(m.shape)


def _pair_gain(g):
    g = jnp.tile(g.astype(F32), 2)[None, :]
    return jnp.concatenate([g, _swap_halves(g)], axis=0)


def _pad_gain(g):
    return jnp.concatenate([g.astype(F32), jnp.zeros((LANES - g.shape[0],), F32)])[None, :]


def _layer_params(l, attn_norm_g, w_in, da_q_norm_g, da_k_norm_g, da_lam_q1, da_lam_k1, da_lam_q2, da_lam_k2,
                  da_subln_g, mla_cq_norm_g, mla_ckv_norm_g, mla_w_uq, mla_w_ukv, mla_q_norm_g, mla_k_norm_g,
                  gqa_q_norm_g, gqa_k_norm_g, gqa_sink, w_branch, w_out, mlp_norm_g, w_up, w_down):
    w = w_in[l]
    sec = [w[:, a:b] for a, b in zip(IN_OFF[:-1], IN_OFF[1:])]
    da_q, da_k, da_v, cq, ckv, kr, gq, gk, gv, gate = sec
    dup = lambda m: jnp.concatenate([m[:, :HEAD64], m[:, :HEAD64], m[:, HEAD64:], m[:, HEAD64:]], axis=1)
    pad64 = lambda m: jnp.pad(m, ((0, 0),) * (m.ndim - 1) + ((0, LANES - HEAD64),))
    wa = jnp.concatenate([da_q, _swap_halves(da_q), da_k, _swap_halves(da_k), da_v, cq, ckv,
                          pad64(kr), pad64(_swap_halves(kr)), gq, _swap_halves(gq),
                          dup(gk), _swap_halves(dup(gk)), dup(gv)], axis=1).astype(BF16)
    wuq = mla_w_uq[l].reshape(MLA_Q_RANK, MLA_HEADS, MLA_QK)
    wuq_r = wuq[:, :, MLA_NOPE:]
    wuq = jnp.concatenate([wuq[:, :, :MLA_NOPE], pad64(wuq_r), pad64(_swap_halves(wuq_r))], axis=-1)
    wuq = wuq.reshape(MLA_Q_RANK, MLA_HEADS * 3 * LANES).astype(BF16)
    wukv = mla_w_ukv[l].reshape(MLA_KV_RANK, MLA_HEADS, 2 * LANES)
    wuk = wukv[:, :, :MLA_NOPE].reshape(MLA_KV_RANK, MLA_HEADS * LANES).astype(BF16)
    wuv = wukv[:, :, MLA_NOPE:].reshape(MLA_KV_RANK, MLA_HEADS * LANES).astype(BF16)
    mq, mk = mla_q_norm_g[l], mla_k_norm_g[l]
    gt = jnp.concatenate([_pair_gain(da_q_norm_g[l]), _pair_gain(da_k_norm_g[l]),
                          _pair_gain(gqa_q_norm_g[l]), _pair_gain(gqa_k_norm_g[l]),
                          mq[None, :MLA_NOPE].astype(F32), _pad_gain(mq[MLA_NOPE:]), _pad_gain(_swap_halves(mq[MLA_NOPE:])),
                          mk[None, :MLA_NOPE].astype(F32), _pad_gain(mk[MLA_NOPE:]), _pad_gain(_swap_halves(mk[MLA_NOPE:])),
                          jnp.zeros((2, LANES), F32)], axis=0)
    lam_tab = jnp.concatenate([_pad_gain(v[l]) for v in (da_lam_q1, da_lam_k1, da_lam_q2, da_lam_k2)], axis=0)
    return dict(
        ng=attn_norm_g[l][None, :].astype(F32), wa=wa, wuq=wuq, wuk=wuk, wuv=wuv, gt=gt,
        gcq=mla_cq_norm_g[l][None, :].astype(F32), gckv=mla_ckv_norm_g[l][None, :].astype(F32),
        lam_tab=lam_tab, sub_g=da_subln_g[l][None, :].astype(F32), sink=gqa_sink[l].astype(F32),
        wg=gate.astype(BF16), wb=w_branch[l].astype(BF16), wo=w_out[l].astype(BF16),
        mg=mlp_norm_g[l][None, :].astype(F32), wu=w_up[l].astype(BF16), wd=w_down[l].astype(BF16))


def _rope_tables(n_real):
    r = jnp.arange(n_real + TAIL, dtype=jnp.int32)
    pos = jnp.where(r < n_real, r + N_META, jnp.where(r < n_real + N_META, r - n_real, 0)).astype(F32)
    inv_freq = 1.0 / (ROPE_THETA ** (jnp.arange(0, HEAD64, 2, dtype=F32) / HEAD64))
    ang = pos[:, None] * inv_freq[None, :]
    cos, sin = jnp.cos(ang), jnp.sin(ang)
    return jnp.tile(cos, (1, 4)), jnp.tile(jnp.concatenate([-sin, sin], axis=1), (1, 2))


def _trunk(x, meta_tokens, layers):
    B, S, D = x.shape
    assert D == D_MODEL and S % BLOCK == 0
    meta = jnp.broadcast_to(meta_tokens[None].astype(x.dtype), (B, N_META, D))
    h = jnp.concatenate([x, meta, jnp.zeros((B, TAIL - N_META, D), x.dtype)], axis=1)
    cos, sin = _rope_tables(S)
    for l, p in enumerate(layers):
        lam_init = 0.8 - 0.6 * math.exp(-0.3 * l)
        qtda, kda, vtda, qtm, km, vtm, qg, ktg, vg = _prep_call(
            h, cos, sin, p["ng"], p["wa"], p["wuq"], p["wuk"], p["wuv"], p["gt"], p["gcq"], p["gckv"])
        oda = _dense_attn_call(qtda, kda, vtda, S, differential=True, lam_tab=p["lam_tab"], sub_g=p["sub_g"],
                               lam_init=lam_init, name="diff_attn")
        om = _dense_attn_call(qtm, km, vtm, S, differential=False, name="latent_attn")
        og = _window_attn_call(qg, ktg, vg, p["sink"], S)
        h = _merge_call(h, oda, om, og, p["ng"], p["wg"], p["wb"], p["wo"])
        h = _mlp_call(h, p["mg"], p["wu"], p["wd"], out_rows=S if l == len(layers) - 1 else S + TAIL)
    return h


def kernel(x_prompt, x_sample, meta_tokens, attn_norm_g, w_in, da_q_norm_g, da_k_norm_g, da_lam_q1, da_lam_k1, da_lam_q2, da_lam_k2, da_subln_g, mla_cq_norm_g, mla_ckv_norm_g, mla_w_uq, mla_w_ukv, mla_q_norm_g, mla_k_norm_g, gqa_q_norm_g, gqa_k_norm_g, gqa_sink, w_branch, w_out, mlp_norm_g, w_up, w_down):
    depth = w_in.shape[0]
    layers = [_layer_params(l, attn_norm_g, w_in, da_q_norm_g, da_k_norm_g, da_lam_q1, da_lam_k1, da_lam_q2,
                            da_lam_k2, da_subln_g, mla_cq_norm_g, mla_ckv_norm_g, mla_w_uq, mla_w_ukv,
                            mla_q_norm_g, mla_k_norm_g, gqa_q_norm_g, gqa_k_norm_g, gqa_sink, w_branch, w_out,
                            mlp_norm_g, w_up, w_down) for l in range(depth)]
    return (_trunk(x_prompt, meta_tokens, layers), _trunk(x_sample, meta_tokens, layers))
```

```python
import functools
import math

import jax
import jax.numpy as jnp
from jax import lax
from jax.experimental import pallas as pl
from jax.experimental.pallas import tpu as pltpu

F32 = jnp.float32
BF16 = jnp.bfloat16

D_MODEL = 1024
N_META = 16
BLOCK = 128
WINDOW = 128
ROPE_THETA = 10000.0
EPS = 1e-6
NEG = -1e30
LOG2E = 1.4426950408889634

HEAD64 = 64
MLA_HEADS = 4
MLA_NOPE = 128
MLA_QK = 192
MLA_Q_RANK = 384
MLA_KV_RANK = 256
GQA_HEADS = 8
GQA_GROUP = 4
BRANCH_WIDTH = 512
N_BRANCH = 3
D_FF = 4 * D_MODEL
LANES = 128
TAIL = 128
ONES_ROWS = 16
VT_ROWS = LANES + ONES_ROWS
GQA_VT_ROWS = HEAD64 + ONES_ROWS
CHUNK_GROUP = 8

A_DAQ, A_DAQS, A_DAK, A_DAKS, A_DAV, A_CQ, A_CKV, A_KR, A_KRS, A_GQ, A_GQS, A_GK, A_GKS, A_GV, A_END = (
    0, 512, 1024, 1536, 2048, 2560, 2944, 3200, 3328, 3456, 3968, 4480, 4608, 4736, 4864)
IN_OFF = (0, 512, 1024, 1536, 1920, 2176, 2240, 2752, 2880, 3008, 6080)

VMEM_LIMIT = 56 * 1024 * 1024


def _pick(n, candidates):
    for c in candidates:
        if n % c == 0:
            return c
    raise ValueError(f"no tile in {candidates} divides {n}")


def _cparams(sem):
    return pltpu.CompilerParams(dimension_semantics=sem, vmem_limit_bytes=VMEM_LIMIT)


def _resident(shape):
    zeros = (0,) * len(shape)
    return pl.BlockSpec(shape, lambda *_: zeros, pipeline_mode=pl.Buffered(1))


def _rms_scale(x, width):
    return lax.rsqrt(jnp.sum(x * x, axis=-1, keepdims=True) * (1.0 / width) + EPS)


def _pair_rms_scale(xc, lo_mask):
    sq = xc * xc
    lo = jnp.sum(jnp.where(lo_mask, sq, 0.0), axis=-1, keepdims=True)
    hi = jnp.sum(jnp.where(lo_mask, 0.0, sq), axis=-1, keepdims=True)
    return jnp.where(lo_mask, lax.rsqrt(lo * (1.0 / HEAD64) + EPS), lax.rsqrt(hi * (1.0 / HEAD64) + EPS))


def _rope(xc, swapped, cos, sin_signed):
    return xc * cos + swapped * sin_signed


def _prep_kernel(h_ref, cos_ref, sin_ref, ng_ref, wa_ref, wuq_ref, wuk_ref, wuv_ref, gt_ref, gcq_ref, gckv_ref,
                 qtda_ref, kda_ref, vtda_ref, qtm_ref, km_ref, vtm_ref, qtg_ref, kg_ref, vtg_ref):
    x = h_ref[0]
    xn = ((x * _rms_scale(x, D_MODEL)) * ng_ref[...]).astype(BF16)
    xp = jnp.dot(xn, wa_ref[...], preferred_element_type=F32)
    cos = cos_ref[...]
    sin_s = sin_ref[...]
    lane = lax.broadcasted_iota(jnp.int32, cos.shape, 1)
    lo64 = lane < HEAD64
    s64 = (HEAD64 ** -0.5) * LOG2E
    s192 = (MLA_QK ** -0.5) * LOG2E

    def head_pair(col, col_swapped, gains):
        xc = xp[:, col:col + LANES]
        xs = xp[:, col_swapped:col_swapped + LANES]
        r = _pair_rms_scale(xc, lo64)
        return _rope((xc * r) * gains[0:1, :], (xs * r) * gains[1:2, :], cos, sin_s)

    ones_rows = jnp.ones((ONES_ROWS, x.shape[0]), BF16)

    def store_vt(vt_ref, hd, v_slab):
        vt_ref[0, hd * VT_ROWS:hd * VT_ROWS + LANES, :] = v_slab.T.astype(BF16)
        vt_ref[0, hd * VT_ROWS + LANES:(hd + 1) * VT_ROWS, :] = ones_rows

    g_q, g_k = gt_ref[0:2, :], gt_ref[2:4, :]
    for c in range(4):
        sl = slice(c * LANES, (c + 1) * LANES)
        qtda_ref[0, sl, :] = (head_pair(A_DAQ + c * LANES, A_DAQS + c * LANES, g_q) * s64).T.astype(BF16)
        kda_ref[0, c] = head_pair(A_DAK + c * LANES, A_DAKS + c * LANES, g_k).astype(BF16)
        store_vt(vtda_ref, c, xp[:, A_DAV + c * LANES:A_DAV + (c + 1) * LANES])

    cq = xp[:, A_CQ:A_CQ + MLA_Q_RANK]
    cqn = ((cq * _rms_scale(cq, MLA_Q_RANK)) * gcq_ref[...]).astype(BF16)
    qm = jnp.dot(cqn, wuq_ref[...], preferred_element_type=F32)
    ckv = xp[:, A_CKV:A_CKV + MLA_KV_RANK]
    ckvn = ((ckv * _rms_scale(ckv, MLA_KV_RANK)) * gckv_ref[...]).astype(BF16)
    kn_all = jnp.dot(ckvn, wuk_ref[...], preferred_element_type=F32)
    vm_all = jnp.dot(ckvn, wuv_ref[...], preferred_element_type=F32)
    kr = xp[:, A_KR:A_KR + LANES]
    krs = xp[:, A_KRS:A_KRS + LANES]
    kr_ss = jnp.sum(kr * kr, axis=-1, keepdims=True)
    gq_n, gq_r, gq_rs = gt_ref[8:9, :], gt_ref[9:10, :], gt_ref[10:11, :]
    gk_n, gk_r, gk_rs = gt_ref[11:12, :], gt_ref[12:13, :], gt_ref[13:14, :]
    for hd in range(MLA_HEADS):
        b = hd * 2 * LANES
        qb = hd * 3 * LANES
        qn = qm[:, qb:qb + LANES]
        qr = qm[:, qb + LANES:qb + 2 * LANES]
        qrs = qm[:, qb + 2 * LANES:qb + 3 * LANES]
        r = lax.rsqrt((jnp.sum(qn * qn, axis=-1, keepdims=True) + jnp.sum(qr * qr, axis=-1, keepdims=True))
                      * (1.0 / MLA_QK) + EPS)
        qtm_ref[0, b:b + LANES, :] = (((qn * r) * gq_n) * s192).T.astype(BF16)
        qtm_ref[0, b + LANES:b + 2 * LANES, :] = (
            _rope((qr * r) * gq_r, (qrs * r) * gq_rs, cos, sin_s) * s192).T.astype(BF16)
        kn = kn_all[:, hd * LANES:(hd + 1) * LANES]
        rk = lax.rsqrt((jnp.sum(kn * kn, axis=-1, keepdims=True) + kr_ss) * (1.0 / MLA_QK) + EPS)
        km_ref[0, hd, :, :LANES] = ((kn * rk) * gk_n).astype(BF16)
        km_ref[0, hd, :, LANES:] = _rope((kr * rk) * gk_r, (krs * rk) * gk_rs, cos, sin_s).astype(BF16)
        store_vt(vtm_ref, hd, vm_all[:, hd * LANES:(hd + 1) * LANES])

    g_q, g_k = gt_ref[4:6, :], gt_ref[6:8, :]
    for c in range(4):
        sl = slice(c * LANES, (c + 1) * LANES)
        qtg_ref[0, sl, :] = (head_pair(A_GQ + c * LANES, A_GQS + c * LANES, g_q) * s64).T.astype(BF16)
    kg_ref[0] = head_pair(A_GK, A_GKS, g_k).astype(BF16)
    vgt = xp[:, A_GV:A_GV + LANES].T.astype(BF16)
    for g in range(GQA_HEADS // GQA_GROUP):
        vtg_ref[0, g * GQA_VT_ROWS:g * GQA_VT_ROWS + HEAD64, :] = vgt[g * HEAD64:(g + 1) * HEAD64]
        vtg_ref[0, g * GQA_VT_ROWS + HEAD64:(g + 1) * GQA_VT_ROWS, :] = ones_rows


def _prep_call(h, cos, sin, ng, wa, wuq, wuk, wuv, gt, gcq, gckv):
    B, Lp, D = h.shape
    tm = _pick(Lp, (384, 128))
    row = lambda b, j: (b, j, 0)
    colt = lambda b, j: (b, 0, j)
    tab = pl.BlockSpec((tm, LANES), lambda b, j: (j, 0))
    sds = lambda *s: jax.ShapeDtypeStruct(s, BF16)
    vt_rows = 4 * VT_ROWS
    out_shape = [sds(B, 512, Lp), sds(B, 4, Lp, LANES), sds(B, vt_rows, Lp),
                 sds(B, 1024, Lp), sds(B, MLA_HEADS, Lp, 2 * LANES), sds(B, vt_rows, Lp),
                 sds(B, 512, Lp), sds(B, Lp, LANES), sds(B, 2 * GQA_VT_ROWS, Lp)]
    hrow = lambda b, j: (b, 0, j, 0)
    out_specs = [pl.BlockSpec((1, 512, tm), colt), pl.BlockSpec((1, 4, tm, LANES), hrow), pl.BlockSpec((1, vt_rows, tm), colt),
                 pl.BlockSpec((1, 1024, tm), colt), pl.BlockSpec((1, MLA_HEADS, tm, 2 * LANES), hrow),
                 pl.BlockSpec((1, vt_rows, tm), colt),
                 pl.BlockSpec((1, 512, tm), colt), pl.BlockSpec((1, tm, LANES), row),
                 pl.BlockSpec((1, 2 * GQA_VT_ROWS, tm), colt)]
    in_specs = [pl.BlockSpec((1, tm, D), row), tab, tab, _resident(ng.shape), _resident(wa.shape),
                _resident(wuq.shape), _resident(wuk.shape), _resident(wuv.shape), _resident(gt.shape),
                _resident(gcq.shape), _resident(gckv.shape)]
    return pl.pallas_call(
        _prep_kernel, out_shape=out_shape, grid=(B, Lp // tm), in_specs=in_specs, out_specs=out_specs,
        compiler_params=_cparams(("parallel", "parallel")), name="prep",
    )(h, cos, sin, ng, wa, wuq, wuk, wuv, gt, gcq, gckv)


def _dense_attn_kernel(*refs, n_real, tk, differential, lam_init):
    if differential:
        qt_ref, k_ref, vt_ref, lam_ref, sg_ref, o_ref, s_ref = refs
    else:
        qt_ref, k_ref, vt_ref, o_ref, s_ref = refs
    tq = qt_ref.shape[2]
    dv = o_ref.shape[2]
    qt = qt_ref[0]
    if differential:
        z = jnp.zeros((HEAD64, tq), BF16)
        wq = jnp.concatenate([jnp.concatenate([qt[:HEAD64], z], axis=1),
                              jnp.concatenate([z, qt[HEAD64:]], axis=1)], axis=0)
    else:
        wq = qt
    width = wq.shape[1]

    def scores(kc):
        return jnp.dot(kc, wq, preferred_element_type=F32)

    def absorb(m, acc, s, vtc):
        m_new = jnp.maximum(m, jnp.max(s, axis=0, keepdims=True))
        p = jnp.exp2(s - m_new).astype(BF16)
        alpha = jnp.exp2(m - m_new)
        return m_new, alpha * acc + jnp.dot(vtc, p, preferred_element_type=F32)

    def kchunk(c):
        return k_ref[0, 0, pl.ds(pl.multiple_of(c * tk, tk), tk), :]

    def vchunk(c):
        return vt_ref[0, :, pl.ds(pl.multiple_of(c * tk, tk), tk)]

    def group(j, carry):
        m, acc = carry
        for g in range(CHUNK_GROUP):
            c = CHUNK_GROUP * j + g
            s_ref[(g + 1) % 2] = scores(kchunk(c + 1))
            m, acc = absorb(m, acc, s_ref[g % 2], vchunk(c))
        return m, acc

    n_chunks = n_real // tk
    n_groups = (n_chunks - 1) // CHUNK_GROUP
    s_ref[0] = scores(kchunk(0))
    carry = (jnp.full((1, width), NEG, F32), jnp.zeros((dv + ONES_ROWS, width), F32))
    m, acc = lax.fori_loop(0, n_groups, group, carry)
    valid = lax.broadcasted_iota(jnp.int32, (TAIL, width), 0) < N_META
    first = CHUNK_GROUP * n_groups
    for c in range(first, n_chunks):
        if c + 1 < n_chunks:
            s_ref[(c - first + 1) % 2] = scores(kchunk(c + 1))
        else:
            s_tail = jnp.where(valid, scores(k_ref[0, 0, n_real:n_real + TAIL, :]), NEG)
        m, acc = absorb(m, acc, s_ref[(c - first) % 2], vchunk(c))
    _, acc = absorb(m, acc, s_tail, vt_ref[0, :, n_real:n_real + TAIL])

    ot = acc[:dv] * (1.0 / acc[dv:dv + 1])
    if differential:
        lt = lam_ref[...]
        lam = (jnp.exp(jnp.sum(lt[0:1, :] * lt[1:2, :], axis=-1, keepdims=True))
               - jnp.exp(jnp.sum(lt[2:3, :] * lt[3:4, :], axis=-1, keepdims=True)) + lam_init)
        o = (ot[:, :tq] - lam * ot[:, tq:]).T
        o = ((o * _rms_scale(o, dv)) * sg_ref[...]) * (1.0 - lam_init)
    else:
        o = ot.T
    o_ref[0] = o.astype(o_ref.dtype)


def _dense_attn_call(qt, k, vt, n_real, *, differential, lam_tab=None, sub_g=None, lam_init=0.0, name):
    B, heads, Lp, dq = k.shape
    if differential:
        tq = _pick(Lp, (384, 128))
    else:
        tq = _pick(n_real, (512, 256, 128))
    tk = _pick(n_real, (512, 256, 128))
    in_specs = [pl.BlockSpec((1, dq, tq), lambda b, h, i: (b, h, i)),
                pl.BlockSpec((1, 1, Lp, dq), lambda b, h, i: (b, h, 0, 0)),
                pl.BlockSpec((1, VT_ROWS, Lp), lambda b, h, i: (b, h, 0))]
    args = [qt, k, vt]
    if differential:
        in_specs += [_resident(lam_tab.shape), _resident(sub_g.shape)]
        args += [lam_tab, sub_g]
    width = (2 if differential else 1) * tq
    kern = functools.partial(_dense_attn_kernel, n_real=n_real, tk=tk, differential=differential, lam_init=lam_init)
    return pl.pallas_call(
        kern, out_shape=jax.ShapeDtypeStruct((B, Lp, heads * LANES), BF16), grid=(B, heads, pl.cdiv(Lp, tq)),
        in_specs=in_specs, out_specs=pl.BlockSpec((1, tq, LANES), lambda b, h, i: (b, i, h)),
        scratch_shapes=[pltpu.VMEM((2, tk, width), F32)],
        compiler_params=_cparams(("parallel", "parallel", "parallel")), name=name,
    )(*args)


def _window_attn_kernel(sink_ref, qt_ref, k_ref, vt_ref, bias_ref, o_ref, *, n_real):
    tq = qt_ref.shape[2]
    t = pl.program_id(1)
    is_meta = t == n_real // tq
    span = tq + 2 * BLOCK
    start = pl.multiple_of(jnp.where(is_meta, 0, jnp.maximum(t * tq - BLOCK, 0)), BLOCK)
    kwin = jnp.concatenate([k_ref[0, pl.ds(start, span), :], k_ref[0, n_real:n_real + TAIL, :]], axis=0)
    vwin = jnp.concatenate([vt_ref[0, :, pl.ds(start, span)], vt_ref[0, :, n_real:n_real + TAIL]], axis=1)
    n_tiles = n_real // tq
    kind = jnp.where(is_meta, 4, (t == 0).astype(jnp.int32) + 2 * (t == n_tiles - 1).astype(jnp.int32))
    bias = jnp.concatenate([bias_ref[kind]] * GQA_GROUP, axis=1)
    qt = qt_ref[0]
    z = jnp.zeros((HEAD64, GQA_GROUP * tq), BF16)
    for g in range(GQA_HEADS // GQA_GROUP):
        heads = range(g * GQA_GROUP, (g + 1) * GQA_GROUP)
        qrow = jnp.concatenate([qt[h * HEAD64:(h + 1) * HEAD64] for h in heads], axis=1)
        wq = jnp.concatenate([qrow, z] if g == 0 else [z, qrow], axis=0)
        sink = jnp.concatenate([jnp.full((1, tq), sink_ref[h] * LOG2E, F32) for h in heads], axis=1)
        s = jnp.dot(kwin, wq, preferred_element_type=F32) + bias
        m = jnp.maximum(jnp.max(s, axis=0, keepdims=True), sink)
        p = jnp.exp2(s - m).astype(BF16)
        acc = jnp.dot(vwin[g * GQA_VT_ROWS:(g + 1) * GQA_VT_ROWS], p, preferred_element_type=F32)
        ot = acc[:HEAD64] * (1.0 / (acc[HEAD64:HEAD64 + 1] + jnp.exp2(sink - m)))
        for i in range(GQA_GROUP // 2):
            pair = jnp.concatenate([ot[:, 2 * i * tq:(2 * i + 1) * tq], ot[:, (2 * i + 1) * tq:(2 * i + 2) * tq]], axis=0)
            c = g * (GQA_GROUP // 2) + i
            o_ref[0, :, c * LANES:(c + 1) * LANES] = pair.T.astype(o_ref.dtype)


def _window_bias(n_real, tq):
    span = tq + 2 * BLOCK
    ks = jnp.arange(span + TAIL, dtype=jnp.int32)[:, None]
    qi = jnp.arange(tq, dtype=jnp.int32)[None, :]
    in_span = ks < span
    tail_ok = (ks >= span) & (ks < span + N_META)
    n_tiles = n_real // tq
    tables = []
    for first, last in ((False, False), (True, False), (False, True), (True, True)):
        t = 0 if first else (n_tiles - 1 if last else 1)
        kpos = max(t * tq - BLOCK, 0) + ks
        dist = kpos - (t * tq + qi)
        tables.append((dist <= WINDOW) & (dist >= -WINDOW) & in_span & (kpos < n_real) | tail_ok)
    tables.append(in_span & (ks <= qi + (WINDOW - N_META)) | tail_ok)
    return jnp.where(jnp.stack(tables), 0.0, NEG).astype(F32)


def _window_attn_call(qt, k, vt, sink, n_real):
    B, Lp, _ = k.shape
    tq = 256 if n_real % 256 == 0 and n_real >= 384 else BLOCK
    assert n_real % tq == 0 and tq + 2 * BLOCK <= Lp
    bias = _window_bias(n_real, tq)
    grid_spec = pltpu.PrefetchScalarGridSpec(
        num_scalar_prefetch=1, grid=(B, n_real // tq + 1),
        in_specs=[pl.BlockSpec((1, GQA_HEADS * HEAD64, tq), lambda b, t, s: (b, 0, t)),
                  pl.BlockSpec((1, Lp, LANES), lambda b, t, s: (b, 0, 0)),
                  pl.BlockSpec((1, 2 * GQA_VT_ROWS, Lp), lambda b, t, s: (b, 0, 0)),
                  _resident(bias.shape)],
        out_specs=pl.BlockSpec((1, tq, 512), lambda b, t, s: (b, t, 0)))
    return pl.pallas_call(
        functools.partial(_window_attn_kernel, n_real=n_real),
        out_shape=jax.ShapeDtypeStruct((B, Lp, 512), BF16), grid_spec=grid_spec,
        compiler_params=_cparams(("parallel", "parallel")), name="window_attn",
    )(sink, qt, k, vt, bias)


def _merge_kernel(h_ref, oda_ref, om_ref, og_ref, ng_ref, wg_ref, wb_ref, wo_ref, out_ref):
    x = h_ref[0]
    xn = ((x * _rms_scale(x, D_MODEL)) * ng_ref[...]).astype(BF16)
    gl = jnp.dot(xn, wg_ref[...], preferred_element_type=F32)
    merged = None
    for g, o_ref in enumerate((oda_ref, om_ref, og_ref)):
        proj = jnp.dot(o_ref[0], wb_ref[g], preferred_element_type=F32)
        term = jax.nn.sigmoid(gl[:, g * D_MODEL:(g + 1) * D_MODEL]) * proj
        merged = term if merged is None else merged + term
    out_ref[0] = x + jnp.dot(merged.astype(BF16), wo_ref[...], preferred_element_type=F32)


def _merge_call(h, oda, om, og, ng, wg, wb, wo):
    B, Lp, D = h.shape
    tm = _pick(Lp, (384, 128))
    row = lambda b, j: (b, j, 0)
    in_specs = [pl.BlockSpec((1, tm, D), row)] + [pl.BlockSpec((1, tm, BRANCH_WIDTH), row)] * 3 + [
        _resident(ng.shape), _resident(wg.shape), _resident(wb.shape), _resident(wo.shape)]
    return pl.pallas_call(
        _merge_kernel, out_shape=jax.ShapeDtypeStruct(h.shape, F32), grid=(B, Lp // tm),
        in_specs=in_specs, out_specs=pl.BlockSpec((1, tm, D), row),
        compiler_params=_cparams(("parallel", "parallel")), name="merge",
    )(h, oda, om, og, ng, wg, wb, wo)


def _mlp_kernel(h_ref, ng_ref, wu_ref, wd_ref, out_ref):
    x = h_ref[0]
    xn = ((x * _rms_scale(x, D_MODEL)) * ng_ref[...]).astype(BF16)
    u = jnp.dot(xn, wu_ref[...], preferred_element_type=F32)
    a = jnp.square(jnp.maximum(u, 0.0)).astype(BF16)
    out_ref[0] = x + jnp.dot(a, wd_ref[...], preferred_element_type=F32)


def _mlp_call(h, ng, wu, wd, out_rows):
    B, Lp, D = h.shape
    tm = _pick(Lp, (384, 128))
    row = lambda b, j: (b, j, 0)
    return pl.pallas_call(
        _mlp_kernel, out_shape=jax.ShapeDtypeStruct((B, out_rows, D), F32), grid=(B, pl.cdiv(out_rows, tm)),
        in_specs=[pl.BlockSpec((1, tm, D), row), _resident(ng.shape), _resident(wu.shape), _resident(wd.shape)],
        out_specs=pl.BlockSpec((1, tm, D), row),
        compiler_params=_cparams(("parallel", "parallel")), name="mlp",
    )(h, ng, wu, wd)


def _swap_halves(m):
    return jnp.flip(m.reshape(m.shape[:-1] + (m.shape[-1] // HEAD64, 2, HEAD64 // 2)), axis=-2).reshape(m.shape)


def _pair_gain(g):
    g = jnp.tile(g.astype(F32), 2)[None, :]
    return jnp.concatenate([g, _swap_halves(g)], axis=0)


def _pad_gain(g):
    return jnp.concatenate([g.astype(F32), jnp.zeros((LANES - g.shape[0],), F32)])[None, :]


def _layer_params(l, attn_norm_g, w_in, da_q_norm_g, da_k_norm_g, da_lam_q1, da_lam_k1, da_lam_q2, da_lam_k2,
                  da_subln_g, mla_cq_norm_g, mla_ckv_norm_g, mla_w_uq, mla_w_ukv, mla_q_norm_g, mla_k_norm_g,
                  gqa_q_norm_g, gqa_k_norm_g, gqa_sink, w_branch, w_out, mlp_norm_g, w_up, w_down):
    w = w_in[l]
    sec = [w[:, a:b] for a, b in zip(IN_OFF[:-1], IN_OFF[1:])]
    da_q, da_k, da_v, cq, ckv, kr, gq, gk, gv, gate = sec
    pad64 = lambda m: jnp.pad(m, ((0, 0),) * (m.ndim - 1) + ((0, LANES - HEAD64),))
    wa = jnp.concatenate([da_q, _swap_halves(da_q), da_k, _swap_halves(da_k), da_v, cq, ckv,
                          pad64(kr), pad64(_swap_halves(kr)), gq, _swap_halves(gq),
                          gk, _swap_halves(gk), gv], axis=1).astype(BF16)
    wuq = mla_w_uq[l].reshape(MLA_Q_RANK, MLA_HEADS, MLA_QK)
    wuq_r = wuq[:, :, MLA_NOPE:]
    wuq = jnp.concatenate([wuq[:, :, :MLA_NOPE], pad64(wuq_r), pad64(_swap_halves(wuq_r))], axis=-1)
    wuq = wuq.reshape(MLA_Q_RANK, MLA_HEADS * 3 * LANES).astype(BF16)
    wukv = mla_w_ukv[l].reshape(MLA_KV_RANK, MLA_HEADS, 2 * LANES)
    wuk = wukv[:, :, :MLA_NOPE].reshape(MLA_KV_RANK, MLA_HEADS * LANES).astype(BF16)
    wuv = wukv[:, :, MLA_NOPE:].reshape(MLA_KV_RANK, MLA_HEADS * LANES).astype(BF16)
    mq, mk = mla_q_norm_g[l], mla_k_norm_g[l]
    gt = jnp.concatenate([_pair_gain(da_q_norm_g[l]), _pair_gain(da_k_norm_g[l]),
                          _pair_gain(gqa_q_norm_g[l]), _pair_gain(gqa_k_norm_g[l]),
                          mq[None, :MLA_NOPE].astype(F32), _pad_gain(mq[MLA_NOPE:]), _pad_gain(_swap_halves(mq[MLA_NOPE:])),
                          mk[None, :MLA_NOPE].astype(F32), _pad_gain(mk[MLA_NOPE:]), _pad_gain(_swap_halves(mk[MLA_NOPE:])),
                          jnp.zeros((2, LANES), F32)], axis=0)
    lam_tab = jnp.concatenate([_pad_gain(v[l]) for v in (da_lam_q1, da_lam_k1, da_lam_q2, da_lam_k2)], axis=0)
    return dict(
        ng=attn_norm_g[l][None, :].astype(F32), wa=wa, wuq=wuq, wuk=wuk, wuv=wuv, gt=gt,
        gcq=mla_cq_norm_g[l][None, :].astype(F32), gckv=mla_ckv_norm_g[l][None, :].astype(F32),
        lam_tab=lam_tab, sub_g=da_subln_g[l][None, :].astype(F32), sink=gqa_sink[l].astype(F32),
        wg=gate.astype(BF16), wb=w_branch[l].astype(BF16), wo=w_out[l].astype(BF16),
        mg=mlp_norm_g[l][None, :].astype(F32), wu=w_up[l].astype(BF16), wd=w_down[l].astype(BF16))


def _rope_tables(n_real):
    r = jnp.arange(n_real + TAIL, dtype=jnp.int32)
    pos = jnp.where(r < n_real, r + N_META, jnp.where(r < n_real + N_META, r - n_real, 0)).astype(F32)
    inv_freq = 1.0 / (ROPE_THETA ** (jnp.arange(0, HEAD64, 2, dtype=F32) / HEAD64))
    ang = pos[:, None] * inv_freq[None, :]
    cos, sin = jnp.cos(ang), jnp.sin(ang)
    return jnp.tile(cos, (1, 4)), jnp.tile(jnp.concatenate([-sin, sin], axis=1), (1, 2))


def _trunk(x, meta_tokens, layers):
    B, S, D = x.shape
    assert D == D_MODEL and S % BLOCK == 0
    meta = jnp.broadcast_to(meta_tokens[None].astype(x.dtype), (B, N_META, D))
    h = jnp.concatenate([x, meta, jnp.zeros((B, TAIL - N_META, D), x.dtype)], axis=1)
    cos, sin = _rope_tables(S)
    for l, p in enumerate(layers):
        lam_init = 0.8 - 0.6 * math.exp(-0.3 * l)
        qtda, kda, vtda, qtm, km, vtm, qtg, kg, vtg = _prep_call(
            h, cos, sin, p["ng"], p["wa"], p["wuq"], p["wuk"], p["wuv"], p["gt"], p["gcq"], p["gckv"])
        oda = _dense_attn_call(qtda, kda, vtda, S, differential=True, lam_tab=p["lam_tab"], sub_g=p["sub_g"],
                               lam_init=lam_init, name="diff_attn")
        om = _dense_attn_call(qtm, km, vtm, S, differential=False, name="latent_attn")
        og = _window_attn_call(qtg, kg, vtg, p["sink"], S)
        h = _merge_call(h, oda, om, og, p["ng"], p["wg"], p["wb"], p["wo"])
        h = _mlp_call(h, p["mg"], p["wu"], p["wd"], out_rows=S if l == len(layers) - 1 else S + TAIL)
    return h


def kernel(x_prompt, x_sample, meta_tokens, attn_norm_g, w_in, da_q_norm_g, da_k_norm_g, da_lam_q1, da_lam_k1, da_lam_q2, da_lam_k2, da_subln_g, mla_cq_norm_g, mla_ckv_norm_g, mla_w_uq, mla_w_ukv, mla_q_norm_g, mla_k_norm_g, gqa_q_norm_g, gqa_k_norm_g, gqa_sink, w_branch, w_out, mlp_norm_g, w_up, w_down):
    depth = w_in.shape[0]
    layers = [_layer_params(l, attn_norm_g, w_in, da_q_norm_g, da_k_norm_g, da_lam_q1, da_lam_k1, da_lam_q2,
                            da_lam_k2, da_subln_g, mla_cq_norm_g, mla_ckv_norm_g, mla_w_uq, mla_w_ukv,
                            mla_q_norm_g, mla_k_norm_g, gqa_q_norm_g, gqa_k_norm_g, gqa_sink, w_branch, w_out,
                            mlp_norm_g, w_up, w_down) for l in range(depth)]
    return (_trunk(x_prompt, meta_tokens, layers), _trunk(x_sample, meta_tokens, layers))
```

```python
import functools
import math

import jax
import jax.numpy as jnp
from jax import lax
from jax.experimental import pallas as pl
from jax.experimental.pallas import tpu as pltpu

F32 = jnp.float32
BF16 = jnp.bfloat16

D_MODEL = 1024
N_META = 16
BLOCK = 128
WINDOW = 128
ROPE_THETA = 10000.0
EPS = 1e-6
NEG = -1e30
LOG2E = 1.4426950408889634

HEAD64 = 64
MLA_HEADS = 4
MLA_NOPE = 128
MLA_QK = 192
MLA_Q_RANK = 384
MLA_KV_RANK = 256
GQA_HEADS = 8
GQA_GROUP = 4
BRANCH_WIDTH = 512
N_BRANCH = 3
D_FF = 4 * D_MODEL
LANES = 128
TAIL = 128
ONES_ROWS = 16
VT_ROWS = LANES + ONES_ROWS
GQA_VT_ROWS = HEAD64 + ONES_ROWS
CHUNK_GROUP = 8
LATENT_TQ = 768

A_DAQ, A_DAQS, A_DAK, A_DAKS, A_DAV, A_CQ, A_CKV, A_KR, A_KRS, A_GQ, A_GQS, A_GK, A_GKS, A_GV, A_END = (
    0, 512, 1024, 1536, 2048, 2560, 2944, 3200, 3328, 3456, 3968, 4480, 4608, 4736, 4864)
IN_OFF = (0, 512, 1024, 1536, 1920, 2176, 2240, 2752, 2880, 3008, 6080)

VMEM_LIMIT = 56 * 1024 * 1024


def _pick(n, candidates):
    for c in candidates:
        if n % c == 0:
            return c
    raise ValueError(f"no tile in {candidates} divides {n}")


def _cparams(sem):
    return pltpu.CompilerParams(dimension_semantics=sem, vmem_limit_bytes=VMEM_LIMIT)


def _resident(shape):
    zeros = (0,) * len(shape)
    return pl.BlockSpec(shape, lambda *_: zeros, pipeline_mode=pl.Buffered(1))


def _rms_scale(x, width):
    return lax.rsqrt(jnp.sum(x * x, axis=-1, keepdims=True) * (1.0 / width) + EPS)


def _pair_rms_scale(xc, lo_mask):
    sq = xc * xc
    lo = jnp.sum(jnp.where(lo_mask, sq, 0.0), axis=-1, keepdims=True)
    hi = jnp.sum(jnp.where(lo_mask, 0.0, sq), axis=-1, keepdims=True)
    return jnp.where(lo_mask, lax.rsqrt(lo * (1.0 / HEAD64) + EPS), lax.rsqrt(hi * (1.0 / HEAD64) + EPS))


def _rope(xc, swapped, cos, sin_signed):
    return xc * cos + swapped * sin_signed


def _prep_kernel(h_ref, cos_ref, sin_ref, ng_ref, wa_ref, wuq_ref, wuk_ref, wuv_ref, gt_ref, gcq_ref, gckv_ref,
                 qtda_ref, kda_ref, vtda_ref, qtm_ref, km_ref, vtm_ref, qtg_ref, kg_ref, vtg_ref):
    x = h_ref[0]
    xn = ((x * _rms_scale(x, D_MODEL)) * ng_ref[...]).astype(BF16)
    xp = jnp.dot(xn, wa_ref[...], preferred_element_type=F32)
    cos = cos_ref[...]
    sin_s = sin_ref[...]
    lane = lax.broadcasted_iota(jnp.int32, cos.shape, 1)
    lo64 = lane < HEAD64
    s64 = (HEAD64 ** -0.5) * LOG2E
    s192 = (MLA_QK ** -0.5) * LOG2E

    def head_pair(col, col_swapped, gains):
        xc = xp[:, col:col + LANES]
        xs = xp[:, col_swapped:col_swapped + LANES]
        r = _pair_rms_scale(xc, lo64)
        return _rope((xc * r) * gains[0:1, :], (xs * r) * gains[1:2, :], cos, sin_s)

    ones_rows = jnp.ones((ONES_ROWS, x.shape[0]), BF16)

    def store_vt(vt_ref, hd, v_slab):
        vt_ref[0, hd * VT_ROWS:hd * VT_ROWS + LANES, :] = v_slab.T.astype(BF16)
        vt_ref[0, hd * VT_ROWS + LANES:(hd + 1) * VT_ROWS, :] = ones_rows

    g_q, g_k = gt_ref[0:2, :], gt_ref[2:4, :]
    for c in range(4):
        sl = slice(c * LANES, (c + 1) * LANES)
        qtda_ref[0, sl, :] = (head_pair(A_DAQ + c * LANES, A_DAQS + c * LANES, g_q) * s64).T.astype(BF16)
        kda_ref[0, c] = head_pair(A_DAK + c * LANES, A_DAKS + c * LANES, g_k).astype(BF16)
        store_vt(vtda_ref, c, xp[:, A_DAV + c * LANES:A_DAV + (c + 1) * LANES])

    cq = xp[:, A_CQ:A_CQ + MLA_Q_RANK]
    cqn = ((cq * _rms_scale(cq, MLA_Q_RANK)) * gcq_ref[...]).astype(BF16)
    qm = jnp.dot(cqn, wuq_ref[...], preferred_element_type=F32)
    ckv = xp[:, A_CKV:A_CKV + MLA_KV_RANK]
    ckvn = ((ckv * _rms_scale(ckv, MLA_KV_RANK)) * gckv_ref[...]).astype(BF16)
    kn_all = jnp.dot(ckvn, wuk_ref[...], preferred_element_type=F32)
    vm_all = jnp.dot(ckvn, wuv_ref[...], preferred_element_type=F32)
    kr = xp[:, A_KR:A_KR + LANES]
    krs = xp[:, A_KRS:A_KRS + LANES]
    kr_ss = jnp.sum(kr * kr, axis=-1, keepdims=True)
    gq_n, gq_r, gq_rs = gt_ref[8:9, :], gt_ref[9:10, :], gt_ref[10:11, :]
    gk_n, gk_r, gk_rs = gt_ref[11:12, :], gt_ref[12:13, :], gt_ref[13:14, :]
    for hd in range(MLA_HEADS):
        b = hd * 2 * LANES
        qb = hd * 3 * LANES
        qn = qm[:, qb:qb + LANES]
        qr = qm[:, qb + LANES:qb + 2 * LANES]
        qrs = qm[:, qb + 2 * LANES:qb + 3 * LANES]
        r = lax.rsqrt((jnp.sum(qn * qn, axis=-1, keepdims=True) + jnp.sum(qr * qr, axis=-1, keepdims=True))
                      * (1.0 / MLA_QK) + EPS)
        qtm_ref[0, b:b + LANES, :] = (((qn * r) * gq_n) * s192).T.astype(BF16)
        qtm_ref[0, b + LANES:b + 2 * LANES, :] = (
            _rope((qr * r) * gq_r, (qrs * r) * gq_rs, cos, sin_s) * s192).T.astype(BF16)
        kn = kn_all[:, hd * LANES:(hd + 1) * LANES]
        rk = lax.rsqrt((jnp.sum(kn * kn, axis=-1, keepdims=True) + kr_ss) * (1.0 / MLA_QK) + EPS)
        km_ref[0, hd, :, :LANES] = ((kn * rk) * gk_n).astype(BF16)
        km_ref[0, hd, :, LANES:] = _rope((kr * rk) * gk_r, (krs * rk) * gk_rs, cos, sin_s).astype(BF16)
        store_vt(vtm_ref, hd, vm_all[:, hd * LANES:(hd + 1) * LANES])

    g_q, g_k = gt_ref[4:6, :], gt_ref[6:8, :]
    for c in range(4):
        sl = slice(c * LANES, (c + 1) * LANES)
        qtg_ref[0, sl, :] = (head_pair(A_GQ + c * LANES, A_GQS + c * LANES, g_q) * s64).T.astype(BF16)
    kg_ref[0] = head_pair(A_GK, A_GKS, g_k).astype(BF16)
    vgt = xp[:, A_GV:A_GV + LANES].T.astype(BF16)
    for g in range(GQA_HEADS // GQA_GROUP):
        vtg_ref[0, g * GQA_VT_ROWS:g * GQA_VT_ROWS + HEAD64, :] = vgt[g * HEAD64:(g + 1) * HEAD64]
        vtg_ref[0, g * GQA_VT_ROWS + HEAD64:(g + 1) * GQA_VT_ROWS, :] = ones_rows


def _prep_call(h, cos, sin, ng, wa, wuq, wuk, wuv, gt, gcq, gckv):
    B, Lp, D = h.shape
    tm = _pick(Lp, (384, 128))
    row = lambda b, j: (b, j, 0)
    colt = lambda b, j: (b, 0, j)
    tab = pl.BlockSpec((tm, LANES), lambda b, j: (j, 0))
    sds = lambda *s: jax.ShapeDtypeStruct(s, BF16)
    vt_rows = 4 * VT_ROWS
    out_shape = [sds(B, 512, Lp), sds(B, 4, Lp, LANES), sds(B, vt_rows, Lp),
                 sds(B, 1024, Lp), sds(B, MLA_HEADS, Lp, 2 * LANES), sds(B, vt_rows, Lp),
                 sds(B, 512, Lp), sds(B, Lp, LANES), sds(B, 2 * GQA_VT_ROWS, Lp)]
    hrow = lambda b, j: (b, 0, j, 0)
    out_specs = [pl.BlockSpec((1, 512, tm), colt), pl.BlockSpec((1, 4, tm, LANES), hrow), pl.BlockSpec((1, vt_rows, tm), colt),
                 pl.BlockSpec((1, 1024, tm), colt), pl.BlockSpec((1, MLA_HEADS, tm, 2 * LANES), hrow),
                 pl.BlockSpec((1, vt_rows, tm), colt),
                 pl.BlockSpec((1, 512, tm), colt), pl.BlockSpec((1, tm, LANES), row),
                 pl.BlockSpec((1, 2 * GQA_VT_ROWS, tm), colt)]
    in_specs = [pl.BlockSpec((1, tm, D), row), tab, tab, _resident(ng.shape), _resident(wa.shape),
                _resident(wuq.shape), _resident(wuk.shape), _resident(wuv.shape), _resident(gt.shape),
                _resident(gcq.shape), _resident(gckv.shape)]
    return pl.pallas_call(
        _prep_kernel, out_shape=out_shape, grid=(B, Lp // tm), in_specs=in_specs, out_specs=out_specs,
        compiler_params=_cparams(("parallel", "parallel")), name="prep",
    )(h, cos, sin, ng, wa, wuq, wuk, wuv, gt, gcq, gckv)


def _dense_attn_kernel(*refs, n_real, tk, differential, lam_init):
    if differential:
        qt_ref, k_ref, vt_ref, lam_ref, sg_ref, o_ref, s_ref = refs
    else:
        qt_ref, k_ref, vt_ref, o_ref, s_ref = refs
    tq = qt_ref.shape[2]
    dv = o_ref.shape[2]
    qt = qt_ref[0]
    if differential:
        z = jnp.zeros((HEAD64, tq), BF16)
        wq = jnp.concatenate([jnp.concatenate([qt[:HEAD64], z], axis=1),
                              jnp.concatenate([z, qt[HEAD64:]], axis=1)], axis=0)
    else:
        wq = qt
    width = wq.shape[1]

    def scores(kc):
        return jnp.dot(kc, wq, preferred_element_type=F32)

    def absorb(m, acc, s, vtc):
        m_new = jnp.maximum(m, jnp.max(s, axis=0, keepdims=True))
        p = jnp.exp2(s - m_new).astype(BF16)
        alpha = jnp.exp2(m - m_new)
        return m_new, alpha * acc + jnp.dot(vtc, p, preferred_element_type=F32)

    def kchunk(c):
        return k_ref[0, 0, pl.ds(pl.multiple_of(c * tk, tk), tk), :]

    def vchunk(c):
        return vt_ref[0, :, pl.ds(pl.multiple_of(c * tk, tk), tk)]

    def group(j, carry):
        m, acc = carry
        for g in range(CHUNK_GROUP):
            c = CHUNK_GROUP * j + g
            s_ref[(g + 1) % 2] = scores(kchunk(c + 1))
            m, acc = absorb(m, acc, s_ref[g % 2], vchunk(c))
        return m, acc

    n_chunks = n_real // tk
    n_groups = (n_chunks - 1) // CHUNK_GROUP
    s_ref[0] = scores(kchunk(0))
    carry = (jnp.full((1, width), NEG, F32), jnp.zeros((dv + ONES_ROWS, width), F32))
    m, acc = lax.fori_loop(0, n_groups, group, carry)
    valid = lax.broadcasted_iota(jnp.int32, (TAIL, width), 0) < N_META
    first = CHUNK_GROUP * n_groups
    for c in range(first, n_chunks):
        if c + 1 < n_chunks:
            s_ref[(c - first + 1) % 2] = scores(kchunk(c + 1))
        else:
            s_tail = jnp.where(valid, scores(k_ref[0, 0, n_real:n_real + TAIL, :]), NEG)
        m, acc = absorb(m, acc, s_ref[(c - first) % 2], vchunk(c))
    _, acc = absorb(m, acc, s_tail, vt_ref[0, :, n_real:n_real + TAIL])

    ot = acc[:dv] * (1.0 / acc[dv:dv + 1])
    if differential:
        lt = lam_ref[...]
        lam = (jnp.exp(jnp.sum(lt[0:1, :] * lt[1:2, :], axis=-1, keepdims=True))
               - jnp.exp(jnp.sum(lt[2:3, :] * lt[3:4, :], axis=-1, keepdims=True)) + lam_init)
        o = (ot[:, :tq] - lam * ot[:, tq:]).T
        o = ((o * _rms_scale(o, dv)) * sg_ref[...]) * (1.0 - lam_init)
    else:
        o = ot.T
    o_ref[0] = o.astype(o_ref.dtype)


def _dense_attn_call(qt, k, vt, n_real, *, differential, lam_tab=None, sub_g=None, lam_init=0.0, name):
    B, heads, Lp, dq = k.shape
    if differential:
        tq = _pick(Lp, (384, 128))
    else:
        tq = LATENT_TQ if Lp >= LATENT_TQ else _pick(n_real, (512, 256, 128))
    tk = _pick(n_real, (512, 256, 128))
    in_specs = [pl.BlockSpec((1, dq, tq), lambda b, h, i: (b, h, i)),
                pl.BlockSpec((1, 1, Lp, dq), lambda b, h, i: (b, h, 0, 0)),
                pl.BlockSpec((1, VT_ROWS, Lp), lambda b, h, i: (b, h, 0))]
    args = [qt, k, vt]
    if differential:
        in_specs += [_resident(lam_tab.shape), _resident(sub_g.shape)]
        args += [lam_tab, sub_g]
    width = (2 if differential else 1) * tq
    kern = functools.partial(_dense_attn_kernel, n_real=n_real, tk=tk, differential=differential, lam_init=lam_init)
    return pl.pallas_call(
        kern, out_shape=jax.ShapeDtypeStruct((B, Lp, heads * LANES), BF16), grid=(B, heads, pl.cdiv(Lp, tq)),
        in_specs=in_specs, out_specs=pl.BlockSpec((1, tq, LANES), lambda b, h, i: (b, i, h)),
        scratch_shapes=[pltpu.VMEM((2, tk, width), F32)],
        compiler_params=_cparams(("parallel", "parallel", "parallel")), name=name,
    )(*args)


def _window_attn_kernel(sink_ref, qt_ref, k_ref, vt_ref, bias_ref, o_ref, *, n_real):
    tq = qt_ref.shape[2]
    t = pl.program_id(1)
    is_meta = t == n_real // tq
    span = tq + 2 * BLOCK
    start = pl.multiple_of(jnp.where(is_meta, 0, jnp.maximum(t * tq - BLOCK, 0)), BLOCK)
    kwin = jnp.concatenate([k_ref[0, pl.ds(start, span), :], k_ref[0, n_real:n_real + TAIL, :]], axis=0)
    vwin = jnp.concatenate([vt_ref[0, :, pl.ds(start, span)], vt_ref[0, :, n_real:n_real + TAIL]], axis=1)
    n_tiles = n_real // tq
    kind = jnp.where(is_meta, 4, (t == 0).astype(jnp.int32) + 2 * (t == n_tiles - 1).astype(jnp.int32))
    bias = jnp.concatenate([bias_ref[kind]] * GQA_GROUP, axis=1)
    qt = qt_ref[0]
    z = jnp.zeros((HEAD64, GQA_GROUP * tq), BF16)
    for g in range(GQA_HEADS // GQA_GROUP):
        heads = range(g * GQA_GROUP, (g + 1) * GQA_GROUP)
        qrow = jnp.concatenate([qt[h * HEAD64:(h + 1) * HEAD64] for h in heads], axis=1)
        wq = jnp.concatenate([qrow, z] if g == 0 else [z, qrow], axis=0)
        sink = jnp.concatenate([jnp.full((1, tq), sink_ref[h] * LOG2E, F32) for h in heads], axis=1)
        s = jnp.dot(kwin, wq, preferred_element_type=F32) + bias
        m = jnp.maximum(jnp.max(s, axis=0, keepdims=True), sink)
        p = jnp.exp2(s - m).astype(BF16)
        acc = jnp.dot(vwin[g * GQA_VT_ROWS:(g + 1) * GQA_VT_ROWS], p, preferred_element_type=F32)
        ot = acc[:HEAD64] * (1.0 / (acc[HEAD64:HEAD64 + 1] + jnp.exp2(sink - m)))
        for i in range(GQA_GROUP // 2):
            pair = jnp.concatenate([ot[:, 2 * i * tq:(2 * i + 1) * tq], ot[:, (2 * i + 1) * tq:(2 * i + 2) * tq]], axis=0)
            c = g * (GQA_GROUP // 2) + i
            o_ref[0, :, c * LANES:(c + 1) * LANES] = pair.T.astype(o_ref.dtype)


def _window_bias(n_real, tq):
    span = tq + 2 * BLOCK
    ks = jnp.arange(span + TAIL, dtype=jnp.int32)[:, None]
    qi = jnp.arange(tq, dtype=jnp.int32)[None, :]
    in_span = ks < span
    tail_ok = (ks >= span) & (ks < span + N_META)
    n_tiles = n_real // tq
    tables = []
    for first, last in ((False, False), (True, False), (False, True), (True, True)):
        t = 0 if first else (n_tiles - 1 if last else 1)
        kpos = max(t * tq - BLOCK, 0) + ks
        dist = kpos - (t * tq + qi)
        tables.append((dist <= WINDOW) & (dist >= -WINDOW) & in_span & (kpos < n_real) | tail_ok)
    tables.append(in_span & (ks <= qi + (WINDOW - N_META)) | tail_ok)
    return jnp.where(jnp.stack(tables), 0.0, NEG).astype(F32)


def _window_attn_call(qt, k, vt, sink, n_real):
    B, Lp, _ = k.shape
    tq = 256 if n_real % 256 == 0 and n_real >= 384 else BLOCK
    assert n_real % tq == 0 and tq + 2 * BLOCK <= Lp
    bias = _window_bias(n_real, tq)
    grid_spec = pltpu.PrefetchScalarGridSpec(
        num_scalar_prefetch=1, grid=(B, n_real // tq + 1),
        in_specs=[pl.BlockSpec((1, GQA_HEADS * HEAD64, tq), lambda b, t, s: (b, 0, t)),
                  pl.BlockSpec((1, Lp, LANES), lambda b, t, s: (b, 0, 0)),
                  pl.BlockSpec((1, 2 * GQA_VT_ROWS, Lp), lambda b, t, s: (b, 0, 0)),
                  _resident(bias.shape)],
        out_specs=pl.BlockSpec((1, tq, 512), lambda b, t, s: (b, t, 0)))
    return pl.pallas_call(
        functools.partial(_window_attn_kernel, n_real=n_real),
        out_shape=jax.ShapeDtypeStruct((B, Lp, 512), BF16), grid_spec=grid_spec,
        compiler_params=_cparams(("parallel", "parallel")), name="window_attn",
    )(sink, qt, k, vt, bias)


def _merge_kernel(h_ref, oda_ref, om_ref, og_ref, ng_ref, wg_ref, wb_ref, wo_ref, out_ref):
    x = h_ref[0]
    xn = ((x * _rms_scale(x, D_MODEL)) * ng_ref[...]).astype(BF16)
    gl = jnp.dot(xn, wg_ref[...], preferred_element_type=F32)
    merged = None
    for g, o_ref in enumerate((oda_ref, om_ref, og_ref)):
        proj = jnp.dot(o_ref[0], wb_ref[g], preferred_element_type=F32)
        term = jax.nn.sigmoid(gl[:, g * D_MODEL:(g + 1) * D_MODEL]) * proj
        merged = term if merged is None else merged + term
    out_ref[0] = x + jnp.dot(merged.astype(BF16), wo_ref[...], preferred_element_type=F32)


def _merge_call(h, oda, om, og, ng, wg, wb, wo):
    B, Lp, D = h.shape
    tm = _pick(Lp, (384, 128))
    row = lambda b, j: (b, j, 0)
    in_specs = [pl.BlockSpec((1, tm, D), row)] + [pl.BlockSpec((1, tm, BRANCH_WIDTH), row)] * 3 + [
        _resident(ng.shape), _resident(wg.shape), _resident(wb.shape), _resident(wo.shape)]
    return pl.pallas_call(
        _merge_kernel, out_shape=jax.ShapeDtypeStruct(h.shape, F32), grid=(B, Lp // tm),
        in_specs=in_specs, out_specs=pl.BlockSpec((1, tm, D), row),
        compiler_params=_cparams(("parallel", "parallel")), name="merge",
    )(h, oda, om, og, ng, wg, wb, wo)


def _mlp_kernel(h_ref, ng_ref, wu_ref, wd_ref, out_ref):
    x = h_ref[0]
    xn = ((x * _rms_scale(x, D_MODEL)) * ng_ref[...]).astype(BF16)
    u = jnp.dot(xn, wu_ref[...], preferred_element_type=F32)
    a = jnp.square(jnp.maximum(u, 0.0)).astype(BF16)
    out_ref[0] = x + jnp.dot(a, wd_ref[...], preferred_element_type=F32)


def _mlp_call(h, ng, wu, wd, out_rows):
    B, Lp, D = h.shape
    tm = _pick(Lp, (384, 128))
    row = lambda b, j: (b, j, 0)
    return pl.pallas_call(
        _mlp_kernel, out_shape=jax.ShapeDtypeStruct((B, out_rows, D), F32), grid=(B, pl.cdiv(out_rows, tm)),
        in_specs=[pl.BlockSpec((1, tm, D), row), _resident(ng.shape), _resident(wu.shape), _resident(wd.shape)],
        out_specs=pl.BlockSpec((1, tm, D), row),
        compiler_params=_cparams(("parallel", "parallel")), name="mlp",
    )(h, ng, wu, wd)


def _swap_halves(m):
    return jnp.flip(m.reshape(m.shape[:-1] + (m.shape[-1] // HEAD64, 2, HEAD64 // 2)), axis=-2).reshape(m.shape)


def _pair_gain(g):
    g = jnp.tile(g.astype(F32), 2)[None, :]
    return jnp.concatenate([g, _swap_halves(g)], axis=0)


def _pad_gain(g):
    return jnp.concatenate([g.astype(F32), jnp.zeros((LANES - g.shape[0],), F32)])[None, :]


def _layer_params(l, attn_norm_g, w_in, da_q_norm_g, da_k_norm_g, da_lam_q1, da_lam_k1, da_lam_q2, da_lam_k2,
                  da_subln_g, mla_cq_norm_g, mla_ckv_norm_g, mla_w_uq, mla_w_ukv, mla_q_norm_g, mla_k_norm_g,
                  gqa_q_norm_g, gqa_k_norm_g, gqa_sink, w_branch, w_out, mlp_norm_g, w_up, w_down):
    w = w_in[l]
    sec = [w[:, a:b] for a, b in zip(IN_OFF[:-1], IN_OFF[1:])]
    da_q, da_k, da_v, cq, ckv, kr, gq, gk, gv, gate = sec
    pad64 = lambda m: jnp.pad(m, ((0, 0),) * (m.ndim - 1) + ((0, LANES - HEAD64),))
    wa = jnp.concatenate([da_q, _swap_halves(da_q), da_k, _swap_halves(da_k), da_v, cq, ckv,
                          pad64(kr), pad64(_swap_halves(kr)), gq, _swap_halves(gq),
                          gk, _swap_halves(gk), gv], axis=1).astype(BF16)
    wuq = mla_w_uq[l].reshape(MLA_Q_RANK, MLA_HEADS, MLA_QK)
    wuq_r = wuq[:, :, MLA_NOPE:]
    wuq = jnp.concatenate([wuq[:, :, :MLA_NOPE], pad64(wuq_r), pad64(_swap_halves(wuq_r))], axis=-1)
    wuq = wuq.reshape(MLA_Q_RANK, MLA_HEADS * 3 * LANES).astype(BF16)
    wukv = mla_w_ukv[l].reshape(MLA_KV_RANK, MLA_HEADS, 2 * LANES)
    wuk = wukv[:, :, :MLA_NOPE].reshape(MLA_KV_RANK, MLA_HEADS * LANES).astype(BF16)
    wuv = wukv[:, :, MLA_NOPE:].reshape(MLA_KV_RANK, MLA_HEADS * LANES).astype(BF16)
    mq, mk = mla_q_norm_g[l], mla_k_norm_g[l]
    gt = jnp.concatenate([_pair_gain(da_q_norm_g[l]), _pair_gain(da_k_norm_g[l]),
                          _pair_gain(gqa_q_norm_g[l]), _pair_gain(gqa_k_norm_g[l]),
                          mq[None, :MLA_NOPE].astype(F32), _pad_gain(mq[MLA_NOPE:]), _pad_gain(_swap_halves(mq[MLA_NOPE:])),
                          mk[None, :MLA_NOPE].astype(F32), _pad_gain(mk[MLA_NOPE:]), _pad_gain(_swap_halves(mk[MLA_NOPE:])),
                          jnp.zeros((2, LANES), F32)], axis=0)
    lam_tab = jnp.concatenate([_pad_gain(v[l]) for v in (da_lam_q1, da_lam_k1, da_lam_q2, da_lam_k2)], axis=0)
    return dict(
        ng=attn_norm_g[l][None, :].astype(F32), wa=wa, wuq=wuq, wuk=wuk, wuv=wuv, gt=gt,
        gcq=mla_cq_norm_g[l][None, :].astype(F32), gckv=mla_ckv_norm_g[l][None, :].astype(F32),
        lam_tab=lam_tab, sub_g=da_subln_g[l][None, :].astype(F32), sink=gqa_sink[l].astype(F32),
        wg=gate.astype(BF16), wb=w_branch[l].astype(BF16), wo=w_out[l].astype(BF16),
        mg=mlp_norm_g[l][None, :].astype(F32), wu=w_up[l].astype(BF16), wd=w_down[l].astype(BF16))


def _rope_tables(n_real):
    r = jnp.arange(n_real + TAIL, dtype=jnp.int32)
    pos = jnp.where(r < n_real, r + N_META, jnp.where(r < n_real + N_META, r - n_real, 0)).astype(F32)
    inv_freq = 1.0 / (ROPE_THETA ** (jnp.arange(0, HEAD64, 2, dtype=F32) / HEAD64))
    ang = pos[:, None] * inv_freq[None, :]
    cos, sin = jnp.cos(ang), jnp.sin(ang)
    return jnp.tile(cos, (1, 4)), jnp.tile(jnp.concatenate([-sin, sin], axis=1), (1, 2))


def _trunk(x, meta_tokens, layers):
    B, S, D = x.shape
    assert D == D_MODEL and S % BLOCK == 0
    meta = jnp.broadcast_to(meta_tokens[None].astype(x.dtype), (B, N_META, D))
    h = jnp.concatenate([x, meta, jnp.zeros((B, TAIL - N_META, D), x.dtype)], axis=1)
    cos, sin = _rope_tables(S)
    for l, p in enumerate(layers):
        lam_init = 0.8 - 0.6 * math.exp(-0.3 * l)
        qtda, kda, vtda, qtm, km, vtm, qtg, kg, vtg = _prep_call(
            h, cos, sin, p["ng"], p["wa"], p["wuq"], p["wuk"], p["wuv"], p["gt"], p["gcq"], p["gckv"])
        oda = _dense_attn_call(qtda, kda, vtda, S, differential=True, lam_tab=p["lam_tab"], sub_g=p["sub_g"],
                               lam_init=lam_init, name="diff_attn")
        om = _dense_attn_call(qtm, km, vtm, S, differential=False, name="latent_attn")
        og = _window_attn_call(qtg, kg, vtg, p["sink"], S)
        h = _merge_call(h, oda, om, og, p["ng"], p["wg"], p["wb"], p["wo"])
        h = _mlp_call(h, p["mg"], p["wu"], p["wd"], out_rows=S if l == len(layers) - 1 else S + TAIL)
    return h


def kernel(x_prompt, x_sample, meta_tokens, attn_norm_g, w_in, da_q_norm_g, da_k_norm_g, da_lam_q1, da_lam_k1, da_lam_q2, da_lam_k2, da_subln_g, mla_cq_norm_g, mla_ckv_norm_g, mla_w_uq, mla_w_ukv, mla_q_norm_g, mla_k_norm_g, gqa_q_norm_g, gqa_k_norm_g, gqa_sink, w_branch, w_out, mlp_norm_g, w_up, w_down):
    depth = w_in.shape[0]
    layers = [_layer_params(l, attn_norm_g, w_in, da_q_norm_g, da_k_norm_g, da_lam_q1, da_lam_k1, da_lam_q2,
                            da_lam_k2, da_subln_g, mla_cq_norm_g, mla_ckv_norm_g, mla_w_uq, mla_w_ukv,
                            mla_q_norm_g, mla_k_norm_g, gqa_q_norm_g, gqa_k_norm_g, gqa_sink, w_branch, w_out,
                            mlp_norm_g, w_up, w_down) for l in range(depth)]
    return (_trunk(x_prompt, meta_tokens, layers), _trunk(x_sample, meta_tokens, layers))
```

```python
import functools
import math

import jax
import jax.numpy as jnp
from jax import lax
from jax.experimental import pallas as pl
from jax.experimental.pallas import tpu as pltpu

F32 = jnp.float32
BF16 = jnp.bfloat16

D_MODEL = 1024
N_META = 16
BLOCK = 128
WINDOW = 128
ROPE_THETA = 10000.0
EPS = 1e-6
NEG = -1e30
LOG2E = 1.4426950408889634

HEAD64 = 64
MLA_HEADS = 4
MLA_NOPE = 128
MLA_QK = 192
MLA_Q_RANK = 384
MLA_KV_RANK = 256
GQA_HEADS = 8
GQA_GROUP = 4
BRANCH_WIDTH = 512
N_BRANCH = 3
D_FF = 4 * D_MODEL
LANES = 128
TAIL = 128
ONES_ROWS = 16
VT_ROWS = LANES + ONES_ROWS
GQA_VT_ROWS = HEAD64 + ONES_ROWS
CHUNK_GROUP = 8
POST_TM = 384
LATENT_TQ = 768

A_DAQ, A_DAQS, A_DAK, A_DAKS, A_DAV, A_CQ, A_CKV, A_KR, A_KRS, A_GQ, A_GQS, A_GK, A_GKS, A_GV, A_END = (
    0, 512, 1024, 1536, 2048, 2560, 2944, 3200, 3328, 3456, 3968, 4480, 4608, 4736, 4864)
IN_OFF = (0, 512, 1024, 1536, 1920, 2176, 2240, 2752, 2880, 3008, 6080)

VMEM_LIMIT = 56 * 1024 * 1024


def _pick(n, candidates):
    for c in candidates:
        if n % c == 0:
            return c
    raise ValueError(f"no tile in {candidates} divides {n}")


def _cparams(sem):
    return pltpu.CompilerParams(dimension_semantics=sem, vmem_limit_bytes=VMEM_LIMIT)


def _resident(shape):
    zeros = (0,) * len(shape)
    return pl.BlockSpec(shape, lambda *_: zeros, pipeline_mode=pl.Buffered(1))


def _rms_scale(x, width):
    return lax.rsqrt(jnp.sum(x * x, axis=-1, keepdims=True) * (1.0 / width) + EPS)


def _pair_rms_scale(xc, lo_mask):
    sq = xc * xc
    lo = jnp.sum(jnp.where(lo_mask, sq, 0.0), axis=-1, keepdims=True)
    hi = jnp.sum(jnp.where(lo_mask, 0.0, sq), axis=-1, keepdims=True)
    return jnp.where(lo_mask, lax.rsqrt(lo * (1.0 / HEAD64) + EPS), lax.rsqrt(hi * (1.0 / HEAD64) + EPS))


def _rope(xc, swapped, cos, sin_signed):
    return xc * cos + swapped * sin_signed


def _stream_spec(h, tm):
    last = (h.shape[1] - 1) // tm
    return pl.BlockSpec((1, tm, h.shape[2]), lambda b, j: (b, jnp.minimum(j, last), 0))


def _row_tile(h_ref, tail_ref):
    x = h_ref[0]
    if tail_ref is None:
        return x
    tm = x.shape[0]
    tail = jnp.concatenate([jnp.zeros((tm - TAIL, x.shape[1]), x.dtype), tail_ref[...]], axis=0)
    in_tail = lax.broadcasted_iota(jnp.int32, x.shape, 0) >= tm - TAIL
    is_last = pl.program_id(1) == pl.num_programs(1) - 1
    return jnp.where(jnp.logical_and(is_last, in_tail), tail, x)


def _prep_kernel(*refs, has_tail):
    h_ref, tail_ref = (refs[0], refs[1]) if has_tail else (refs[0], None)
    (cos_ref, sin_ref, ng_ref, wa_ref, wuq_ref, wuk_ref, wuv_ref, gt_ref, gcq_ref, gckv_ref,
     qtda_ref, kda_ref, vtda_ref, qtm_ref, km_ref, vtm_ref, qtg_ref, kg_ref, vtg_ref) = refs[2 if has_tail else 1:]
    x = _row_tile(h_ref, tail_ref)
    xn = ((x * _rms_scale(x, D_MODEL)) * ng_ref[...]).astype(BF16)
    xp = jnp.dot(xn, wa_ref[...], preferred_element_type=F32)
    cos = cos_ref[...]
    sin_s = sin_ref[...]
    lane = lax.broadcasted_iota(jnp.int32, cos.shape, 1)
    lo64 = lane < HEAD64
    s64 = (HEAD64 ** -0.5) * LOG2E
    s192 = (MLA_QK ** -0.5) * LOG2E

    def head_pair(col, col_swapped, gains):
        xc = xp[:, col:col + LANES]
        xs = xp[:, col_swapped:col_swapped + LANES]
        r = _pair_rms_scale(xc, lo64)
        return _rope((xc * r) * gains[0:1, :], (xs * r) * gains[1:2, :], cos, sin_s)

    ones_rows = jnp.ones((ONES_ROWS, x.shape[0]), BF16)

    def store_vt(vt_ref, hd, v_slab):
        vt_ref[0, hd * VT_ROWS:hd * VT_ROWS + LANES, :] = v_slab.T.astype(BF16)
        vt_ref[0, hd * VT_ROWS + LANES:(hd + 1) * VT_ROWS, :] = ones_rows

    g_q, g_k = gt_ref[0:2, :], gt_ref[2:4, :]
    for c in range(4):
        sl = slice(c * LANES, (c + 1) * LANES)
        qtda_ref[0, sl, :] = (head_pair(A_DAQ + c * LANES, A_DAQS + c * LANES, g_q) * s64).T.astype(BF16)
        kda_ref[0, c] = head_pair(A_DAK + c * LANES, A_DAKS + c * LANES, g_k).astype(BF16)
        store_vt(vtda_ref, c, xp[:, A_DAV + c * LANES:A_DAV + (c + 1) * LANES])

    cq = xp[:, A_CQ:A_CQ + MLA_Q_RANK]
    cqn = ((cq * _rms_scale(cq, MLA_Q_RANK)) * gcq_ref[...]).astype(BF16)
    qm = jnp.dot(cqn, wuq_ref[...], preferred_element_type=F32)
    ckv = xp[:, A_CKV:A_CKV + MLA_KV_RANK]
    ckvn = ((ckv * _rms_scale(ckv, MLA_KV_RANK)) * gckv_ref[...]).astype(BF16)
    kn_all = jnp.dot(ckvn, wuk_ref[...], preferred_element_type=F32)
    vm_all = jnp.dot(ckvn, wuv_ref[...], preferred_element_type=F32)
    kr = xp[:, A_KR:A_KR + LANES]
    krs = xp[:, A_KRS:A_KRS + LANES]
    kr_ss = jnp.sum(kr * kr, axis=-1, keepdims=True)
    gq_n, gq_r, gq_rs = gt_ref[8:9, :], gt_ref[9:10, :], gt_ref[10:11, :]
    gk_n, gk_r, gk_rs = gt_ref[11:12, :], gt_ref[12:13, :], gt_ref[13:14, :]
    for hd in range(MLA_HEADS):
        b = hd * 2 * LANES
        qb = hd * 3 * LANES
        qn = qm[:, qb:qb + LANES]
        qr = qm[:, qb + LANES:qb + 2 * LANES]
        qrs = qm[:, qb + 2 * LANES:qb + 3 * LANES]
        r = lax.rsqrt((jnp.sum(qn * qn, axis=-1, keepdims=True) + jnp.sum(qr * qr, axis=-1, keepdims=True))
                      * (1.0 / MLA_QK) + EPS)
        qtm_ref[0, b:b + LANES, :] = (((qn * r) * gq_n) * s192).T.astype(BF16)
        qtm_ref[0, b + LANES:b + 2 * LANES, :] = (
            _rope((qr * r) * gq_r, (qrs * r) * gq_rs, cos, sin_s) * s192).T.astype(BF16)
        kn = kn_all[:, hd * LANES:(hd + 1) * LANES]
        rk = lax.rsqrt((jnp.sum(kn * kn, axis=-1, keepdims=True) + kr_ss) * (1.0 / MLA_QK) + EPS)
        km_ref[0, hd, :, :LANES] = ((kn * rk) * gk_n).astype(BF16)
        km_ref[0, hd, :, LANES:] = _rope((kr * rk) * gk_r, (krs * rk) * gk_rs, cos, sin_s).astype(BF16)
        store_vt(vtm_ref, hd, vm_all[:, hd * LANES:(hd + 1) * LANES])

    g_q, g_k = gt_ref[4:6, :], gt_ref[6:8, :]
    for c in range(4):
        sl = slice(c * LANES, (c + 1) * LANES)
        qtg_ref[0, sl, :] = (head_pair(A_GQ + c * LANES, A_GQS + c * LANES, g_q) * s64).T.astype(BF16)
    kg_ref[0] = head_pair(A_GK, A_GKS, g_k).astype(BF16)
    vgt = xp[:, A_GV:A_GV + LANES].T.astype(BF16)
    for g in range(GQA_HEADS // GQA_GROUP):
        vtg_ref[0, g * GQA_VT_ROWS:g * GQA_VT_ROWS + HEAD64, :] = vgt[g * HEAD64:(g + 1) * HEAD64]
        vtg_ref[0, g * GQA_VT_ROWS + HEAD64:(g + 1) * GQA_VT_ROWS, :] = ones_rows


def _prep_call(h, tail, cos, sin, ng, wa, wuq, wuk, wuv, gt, gcq, gckv):
    B, _, D = h.shape
    Lp = cos.shape[0]
    tm = _pick(Lp, (384, 128))
    row = lambda b, j: (b, j, 0)
    colt = lambda b, j: (b, 0, j)
    tab = pl.BlockSpec((tm, LANES), lambda b, j: (j, 0))
    sds = lambda *s: jax.ShapeDtypeStruct(s, BF16)
    vt_rows = 4 * VT_ROWS
    out_shape = [sds(B, 512, Lp), sds(B, 4, Lp, LANES), sds(B, vt_rows, Lp),
                 sds(B, 1024, Lp), sds(B, MLA_HEADS, Lp, 2 * LANES), sds(B, vt_rows, Lp),
                 sds(B, 512, Lp), sds(B, Lp, LANES), sds(B, 2 * GQA_VT_ROWS, Lp)]
    hrow = lambda b, j: (b, 0, j, 0)
    out_specs = [pl.BlockSpec((1, 512, tm), colt), pl.BlockSpec((1, 4, tm, LANES), hrow), pl.BlockSpec((1, vt_rows, tm), colt),
                 pl.BlockSpec((1, 1024, tm), colt), pl.BlockSpec((1, MLA_HEADS, tm, 2 * LANES), hrow),
                 pl.BlockSpec((1, vt_rows, tm), colt),
                 pl.BlockSpec((1, 512, tm), colt), pl.BlockSpec((1, tm, LANES), row),
                 pl.BlockSpec((1, 2 * GQA_VT_ROWS, tm), colt)]
    stream = [h] if tail is None else [h, tail]
    in_specs = [_stream_spec(h, tm)] + ([] if tail is None else [_resident(tail.shape)]) + [
        tab, tab, _resident(ng.shape), _resident(wa.shape), _resident(wuq.shape), _resident(wuk.shape),
        _resident(wuv.shape), _resident(gt.shape), _resident(gcq.shape), _resident(gckv.shape)]
    return pl.pallas_call(
        functools.partial(_prep_kernel, has_tail=tail is not None), out_shape=out_shape, grid=(B, Lp // tm),
        in_specs=in_specs, out_specs=out_specs,
        compiler_params=_cparams(("parallel", "parallel")), name="prep",
    )(*stream, cos, sin, ng, wa, wuq, wuk, wuv, gt, gcq, gckv)


def _dense_attn_kernel(*refs, n_real, tk, differential, lam_init):
    if differential:
        qt_ref, k_ref, vt_ref, lam_ref, sg_ref, o_ref, s_ref = refs
    else:
        qt_ref, k_ref, vt_ref, o_ref, s_ref = refs
    tq = qt_ref.shape[2]
    dv = o_ref.shape[2]
    qt = qt_ref[0]
    if differential:
        z = jnp.zeros((HEAD64, tq), BF16)
        wq = jnp.concatenate([jnp.concatenate([qt[:HEAD64], z], axis=1),
                              jnp.concatenate([z, qt[HEAD64:]], axis=1)], axis=0)
    else:
        wq = qt
    width = wq.shape[1]

    def scores(kc):
        return jnp.dot(kc, wq, preferred_element_type=F32)

    def absorb(m, acc, s, vtc):
        m_new = jnp.maximum(m, jnp.max(s, axis=0, keepdims=True))
        p = jnp.exp2(s - m_new).astype(BF16)
        alpha = jnp.exp2(m - m_new)
        return m_new, alpha * acc + jnp.dot(vtc, p, preferred_element_type=F32)

    def kchunk(c):
        return k_ref[0, 0, pl.ds(pl.multiple_of(c * tk, tk), tk), :]

    def vchunk(c):
        return vt_ref[0, :, pl.ds(pl.multiple_of(c * tk, tk), tk)]

    def group(j, carry):
        m, acc = carry
        for g in range(CHUNK_GROUP):
            c = CHUNK_GROUP * j + g
            s_ref[(g + 1) % 2] = scores(kchunk(c + 1))
            m, acc = absorb(m, acc, s_ref[g % 2], vchunk(c))
        return m, acc

    n_chunks = n_real // tk
    n_groups = (n_chunks - 1) // CHUNK_GROUP
    s_ref[0] = scores(kchunk(0))
    carry = (jnp.full((1, width), NEG, F32), jnp.zeros((dv + ONES_ROWS, width), F32))
    m, acc = lax.fori_loop(0, n_groups, group, carry)
    valid = lax.broadcasted_iota(jnp.int32, (TAIL, width), 0) < N_META
    first = CHUNK_GROUP * n_groups
    for c in range(first, n_chunks):
        if c + 1 < n_chunks:
            s_ref[(c - first + 1) % 2] = scores(kchunk(c + 1))
        else:
            s_tail = jnp.where(valid, scores(k_ref[0, 0, n_real:n_real + TAIL, :]), NEG)
        m, acc = absorb(m, acc, s_ref[(c - first) % 2], vchunk(c))
    _, acc = absorb(m, acc, s_tail, vt_ref[0, :, n_real:n_real + TAIL])

    ot = acc[:dv] * (1.0 / acc[dv:dv + 1])
    if differential:
        lt = lam_ref[...]
        lam = (jnp.exp(jnp.sum(lt[0:1, :] * lt[1:2, :], axis=-1, keepdims=True))
               - jnp.exp(jnp.sum(lt[2:3, :] * lt[3:4, :], axis=-1, keepdims=True)) + lam_init)
        o = (ot[:, :tq] - lam * ot[:, tq:]).T
        o = ((o * _rms_scale(o, dv)) * sg_ref[...]) * (1.0 - lam_init)
    else:
        o = ot.T
    o_ref[0] = o.astype(o_ref.dtype)


def _dense_attn_call(qt, k, vt, n_real, *, differential, lam_tab=None, sub_g=None, lam_init=0.0, name):
    B, heads, Lp, dq = k.shape
    if differential:
        tq = _pick(Lp, (384, 128))
    else:
        tq = LATENT_TQ if Lp >= LATENT_TQ else _pick(n_real, (512, 256, 128))
    tk = _pick(n_real, (512, 256, 128))
    in_specs = [pl.BlockSpec((1, dq, tq), lambda b, h, i: (b, h, i)),
                pl.BlockSpec((1, 1, Lp, dq), lambda b, h, i: (b, h, 0, 0)),
                pl.BlockSpec((1, VT_ROWS, Lp), lambda b, h, i: (b, h, 0))]
    args = [qt, k, vt]
    if differential:
        in_specs += [_resident(lam_tab.shape), _resident(sub_g.shape)]
        args += [lam_tab, sub_g]
    width = (2 if differential else 1) * tq
    kern = functools.partial(_dense_attn_kernel, n_real=n_real, tk=tk, differential=differential, lam_init=lam_init)
    return pl.pallas_call(
        kern, out_shape=jax.ShapeDtypeStruct((B, Lp, heads * LANES), BF16), grid=(B, heads, pl.cdiv(Lp, tq)),
        in_specs=in_specs, out_specs=pl.BlockSpec((1, tq, LANES), lambda b, h, i: (b, i, h)),
        scratch_shapes=[pltpu.VMEM((2, tk, width), F32)],
        compiler_params=_cparams(("parallel", "parallel", "parallel")), name=name,
    )(*args)


def _window_attn_kernel(sink_ref, qt_ref, k_ref, vt_ref, bias_ref, o_ref, *, n_real):
    tq = qt_ref.shape[2]
    t = pl.program_id(1)
    is_meta = t == n_real // tq
    span = tq + 2 * BLOCK
    start = pl.multiple_of(jnp.where(is_meta, 0, jnp.maximum(t * tq - BLOCK, 0)), BLOCK)
    kwin = jnp.concatenate([k_ref[0, pl.ds(start, span), :], k_ref[0, n_real:n_real + TAIL, :]], axis=0)
    vwin = jnp.concatenate([vt_ref[0, :, pl.ds(start, span)], vt_ref[0, :, n_real:n_real + TAIL]], axis=1)
    n_tiles = n_real // tq
    kind = jnp.where(is_meta, 4, (t == 0).astype(jnp.int32) + 2 * (t == n_tiles - 1).astype(jnp.int32))
    bias = jnp.concatenate([bias_ref[kind]] * GQA_GROUP, axis=1)
    qt = qt_ref[0]
    z = jnp.zeros((HEAD64, GQA_GROUP * tq), BF16)
    for g in range(GQA_HEADS // GQA_GROUP):
        heads = range(g * GQA_GROUP, (g + 1) * GQA_GROUP)
        qrow = jnp.concatenate([qt[h * HEAD64:(h + 1) * HEAD64] for h in heads], axis=1)
        wq = jnp.concatenate([qrow, z] if g == 0 else [z, qrow], axis=0)
        sink = jnp.concatenate([jnp.full((1, tq), sink_ref[h] * LOG2E, F32) for h in heads], axis=1)
        s = jnp.dot(kwin, wq, preferred_element_type=F32) + bias
        m = jnp.maximum(jnp.max(s, axis=0, keepdims=True), sink)
        p = jnp.exp2(s - m).astype(BF16)
        acc = jnp.dot(vwin[g * GQA_VT_ROWS:(g + 1) * GQA_VT_ROWS], p, preferred_element_type=F32)
        ot = acc[:HEAD64] * (1.0 / (acc[HEAD64:HEAD64 + 1] + jnp.exp2(sink - m)))
        for i in range(GQA_GROUP // 2):
            pair = jnp.concatenate([ot[:, 2 * i * tq:(2 * i + 1) * tq], ot[:, (2 * i + 1) * tq:(2 * i + 2) * tq]], axis=0)
            c = g * (GQA_GROUP // 2) + i
            o_ref[0, :, c * LANES:(c + 1) * LANES] = pair.T.astype(o_ref.dtype)


def _window_bias(n_real, tq):
    span = tq + 2 * BLOCK
    ks = jnp.arange(span + TAIL, dtype=jnp.int32)[:, None]
    qi = jnp.arange(tq, dtype=jnp.int32)[None, :]
    in_span = ks < span
    tail_ok = (ks >= span) & (ks < span + N_META)
    n_tiles = n_real // tq
    tables = []
    for first, last in ((False, False), (True, False), (False, True), (True, True)):
        t = 0 if first else (n_tiles - 1 if last else 1)
        kpos = max(t * tq - BLOCK, 0) + ks
        dist = kpos - (t * tq + qi)
        tables.append((dist <= WINDOW) & (dist >= -WINDOW) & in_span & (kpos < n_real) | tail_ok)
    tables.append(in_span & (ks <= qi + (WINDOW - N_META)) | tail_ok)
    return jnp.where(jnp.stack(tables), 0.0, NEG).astype(F32)


def _window_attn_call(qt, k, vt, sink, n_real):
    B, Lp, _ = k.shape
    tq = 256 if n_real % 256 == 0 and n_real >= 384 else BLOCK
    assert n_real % tq == 0 and tq + 2 * BLOCK <= Lp
    bias = _window_bias(n_real, tq)
    grid_spec = pltpu.PrefetchScalarGridSpec(
        num_scalar_prefetch=1, grid=(B, n_real // tq + 1),
        in_specs=[pl.BlockSpec((1, GQA_HEADS * HEAD64, tq), lambda b, t, s: (b, 0, t)),
                  pl.BlockSpec((1, Lp, LANES), lambda b, t, s: (b, 0, 0)),
                  pl.BlockSpec((1, 2 * GQA_VT_ROWS, Lp), lambda b, t, s: (b, 0, 0)),
                  _resident(bias.shape)],
        out_specs=pl.BlockSpec((1, tq, 512), lambda b, t, s: (b, t, 0)))
    return pl.pallas_call(
        functools.partial(_window_attn_kernel, n_real=n_real),
        out_shape=jax.ShapeDtypeStruct((B, Lp, 512), BF16), grid_spec=grid_spec,
        compiler_params=_cparams(("parallel", "parallel")), name="window_attn",
    )(sink, qt, k, vt, bias)


def _post_kernel(*refs, has_tail):
    h_ref, tail_ref = (refs[0], refs[1]) if has_tail else (refs[0], None)
    oda_ref, om_ref, og_ref, ng_ref, wg_ref, wb_ref, wo_ref, mg_ref, wu_ref, wd_ref, out_ref = refs[2 if has_tail else 1:]
    x = _row_tile(h_ref, tail_ref)
    xn = ((x * _rms_scale(x, D_MODEL)) * ng_ref[...]).astype(BF16)
    gl = jnp.dot(xn, wg_ref[...], preferred_element_type=F32)
    merged = None
    for g, o_ref in enumerate((oda_ref, om_ref, og_ref)):
        proj = jnp.dot(o_ref[0], wb_ref[g], preferred_element_type=F32)
        term = jax.nn.sigmoid(gl[:, g * D_MODEL:(g + 1) * D_MODEL]) * proj
        merged = term if merged is None else merged + term
    h1 = x + jnp.dot(merged.astype(BF16), wo_ref[...], preferred_element_type=F32)
    h1n = ((h1 * _rms_scale(h1, D_MODEL)) * mg_ref[...]).astype(BF16)
    u = jnp.dot(h1n, wu_ref[...], preferred_element_type=F32)
    a = jnp.square(jnp.maximum(u, 0.0)).astype(BF16)
    out_ref[0] = h1 + jnp.dot(a, wd_ref[...], preferred_element_type=F32)


def _post_call(h, tail, oda, om, og, ng, wg, wb, wo, mg, wu, wd, out_rows):
    B, Lp, _ = oda.shape
    D = h.shape[2]
    tm = _pick(Lp, (POST_TM, 128))
    row = lambda b, j: (b, j, 0)
    stream = [h] if tail is None else [h, tail]
    in_specs = [_stream_spec(h, tm)] + ([] if tail is None else [_resident(tail.shape)]) + [
        pl.BlockSpec((1, tm, BRANCH_WIDTH), row)] * 3 + [_resident(a.shape) for a in (ng, wg, wb, wo, mg, wu, wd)]
    return pl.pallas_call(
        functools.partial(_post_kernel, has_tail=tail is not None),
        out_shape=jax.ShapeDtypeStruct((B, out_rows, D), F32), grid=(B, pl.cdiv(out_rows, tm)),
        in_specs=in_specs, out_specs=pl.BlockSpec((1, tm, D), row),
        compiler_params=_cparams(("parallel", "parallel")), name="post",
    )(*stream, oda, om, og, ng, wg, wb, wo, mg, wu, wd)


def _swap_halves(m):
    return jnp.flip(m.reshape(m.shape[:-1] + (m.shape[-1] // HEAD64, 2, HEAD64 // 2)), axis=-2).reshape(m.shape)


def _pair_gain(g):
    g = jnp.tile(g.astype(F32), 2)[None, :]
    return jnp.concatenate([g, _swap_halves(g)], axis=0)


def _pad_gain(g):
    return jnp.concatenate([g.astype(F32), jnp.zeros((LANES - g.shape[0],), F32)])[None, :]


def _layer_params(l, attn_norm_g, w_in, da_q_norm_g, da_k_norm_g, da_lam_q1, da_lam_k1, da_lam_q2, da_lam_k2,
                  da_subln_g, mla_cq_norm_g, mla_ckv_norm_g, mla_w_uq, mla_w_ukv, mla_q_norm_g, mla_k_norm_g,
                  gqa_q_norm_g, gqa_k_norm_g, gqa_sink, w_branch, w_out, mlp_norm_g, w_up, w_down):
    w = w_in[l]
    sec = [w[:, a:b] for a, b in zip(IN_OFF[:-1], IN_OFF[1:])]
    da_q, da_k, da_v, cq, ckv, kr, gq, gk, gv, gate = sec
    pad64 = lambda m: jnp.pad(m, ((0, 0),) * (m.ndim - 1) + ((0, LANES - HEAD64),))
    wa = jnp.concatenate([da_q, _swap_halves(da_q), da_k, _swap_halves(da_k), da_v, cq, ckv,
                          pad64(kr), pad64(_swap_halves(kr)), gq, _swap_halves(gq),
                          gk, _swap_halves(gk), gv], axis=1).astype(BF16)
    wuq = mla_w_uq[l].reshape(MLA_Q_RANK, MLA_HEADS, MLA_QK)
    wuq_r = wuq[:, :, MLA_NOPE:]
    wuq = jnp.concatenate([wuq[:, :, :MLA_NOPE], pad64(wuq_r), pad64(_swap_halves(wuq_r))], axis=-1)
    wuq = wuq.reshape(MLA_Q_RANK, MLA_HEADS * 3 * LANES).astype(BF16)
    wukv = mla_w_ukv[l].reshape(MLA_KV_RANK, MLA_HEADS, 2 * LANES)
    wuk = wukv[:, :, :MLA_NOPE].reshape(MLA_KV_RANK, MLA_HEADS * LANES).astype(BF16)
    wuv = wukv[:, :, MLA_NOPE:].reshape(MLA_KV_RANK, MLA_HEADS * LANES).astype(BF16)
    mq, mk = mla_q_norm_g[l], mla_k_norm_g[l]
    gt = jnp.concatenate([_pair_gain(da_q_norm_g[l]), _pair_gain(da_k_norm_g[l]),
                          _pair_gain(gqa_q_norm_g[l]), _pair_gain(gqa_k_norm_g[l]),
                          mq[None, :MLA_NOPE].astype(F32), _pad_gain(mq[MLA_NOPE:]), _pad_gain(_swap_halves(mq[MLA_NOPE:])),
                          mk[None, :MLA_NOPE].astype(F32), _pad_gain(mk[MLA_NOPE:]), _pad_gain(_swap_halves(mk[MLA_NOPE:])),
                          jnp.zeros((2, LANES), F32)], axis=0)
    lam_tab = jnp.concatenate([_pad_gain(v[l]) for v in (da_lam_q1, da_lam_k1, da_lam_q2, da_lam_k2)], axis=0)
    return dict(
        ng=attn_norm_g[l][None, :].astype(F32), wa=wa, wuq=wuq, wuk=wuk, wuv=wuv, gt=gt,
        gcq=mla_cq_norm_g[l][None, :].astype(F32), gckv=mla_ckv_norm_g[l][None, :].astype(F32),
        lam_tab=lam_tab, sub_g=da_subln_g[l][None, :].astype(F32), sink=gqa_sink[l].astype(F32),
        wg=gate.astype(BF16), wb=w_branch[l].astype(BF16), wo=w_out[l].astype(BF16),
        mg=mlp_norm_g[l][None, :].astype(F32), wu=w_up[l].astype(BF16), wd=w_down[l].astype(BF16))


def _rope_tables(n_real):
    r = jnp.arange(n_real + TAIL, dtype=jnp.int32)
    pos = jnp.where(r < n_real, r + N_META, jnp.where(r < n_real + N_META, r - n_real, 0)).astype(F32)
    inv_freq = 1.0 / (ROPE_THETA ** (jnp.arange(0, HEAD64, 2, dtype=F32) / HEAD64))
    ang = pos[:, None] * inv_freq[None, :]
    cos, sin = jnp.cos(ang), jnp.sin(ang)
    return jnp.tile(cos, (1, 4)), jnp.tile(jnp.concatenate([-sin, sin], axis=1), (1, 2))


def _trunk(x, meta_tokens, layers):
    B, S, D = x.shape
    assert D == D_MODEL and S % BLOCK == 0
    tail = jnp.concatenate([meta_tokens.astype(x.dtype), jnp.zeros((TAIL - N_META, D), x.dtype)], axis=0)
    h = x
    cos, sin = _rope_tables(S)
    for l, p in enumerate(layers):
        lam_init = 0.8 - 0.6 * math.exp(-0.3 * l)
        qtda, kda, vtda, qtm, km, vtm, qtg, kg, vtg = _prep_call(
            h, tail, cos, sin, p["ng"], p["wa"], p["wuq"], p["wuk"], p["wuv"], p["gt"], p["gcq"], p["gckv"])
        oda = _dense_attn_call(qtda, kda, vtda, S, differential=True, lam_tab=p["lam_tab"], sub_g=p["sub_g"],
                               lam_init=lam_init, name="diff_attn")
        om = _dense_attn_call(qtm, km, vtm, S, differential=False, name="latent_attn")
        og = _window_attn_call(qtg, kg, vtg, p["sink"], S)
        h = _post_call(h, tail, oda, om, og, p["ng"], p["wg"], p["wb"], p["wo"], p["mg"], p["wu"], p["wd"],
                       out_rows=S if l == len(layers) - 1 else S + TAIL)
        tail = None
    return h


def kernel(x_prompt, x_sample, meta_tokens, attn_norm_g, w_in, da_q_norm_g, da_k_norm_g, da_lam_q1, da_lam_k1, da_lam_q2, da_lam_k2, da_subln_g, mla_cq_norm_g, mla_ckv_norm_g, mla_w_uq, mla_w_ukv, mla_q_norm_g, mla_k_norm_g, gqa_q_norm_g, gqa_k_norm_g, gqa_sink, w_branch, w_out, mlp_norm_g, w_up, w_down):
    depth = w_in.shape[0]
    layers = [_layer_params(l, attn_norm_g, w_in, da_q_norm_g, da_k_norm_g, da_lam_q1, da_lam_k1, da_lam_q2,
                            da_lam_k2, da_subln_g, mla_cq_norm_g, mla_ckv_norm_g, mla_w_uq, mla_w_ukv,
                            mla_q_norm_g, mla_k_norm_g, gqa_q_norm_g, gqa_k_norm_g, gqa_sink, w_branch, w_out,
                            mlp_norm_g, w_up, w_down) for l in range(depth)]
    return (_trunk(x_prompt, meta_tokens, layers), _trunk(x_sample, meta_tokens, layers))
```

```python
import functools
import math

import jax
import jax.numpy as jnp
from jax import lax
from jax.experimental import pallas as pl
from jax.experimental.pallas import tpu as pltpu

F32 = jnp.float32
BF16 = jnp.bfloat16

D_MODEL = 1024
N_META = 16
BLOCK = 128
WINDOW = 128
ROPE_THETA = 10000.0
EPS = 1e-6
NEG = -1e30
LOG2E = 1.4426950408889634

HEAD64 = 64
MLA_HEADS = 4
MLA_NOPE = 128
MLA_QK = 192
MLA_Q_RANK = 384
MLA_KV_RANK = 256
GQA_HEADS = 8
GQA_GROUP = 4
BRANCH_WIDTH = 512
N_BRANCH = 3
D_FF = 4 * D_MODEL
LANES = 128
TAIL = 128
ONES_ROWS = 16
VT_ROWS = LANES + ONES_ROWS
GQA_VT_ROWS = HEAD64 + ONES_ROWS
CHUNK_GROUP = 8
POST_TM = 384
LATENT_TQ = 768

A_DAQ, A_DAK, A_DAV, A_CQ, A_CKV, A_KR, A_KRS, A_GQ, A_GQS, A_GK, A_GKS, A_GV, A_END = (
    0, 512, 1024, 1536, 1920, 2176, 2304, 2432, 2944, 3456, 3584, 3712, 3840)
IN_OFF = (0, 512, 1024, 1536, 1920, 2176, 2240, 2752, 2880, 3008, 6080)

VMEM_LIMIT = 56 * 1024 * 1024


def _pick(n, candidates):
    for c in candidates:
        if n % c == 0:
            return c
    raise ValueError(f"no tile in {candidates} divides {n}")


def _cparams(sem):
    return pltpu.CompilerParams(dimension_semantics=sem, vmem_limit_bytes=VMEM_LIMIT)


def _resident(shape):
    zeros = (0,) * len(shape)
    return pl.BlockSpec(shape, lambda *_: zeros, pipeline_mode=pl.Buffered(1))


def _rms_scale(x, width):
    return lax.rsqrt(jnp.sum(x * x, axis=-1, keepdims=True) * (1.0 / width) + EPS)


def _pair_rms_scale(xc, lo_mask):
    sq = xc * xc
    lo = jnp.sum(jnp.where(lo_mask, sq, 0.0), axis=-1, keepdims=True)
    hi = jnp.sum(jnp.where(lo_mask, 0.0, sq), axis=-1, keepdims=True)
    return jnp.where(lo_mask, lax.rsqrt(lo * (1.0 / HEAD64) + EPS), lax.rsqrt(hi * (1.0 / HEAD64) + EPS))


def _rope(xc, swapped, cos, sin_signed):
    return xc * cos + swapped * sin_signed


def _stream_spec(h, tm):
    last = (h.shape[1] - 1) // tm
    return pl.BlockSpec((1, tm, h.shape[2]), lambda b, j: (b, jnp.minimum(j, last), 0))


def _row_tile(h_ref, tail_ref):
    x = h_ref[0]
    if tail_ref is None:
        return x
    tm = x.shape[0]
    tail = jnp.concatenate([jnp.zeros((tm - TAIL, x.shape[1]), x.dtype), tail_ref[...]], axis=0)
    in_tail = lax.broadcasted_iota(jnp.int32, x.shape, 0) >= tm - TAIL
    is_last = pl.program_id(1) == pl.num_programs(1) - 1
    return jnp.where(jnp.logical_and(is_last, in_tail), tail, x)


def _prep_kernel(*refs, has_tail):
    h_ref, tail_ref = (refs[0], refs[1]) if has_tail else (refs[0], None)
    (cos_ref, sin_ref, ng_ref, wa_ref, wuq_ref, wuk_ref, wuv_ref, gt_ref, gcq_ref, gckv_ref,
     qtda_ref, kda_ref, vtda_ref, qtm_ref, km_ref, vtm_ref, qtg_ref, kg_ref, vtg_ref) = refs[2 if has_tail else 1:]
    x = _row_tile(h_ref, tail_ref)
    xn = ((x * _rms_scale(x, D_MODEL)) * ng_ref[...]).astype(BF16)
    xp = jnp.dot(xn, wa_ref[...], preferred_element_type=F32)
    cos = cos_ref[...]
    sin_s = sin_ref[...]
    lane = lax.broadcasted_iota(jnp.int32, cos.shape, 1)
    lo64 = lane < HEAD64
    s64 = (HEAD64 ** -0.5) * LOG2E
    s192 = (MLA_QK ** -0.5) * LOG2E

    first_half = (lane & 32) == 0

    def head_pair(col, col_swapped, gains):
        xc = xp[:, col:col + LANES]
        r = _pair_rms_scale(xc, lo64)
        y = (xc * r) * gains[0:1, :]
        if col_swapped is None:
            ys = jnp.where(first_half, pltpu.roll(y, LANES - 32, 1), pltpu.roll(y, 32, 1))
        else:
            ys = (xp[:, col_swapped:col_swapped + LANES] * r) * gains[1:2, :]
        return _rope(y, ys, cos, sin_s)

    ones_rows = jnp.ones((ONES_ROWS, x.shape[0]), BF16)

    def store_vt(vt_ref, hd, v_slab):
        vt_ref[0, hd * VT_ROWS:hd * VT_ROWS + LANES, :] = v_slab.T.astype(BF16)
        vt_ref[0, hd * VT_ROWS + LANES:(hd + 1) * VT_ROWS, :] = ones_rows

    g_q, g_k = gt_ref[0:2, :], gt_ref[2:4, :]
    for c in range(4):
        sl = slice(c * LANES, (c + 1) * LANES)
        qtda_ref[0, sl, :] = (head_pair(A_DAQ + c * LANES, None, g_q) * s64).T.astype(BF16)
        kda_ref[0, c] = head_pair(A_DAK + c * LANES, None, g_k).astype(BF16)
        store_vt(vtda_ref, c, xp[:, A_DAV + c * LANES:A_DAV + (c + 1) * LANES])

    cq = xp[:, A_CQ:A_CQ + MLA_Q_RANK]
    cqn = ((cq * _rms_scale(cq, MLA_Q_RANK)) * gcq_ref[...]).astype(BF16)
    qm = jnp.dot(cqn, wuq_ref[...], preferred_element_type=F32)
    ckv = xp[:, A_CKV:A_CKV + MLA_KV_RANK]
    ckvn = ((ckv * _rms_scale(ckv, MLA_KV_RANK)) * gckv_ref[...]).astype(BF16)
    kn_all = jnp.dot(ckvn, wuk_ref[...], preferred_element_type=F32)
    vm_all = jnp.dot(ckvn, wuv_ref[...], preferred_element_type=F32)
    kr = xp[:, A_KR:A_KR + LANES]
    krs = xp[:, A_KRS:A_KRS + LANES]
    kr_ss = jnp.sum(kr * kr, axis=-1, keepdims=True)
    gq_n, gq_r, gq_rs = gt_ref[8:9, :], gt_ref[9:10, :], gt_ref[10:11, :]
    gk_n, gk_r, gk_rs = gt_ref[11:12, :], gt_ref[12:13, :], gt_ref[13:14, :]
    for hd in range(MLA_HEADS):
        b = hd * 2 * LANES
        qb = hd * 3 * LANES
        qn = qm[:, qb:qb + LANES]
        qr = qm[:, qb + LANES:qb + 2 * LANES]
        qrs = qm[:, qb + 2 * LANES:qb + 3 * LANES]
        r = lax.rsqrt((jnp.sum(qn * qn, axis=-1, keepdims=True) + jnp.sum(qr * qr, axis=-1, keepdims=True))
                      * (1.0 / MLA_QK) + EPS)
        qtm_ref[0, b:b + LANES, :] = (((qn * r) * gq_n) * s192).T.astype(BF16)
        qtm_ref[0, b + LANES:b + 2 * LANES, :] = (
            _rope((qr * r) * gq_r, (qrs * r) * gq_rs, cos, sin_s) * s192).T.astype(BF16)
        kn = kn_all[:, hd * LANES:(hd + 1) * LANES]
        rk = lax.rsqrt((jnp.sum(kn * kn, axis=-1, keepdims=True) + kr_ss) * (1.0 / MLA_QK) + EPS)
        km_ref[0, hd, :, :LANES] = ((kn * rk) * gk_n).astype(BF16)
        km_ref[0, hd, :, LANES:] = _rope((kr * rk) * gk_r, (krs * rk) * gk_rs, cos, sin_s).astype(BF16)
        store_vt(vtm_ref, hd, vm_all[:, hd * LANES:(hd + 1) * LANES])

    g_q, g_k = gt_ref[4:6, :], gt_ref[6:8, :]
    for c in range(4):
        sl = slice(c * LANES, (c + 1) * LANES)
        qtg_ref[0, sl, :] = (head_pair(A_GQ + c * LANES, A_GQS + c * LANES, g_q) * s64).T.astype(BF16)
    kg_ref[0] = head_pair(A_GK, A_GKS, g_k).astype(BF16)
    vgt = xp[:, A_GV:A_GV + LANES].T.astype(BF16)
    for g in range(GQA_HEADS // GQA_GROUP):
        vtg_ref[0, g * GQA_VT_ROWS:g * GQA_VT_ROWS + HEAD64, :] = vgt[g * HEAD64:(g + 1) * HEAD64]
        vtg_ref[0, g * GQA_VT_ROWS + HEAD64:(g + 1) * GQA_VT_ROWS, :] = ones_rows


def _prep_call(h, tail, cos, sin, ng, wa, wuq, wuk, wuv, gt, gcq, gckv):
    B, _, D = h.shape
    Lp = cos.shape[0]
    tm = _pick(Lp, (384, 128))
    row = lambda b, j: (b, j, 0)
    colt = lambda b, j: (b, 0, j)
    tab = pl.BlockSpec((tm, LANES), lambda b, j: (j, 0))
    sds = lambda *s: jax.ShapeDtypeStruct(s, BF16)
    vt_rows = 4 * VT_ROWS
    out_shape = [sds(B, 512, Lp), sds(B, 4, Lp, LANES), sds(B, vt_rows, Lp),
                 sds(B, 1024, Lp), sds(B, MLA_HEADS, Lp, 2 * LANES), sds(B, vt_rows, Lp),
                 sds(B, 512, Lp), sds(B, Lp, LANES), sds(B, 2 * GQA_VT_ROWS, Lp)]
    hrow = lambda b, j: (b, 0, j, 0)
    out_specs = [pl.BlockSpec((1, 512, tm), colt), pl.BlockSpec((1, 4, tm, LANES), hrow), pl.BlockSpec((1, vt_rows, tm), colt),
                 pl.BlockSpec((1, 1024, tm), colt), pl.BlockSpec((1, MLA_HEADS, tm, 2 * LANES), hrow),
                 pl.BlockSpec((1, vt_rows, tm), colt),
                 pl.BlockSpec((1, 512, tm), colt), pl.BlockSpec((1, tm, LANES), row),
                 pl.BlockSpec((1, 2 * GQA_VT_ROWS, tm), colt)]
    stream = [h] if tail is None else [h, tail]
    in_specs = [_stream_spec(h, tm)] + ([] if tail is None else [_resident(tail.shape)]) + [
        tab, tab, _resident(ng.shape), _resident(wa.shape), _resident(wuq.shape), _resident(wuk.shape),
        _resident(wuv.shape), _resident(gt.shape), _resident(gcq.shape), _resident(gckv.shape)]
    return pl.pallas_call(
        functools.partial(_prep_kernel, has_tail=tail is not None), out_shape=out_shape, grid=(B, Lp // tm),
        in_specs=in_specs, out_specs=out_specs,
        compiler_params=_cparams(("parallel", "parallel")), name="prep",
    )(*stream, cos, sin, ng, wa, wuq, wuk, wuv, gt, gcq, gckv)


def _dense_attn_kernel(*refs, n_real, tk, differential, lam_init):
    if differential:
        qt_ref, k_ref, vt_ref, lam_ref, sg_ref, o_ref, s_ref = refs
    else:
        qt_ref, k_ref, vt_ref, o_ref, s_ref = refs
    tq = qt_ref.shape[2]
    dv = o_ref.shape[2]
    qt = qt_ref[0]
    if differential:
        z = jnp.zeros((HEAD64, tq), BF16)
        wq = jnp.concatenate([jnp.concatenate([qt[:HEAD64], z], axis=1),
                              jnp.concatenate([z, qt[HEAD64:]], axis=1)], axis=0)
    else:
        wq = qt
    width = wq.shape[1]

    def scores(kc):
        return jnp.dot(kc, wq, preferred_element_type=F32)

    def absorb(m, acc, s, vtc):
        m_new = jnp.maximum(m, jnp.max(s, axis=0, keepdims=True))
        p = jnp.exp2(s - m_new).astype(BF16)
        alpha = jnp.exp2(m - m_new)
        return m_new, alpha * acc + jnp.dot(vtc, p, preferred_element_type=F32)

    def kchunk(c):
        return k_ref[0, 0, pl.ds(pl.multiple_of(c * tk, tk), tk), :]

    def vchunk(c):
        return vt_ref[0, :, pl.ds(pl.multiple_of(c * tk, tk), tk)]

    def group(j, carry):
        m, acc = carry
        for g in range(CHUNK_GROUP):
            c = CHUNK_GROUP * j + g
            s_ref[(g + 1) % 2] = scores(kchunk(c + 1))
            m, acc = absorb(m, acc, s_ref[g % 2], vchunk(c))
        return m, acc

    n_chunks = n_real // tk
    n_groups = (n_chunks - 1) // CHUNK_GROUP
    s_ref[0] = scores(kchunk(0))
    carry = (jnp.full((1, width), NEG, F32), jnp.zeros((dv + ONES_ROWS, width), F32))
    m, acc = lax.fori_loop(0, n_groups, group, carry)
    valid = lax.broadcasted_iota(jnp.int32, (TAIL, width), 0) < N_META
    first = CHUNK_GROUP * n_groups
    for c in range(first, n_chunks):
        if c + 1 < n_chunks:
            s_ref[(c - first + 1) % 2] = scores(kchunk(c + 1))
        else:
            s_tail = jnp.where(valid, scores(k_ref[0, 0, n_real:n_real + TAIL, :]), NEG)
        m, acc = absorb(m, acc, s_ref[(c - first) % 2], vchunk(c))
    _, acc = absorb(m, acc, s_tail, vt_ref[0, :, n_real:n_real + TAIL])

    ot = acc[:dv] * (1.0 / acc[dv:dv + 1])
    if differential:
        lt = lam_ref[...]
        lam = (jnp.exp(jnp.sum(lt[0:1, :] * lt[1:2, :], axis=-1, keepdims=True))
               - jnp.exp(jnp.sum(lt[2:3, :] * lt[3:4, :], axis=-1, keepdims=True)) + lam_init)
        o = (ot[:, :tq] - lam * ot[:, tq:]).T
        o = ((o * _rms_scale(o, dv)) * sg_ref[...]) * (1.0 - lam_init)
    else:
        o = ot.T
    o_ref[0] = o.astype(o_ref.dtype)


def _dense_attn_call(qt, k, vt, n_real, *, differential, lam_tab=None, sub_g=None, lam_init=0.0, name):
    B, heads, Lp, dq = k.shape
    if differential:
        tq = _pick(Lp, (384, 128))
    else:
        tq = LATENT_TQ if Lp >= LATENT_TQ else _pick(n_real, (512, 256, 128))
    tk = _pick(n_real, (512, 256, 128))
    in_specs = [pl.BlockSpec((1, dq, tq), lambda b, h, i: (b, h, i)),
                pl.BlockSpec((1, 1, Lp, dq), lambda b, h, i: (b, h, 0, 0)),
                pl.BlockSpec((1, VT_ROWS, Lp), lambda b, h, i: (b, h, 0))]
    args = [qt, k, vt]
    if differential:
        in_specs += [_resident(lam_tab.shape), _resident(sub_g.shape)]
        args += [lam_tab, sub_g]
    width = (2 if differential else 1) * tq
    kern = functools.partial(_dense_attn_kernel, n_real=n_real, tk=tk, differential=differential, lam_init=lam_init)
    return pl.pallas_call(
        kern, out_shape=jax.ShapeDtypeStruct((B, Lp, heads * LANES), BF16), grid=(B, heads, pl.cdiv(Lp, tq)),
        in_specs=in_specs, out_specs=pl.BlockSpec((1, tq, LANES), lambda b, h, i: (b, i, h)),
        scratch_shapes=[pltpu.VMEM((2, tk, width), F32)],
        compiler_params=_cparams(("parallel", "parallel", "parallel")), name=name,
    )(*args)


def _window_attn_kernel(sink_ref, qt_ref, k_ref, vt_ref, bias_ref, o_ref, *, n_real):
    tq = qt_ref.shape[2]
    t = pl.program_id(1)
    is_meta = t == n_real // tq
    span = tq + 2 * BLOCK
    start = pl.multiple_of(jnp.where(is_meta, 0, jnp.maximum(t * tq - BLOCK, 0)), BLOCK)
    kwin = jnp.concatenate([k_ref[0, pl.ds(start, span), :], k_ref[0, n_real:n_real + TAIL, :]], axis=0)
    vwin = jnp.concatenate([vt_ref[0, :, pl.ds(start, span)], vt_ref[0, :, n_real:n_real + TAIL]], axis=1)
    n_tiles = n_real // tq
    kind = jnp.where(is_meta, 4, (t == 0).astype(jnp.int32) + 2 * (t == n_tiles - 1).astype(jnp.int32))
    bias = jnp.concatenate([bias_ref[kind]] * GQA_GROUP, axis=1)
    qt = qt_ref[0]
    n_groups = GQA_HEADS // GQA_GROUP
    gw = GQA_GROUP * tq
    z = jnp.zeros((HEAD64, gw), BF16)
    rows = []
    for g in range(n_groups):
        qrow = jnp.concatenate([qt[h * HEAD64:(h + 1) * HEAD64] for h in range(g * GQA_GROUP, (g + 1) * GQA_GROUP)], axis=1)
        rows.append(jnp.concatenate([qrow if i == g else z for i in range(n_groups)], axis=1))
    wq = jnp.concatenate(rows, axis=0)
    sink = jnp.concatenate([jnp.full((1, tq), sink_ref[h] * LOG2E, F32) for h in range(GQA_HEADS)], axis=1)
    s = jnp.dot(kwin, wq, preferred_element_type=F32) + jnp.concatenate([bias, bias], axis=1)
    m = jnp.maximum(jnp.max(s, axis=0, keepdims=True), sink)
    p = jnp.exp2(s - m).astype(BF16)
    extra = jnp.exp2(sink - m)
    for g in range(n_groups):
        acc = jnp.dot(vwin[g * GQA_VT_ROWS:(g + 1) * GQA_VT_ROWS], p[:, g * gw:(g + 1) * gw], preferred_element_type=F32)
        ot = acc[:HEAD64] * (1.0 / (acc[HEAD64:HEAD64 + 1] + extra[:, g * gw:(g + 1) * gw]))
        for i in range(GQA_GROUP // 2):
            pair = jnp.concatenate([ot[:, 2 * i * tq:(2 * i + 1) * tq], ot[:, (2 * i + 1) * tq:(2 * i + 2) * tq]], axis=0)
            c = g * (GQA_GROUP // 2) + i
            o_ref[0, :, c * LANES:(c + 1) * LANES] = pair.T.astype(o_ref.dtype)


def _window_bias(n_real, tq):
    span = tq + 2 * BLOCK
    ks = jnp.arange(span + TAIL, dtype=jnp.int32)[:, None]
    qi = jnp.arange(tq, dtype=jnp.int32)[None, :]
    in_span = ks < span
    tail_ok = (ks >= span) & (ks < span + N_META)
    n_tiles = n_real // tq
    tables = []
    for first, last in ((False, False), (True, False), (False, True), (True, True)):
        t = 0 if first else (n_tiles - 1 if last else 1)
        kpos = max(t * tq - BLOCK, 0) + ks
        dist = kpos - (t * tq + qi)
        tables.append((dist <= WINDOW) & (dist >= -WINDOW) & in_span & (kpos < n_real) | tail_ok)
    tables.append(in_span & (ks <= qi + (WINDOW - N_META)) | tail_ok)
    return jnp.where(jnp.stack(tables), 0.0, NEG).astype(F32)


def _window_attn_call(qt, k, vt, sink, n_real):
    B, Lp, _ = k.shape
    tq = 256 if n_real % 256 == 0 and n_real >= 384 else BLOCK
    assert n_real % tq == 0 and tq + 2 * BLOCK <= Lp
    bias = _window_bias(n_real, tq)
    grid_spec = pltpu.PrefetchScalarGridSpec(
        num_scalar_prefetch=1, grid=(B, n_real // tq + 1),
        in_specs=[pl.BlockSpec((1, GQA_HEADS * HEAD64, tq), lambda b, t, s: (b, 0, t)),
                  pl.BlockSpec((1, Lp, LANES), lambda b, t, s: (b, 0, 0)),
                  pl.BlockSpec((1, 2 * GQA_VT_ROWS, Lp), lambda b, t, s: (b, 0, 0)),
                  _resident(bias.shape)],
        out_specs=pl.BlockSpec((1, tq, 512), lambda b, t, s: (b, t, 0)))
    return pl.pallas_call(
        functools.partial(_window_attn_kernel, n_real=n_real),
        out_shape=jax.ShapeDtypeStruct((B, Lp, 512), BF16), grid_spec=grid_spec,
        compiler_params=_cparams(("parallel", "parallel")), name="window_attn",
    )(sink, qt, k, vt, bias)


def _post_kernel(*refs, has_tail):
    h_ref, tail_ref = (refs[0], refs[1]) if has_tail else (refs[0], None)
    oda_ref, om_ref, og_ref, ng_ref, wg_ref, wb_ref, wo_ref, mg_ref, wu_ref, wd_ref, out_ref = refs[2 if has_tail else 1:]
    x = _row_tile(h_ref, tail_ref)
    xn = ((x * _rms_scale(x, D_MODEL)) * ng_ref[...]).astype(BF16)
    gl = jnp.dot(xn, wg_ref[...], preferred_element_type=F32)
    merged = None
    for g, o_ref in enumerate((oda_ref, om_ref, og_ref)):
        proj = jnp.dot(o_ref[0], wb_ref[g], preferred_element_type=F32)
        term = jax.nn.sigmoid(gl[:, g * D_MODEL:(g + 1) * D_MODEL]) * proj
        merged = term if merged is None else merged + term
    h1 = x + jnp.dot(merged.astype(BF16), wo_ref[...], preferred_element_type=F32)
    h1n = ((h1 * _rms_scale(h1, D_MODEL)) * mg_ref[...]).astype(BF16)
    u = jnp.dot(h1n, wu_ref[...], preferred_element_type=F32)
    a = jnp.square(jnp.maximum(u, 0.0)).astype(BF16)
    out_ref[0] = h1 + jnp.dot(a, wd_ref[...], preferred_element_type=F32)


def _post_call(h, tail, oda, om, og, ng, wg, wb, wo, mg, wu, wd, out_rows):
    B, Lp, _ = oda.shape
    D = h.shape[2]
    tm = _pick(Lp, (POST_TM, 128))
    row = lambda b, j: (b, j, 0)
    stream = [h] if tail is None else [h, tail]
    in_specs = [_stream_spec(h, tm)] + ([] if tail is None else [_resident(tail.shape)]) + [
        pl.BlockSpec((1, tm, BRANCH_WIDTH), row)] * 3 + [_resident(a.shape) for a in (ng, wg, wb, wo, mg, wu, wd)]
    return pl.pallas_call(
        functools.partial(_post_kernel, has_tail=tail is not None),
        out_shape=jax.ShapeDtypeStruct((B, out_rows, D), F32), grid=(B, pl.cdiv(out_rows, tm)),
        in_specs=in_specs, out_specs=pl.BlockSpec((1, tm, D), row),
        compiler_params=_cparams(("parallel", "parallel")), name="post",
    )(*stream, oda, om, og, ng, wg, wb, wo, mg, wu, wd)


def _swap_halves(m):
    return jnp.flip(m.reshape(m.shape[:-1] + (m.shape[-1] // HEAD64, 2, HEAD64 // 2)), axis=-2).reshape(m.shape)


def _pair_gain(g):
    g = jnp.tile(g.astype(F32), 2)[None, :]
    return jnp.concatenate([g, _swap_halves(g)], axis=0)


def _pad_gain(g):
    return jnp.concatenate([g.astype(F32), jnp.zeros((LANES - g.shape[0],), F32)])[None, :]


def _layer_params(l, attn_norm_g, w_in, da_q_norm_g, da_k_norm_g, da_lam_q1, da_lam_k1, da_lam_q2, da_lam_k2,
                  da_subln_g, mla_cq_norm_g, mla_ckv_norm_g, mla_w_uq, mla_w_ukv, mla_q_norm_g, mla_k_norm_g,
                  gqa_q_norm_g, gqa_k_norm_g, gqa_sink, w_branch, w_out, mlp_norm_g, w_up, w_down):
    w = w_in[l]
    sec = [w[:, a:b] for a, b in zip(IN_OFF[:-1], IN_OFF[1:])]
    da_q, da_k, da_v, cq, ckv, kr, gq, gk, gv, gate = sec
    pad64 = lambda m: jnp.pad(m, ((0, 0),) * (m.ndim - 1) + ((0, LANES - HEAD64),))
    wa = jnp.concatenate([da_q, da_k, da_v, cq, ckv,
                          pad64(kr), pad64(_swap_halves(kr)), gq, _swap_halves(gq),
                          gk, _swap_halves(gk), gv], axis=1).astype(BF16)
    wuq = mla_w_uq[l].reshape(MLA_Q_RANK, MLA_HEADS, MLA_QK)
    wuq_r = wuq[:, :, MLA_NOPE:]
    wuq = jnp.concatenate([wuq[:, :, :MLA_NOPE], pad64(wuq_r), pad64(_swap_halves(wuq_r))], axis=-1)
    wuq = wuq.reshape(MLA_Q_RANK, MLA_HEADS * 3 * LANES).astype(BF16)
    wukv = mla_w_ukv[l].reshape(MLA_KV_RANK, MLA_HEADS, 2 * LANES)
    wuk = wukv[:, :, :MLA_NOPE].reshape(MLA_KV_RANK, MLA_HEADS * LANES).astype(BF16)
    wuv = wukv[:, :, MLA_NOPE:].reshape(MLA_KV_RANK, MLA_HEADS * LANES).astype(BF16)
    mq, mk = mla_q_norm_g[l], mla_k_norm_g[l]
    gt = jnp.concatenate([_pair_gain(da_q_norm_g[l]), _pair_gain(da_k_norm_g[l]),
                          _pair_gain(gqa_q_norm_g[l]), _pair_gain(gqa_k_norm_g[l]),
                          mq[None, :MLA_NOPE].astype(F32), _pad_gain(mq[MLA_NOPE:]), _pad_gain(_swap_halves(mq[MLA_NOPE:])),
                          mk[None, :MLA_NOPE].astype(F32), _pad_gain(mk[MLA_NOPE:]), _pad_gain(_swap_halves(mk[MLA_NOPE:])),
                          jnp.zeros((2, LANES), F32)], axis=0)
    lam_tab = jnp.concatenate([_pad_gain(v[l]) for v in (da_lam_q1, da_lam_k1, da_lam_q2, da_lam_k2)], axis=0)
    return dict(
        ng=attn_norm_g[l][None, :].astype(F32), wa=wa, wuq=wuq, wuk=wuk, wuv=wuv, gt=gt,
        gcq=mla_cq_norm_g[l][None, :].astype(F32), gckv=mla_ckv_norm_g[l][None, :].astype(F32),
        lam_tab=lam_tab, sub_g=da_subln_g[l][None, :].astype(F32), sink=gqa_sink[l].astype(F32),
        wg=gate.astype(BF16), wb=w_branch[l].astype(BF16), wo=w_out[l].astype(BF16),
        mg=mlp_norm_g[l][None, :].astype(F32), wu=w_up[l].astype(BF16), wd=w_down[l].astype(BF16))


def _rope_tables(n_real):
    r = jnp.arange(n_real + TAIL, dtype=jnp.int32)
    pos = jnp.where(r < n_real, r + N_META, jnp.where(r < n_real + N_META, r - n_real, 0)).astype(F32)
    inv_freq = 1.0 / (ROPE_THETA ** (jnp.arange(0, HEAD64, 2, dtype=F32) / HEAD64))
    ang = pos[:, None] * inv_freq[None, :]
    cos, sin = jnp.cos(ang), jnp.sin(ang)
    return jnp.tile(cos, (1, 4)), jnp.tile(jnp.concatenate([-sin, sin], axis=1), (1, 2))


def _trunk(x, meta_tokens, layers):
    B, S, D = x.shape
    assert D == D_MODEL and S % BLOCK == 0
    tail = jnp.concatenate([meta_tokens.astype(x.dtype), jnp.zeros((TAIL - N_META, D), x.dtype)], axis=0)
    h = x
    cos, sin = _rope_tables(S)
    for l, p in enumerate(layers):
        lam_init = 0.8 - 0.6 * math.exp(-0.3 * l)
        qtda, kda, vtda, qtm, km, vtm, qtg, kg, vtg = _prep_call(
            h, tail, cos, sin, p["ng"], p["wa"], p["wuq"], p["wuk"], p["wuv"], p["gt"], p["gcq"], p["gckv"])
        oda = _dense_attn_call(qtda, kda, vtda, S, differential=True, lam_tab=p["lam_tab"], sub_g=p["sub_g"],
                               lam_init=lam_init, name="diff_attn")
        om = _dense_attn_call(qtm, km, vtm, S, differential=False, name="latent_attn")
        og = _window_attn_call(qtg, kg, vtg, p["sink"], S)
        h = _post_call(h, tail, oda, om, og, p["ng"], p["wg"], p["wb"], p["wo"], p["mg"], p["wu"], p["wd"],
                       out_rows=S if l == len(layers) - 1 else S + TAIL)
        tail = None
    return h


def kernel(x_prompt, x_sample, meta_tokens, attn_norm_g, w_in, da_q_norm_g, da_k_norm_g, da_lam_q1, da_lam_k1, da_lam_q2, da_lam_k2, da_subln_g, mla_cq_norm_g, mla_ckv_norm_g, mla_w_uq, mla_w_ukv, mla_q_norm_g, mla_k_norm_g, gqa_q_norm_g, gqa_k_norm_g, gqa_sink, w_branch, w_out, mlp_norm_g, w_up, w_down):
    depth = w_in.shape[0]
    layers = [_layer_params(l, attn_norm_g, w_in, da_q_norm_g, da_k_norm_g, da_lam_q1, da_lam_k1, da_lam_q2,
                            da_lam_k2, da_subln_g, mla_cq_norm_g, mla_ckv_norm_g, mla_w_uq, mla_w_ukv,
                            mla_q_norm_g, mla_k_norm_g, gqa_q_norm_g, gqa_k_norm_g, gqa_sink, w_branch, w_out,
                            mlp_norm_g, w_up, w_down) for l in range(depth)]
    return (_trunk(x_prompt, meta_tokens, layers), _trunk(x_sample, meta_tokens, layers))
```

```python
import functools
import math

import jax
import jax.numpy as jnp
from jax import lax
from jax.experimental import pallas as pl
from jax.experimental.pallas import tpu as pltpu

F32 = jnp.float32
BF16 = jnp.bfloat16

D_MODEL = 1024
N_META = 16
BLOCK = 128
WINDOW = 128
ROPE_THETA = 10000.0
EPS = 1e-6
NEG = -1e30
LOG2E = 1.4426950408889634

HEAD64 = 64
MLA_HEADS = 4
MLA_NOPE = 128
MLA_QK = 192
MLA_Q_RANK = 384
MLA_KV_RANK = 256
GQA_HEADS = 8
GQA_GROUP = 4
BRANCH_WIDTH = 512
LANES = 128
TAIL = 128
ONES_ROWS = 16
VT_ROWS = LANES + ONES_ROWS
GQA_VT_ROWS = HEAD64 + ONES_ROWS
CHUNK_GROUP = 8
POST_TM = 384
LATENT_TQ = 768

A_DAQ, A_DAK, A_DAV, A_CQ, A_CKV, A_KR, A_KRS, A_GQ, A_GQS, A_GK, A_GKS, A_GV, A_END = (
    0, 512, 1024, 1536, 1920, 2176, 2304, 2432, 2944, 3456, 3584, 3712, 3840)
IN_OFF = (0, 512, 1024, 1536, 1920, 2176, 2240, 2752, 2880, 3008, 6080)

VMEM_LIMIT = 56 * 1024 * 1024
ATTN_VMEM_BUDGET = 44 * 1024 * 1024


def _pick(n, candidates):
    for c in candidates:
        if n % c == 0:
            return c
    raise ValueError(f"no tile in {candidates} divides {n}")


def _cparams(sem):
    return pltpu.CompilerParams(dimension_semantics=sem, vmem_limit_bytes=VMEM_LIMIT)


def _resident(shape):
    zeros = (0,) * len(shape)
    return pl.BlockSpec(shape, lambda *_: zeros, pipeline_mode=pl.Buffered(1))


def _rms_scale(x, width):
    return lax.rsqrt(jnp.sum(x * x, axis=-1, keepdims=True) * (1.0 / width) + EPS)


def _pair_rms_scale(xc, lo_mask):
    sq = xc * xc
    lo = jnp.sum(jnp.where(lo_mask, sq, 0.0), axis=-1, keepdims=True)
    hi = jnp.sum(jnp.where(lo_mask, 0.0, sq), axis=-1, keepdims=True)
    return jnp.where(lo_mask, lax.rsqrt(lo * (1.0 / HEAD64) + EPS), lax.rsqrt(hi * (1.0 / HEAD64) + EPS))


def _rope(xc, swapped, cos, sin_signed):
    return xc * cos + swapped * sin_signed


def _stream_spec(h, tm):
    last = (h.shape[1] - 1) // tm
    return pl.BlockSpec((1, tm, h.shape[2]), lambda b, j: (b, jnp.minimum(j, last), 0))


def _row_tile(h_ref, tail_ref):
    x = h_ref[0]
    if tail_ref is None:
        return x
    tm = x.shape[0]
    tail = jnp.concatenate([jnp.zeros((tm - TAIL, x.shape[1]), x.dtype), tail_ref[...]], axis=0)
    in_tail = lax.broadcasted_iota(jnp.int32, x.shape, 0) >= tm - TAIL
    is_last = pl.program_id(1) == pl.num_programs(1) - 1
    return jnp.where(jnp.logical_and(is_last, in_tail), tail, x)


def _prep_kernel(*refs, has_tail):
    h_ref, tail_ref = (refs[0], refs[1]) if has_tail else (refs[0], None)
    (cos_ref, sin_ref, ng_ref, wa_ref, wuq_ref, wuk_ref, wuv_ref, gt_ref, gcq_ref, gckv_ref,
     qtda_ref, kda_ref, vtda_ref, qtm_ref, km_ref, vtm_ref, qtg_ref, kg_ref, vtg_ref) = refs[2 if has_tail else 1:]
    x = _row_tile(h_ref, tail_ref)
    xn = ((x * _rms_scale(x, D_MODEL)) * ng_ref[...]).astype(BF16)
    xp = jnp.dot(xn, wa_ref[...], preferred_element_type=F32)
    cos = cos_ref[...]
    sin_s = sin_ref[...]
    lane = lax.broadcasted_iota(jnp.int32, cos.shape, 1)
    lo64 = lane < HEAD64
    s64 = (HEAD64 ** -0.5) * LOG2E
    s192 = (MLA_QK ** -0.5) * LOG2E

    first_half = (lane & 32) == 0

    def head_pair(col, col_swapped, gains):
        xc = xp[:, col:col + LANES]
        r = _pair_rms_scale(xc, lo64)
        y = (xc * r) * gains[0:1, :]
        if col_swapped is None:
            ys = jnp.where(first_half, pltpu.roll(y, LANES - 32, 1), pltpu.roll(y, 32, 1))
        else:
            ys = (xp[:, col_swapped:col_swapped + LANES] * r) * gains[1:2, :]
        return _rope(y, ys, cos, sin_s)

    ones_rows = jnp.ones((ONES_ROWS, x.shape[0]), BF16)

    def store_vt(vt_ref, hd, v_slab):
        vt_ref[0, hd * VT_ROWS:hd * VT_ROWS + LANES, :] = v_slab.T.astype(BF16)
        vt_ref[0, hd * VT_ROWS + LANES:(hd + 1) * VT_ROWS, :] = ones_rows

    g_q, g_k = gt_ref[0:2, :], gt_ref[2:4, :]
    for c in range(4):
        sl = slice(c * LANES, (c + 1) * LANES)
        qtda_ref[0, sl, :] = (head_pair(A_DAQ + c * LANES, None, g_q) * s64).T.astype(BF16)
        kda_ref[0, c] = head_pair(A_DAK + c * LANES, None, g_k).astype(BF16)
        store_vt(vtda_ref, c, xp[:, A_DAV + c * LANES:A_DAV + (c + 1) * LANES])

    cq = xp[:, A_CQ:A_CQ + MLA_Q_RANK]
    cqn = ((cq * _rms_scale(cq, MLA_Q_RANK)) * gcq_ref[...]).astype(BF16)
    qm = jnp.dot(cqn, wuq_ref[...], preferred_element_type=F32)
    ckv = xp[:, A_CKV:A_CKV + MLA_KV_RANK]
    ckvn = ((ckv * _rms_scale(ckv, MLA_KV_RANK)) * gckv_ref[...]).astype(BF16)
    kn_all = jnp.dot(ckvn, wuk_ref[...], preferred_element_type=F32)
    vm_all = jnp.dot(ckvn, wuv_ref[...], preferred_element_type=F32)
    kr = xp[:, A_KR:A_KR + LANES]
    krs = xp[:, A_KRS:A_KRS + LANES]
    kr_ss = jnp.sum(kr * kr, axis=-1, keepdims=True)
    gq_n, gq_r, gq_rs = gt_ref[8:9, :], gt_ref[9:10, :], gt_ref[10:11, :]
    gk_n, gk_r, gk_rs = gt_ref[11:12, :], gt_ref[12:13, :], gt_ref[13:14, :]
    for hd in range(MLA_HEADS):
        b = hd * 2 * LANES
        qb = hd * 3 * LANES
        qn = qm[:, qb:qb + LANES]
        qr = qm[:, qb + LANES:qb + 2 * LANES]
        qrs = qm[:, qb + 2 * LANES:qb + 3 * LANES]
        r = lax.rsqrt((jnp.sum(qn * qn, axis=-1, keepdims=True) + jnp.sum(qr * qr, axis=-1, keepdims=True))
                      * (1.0 / MLA_QK) + EPS)
        qtm_ref[0, b:b + LANES, :] = (((qn * r) * gq_n) * s192).T.astype(BF16)
        qtm_ref[0, b + LANES:b + 2 * LANES, :] = (
            _rope((qr * r) * gq_r, (qrs * r) * gq_rs, cos, sin_s) * s192).T.astype(BF16)
        kn = kn_all[:, hd * LANES:(hd + 1) * LANES]
        rk = lax.rsqrt((jnp.sum(kn * kn, axis=-1, keepdims=True) + kr_ss) * (1.0 / MLA_QK) + EPS)
        km_ref[0, hd, :, :LANES] = ((kn * rk) * gk_n).astype(BF16)
        km_ref[0, hd, :, LANES:] = _rope((kr * rk) * gk_r, (krs * rk) * gk_rs, cos, sin_s).astype(BF16)
        store_vt(vtm_ref, hd, vm_all[:, hd * LANES:(hd + 1) * LANES])

    g_q, g_k = gt_ref[4:6, :], gt_ref[6:8, :]
    for c in range(4):
        sl = slice(c * LANES, (c + 1) * LANES)
        qtg_ref[0, sl, :] = (head_pair(A_GQ + c * LANES, A_GQS + c * LANES, g_q) * s64).T.astype(BF16)
    kg_ref[0] = head_pair(A_GK, A_GKS, g_k).astype(BF16)
    vgt = xp[:, A_GV:A_GV + LANES].T.astype(BF16)
    for g in range(GQA_HEADS // GQA_GROUP):
        vtg_ref[0, g * GQA_VT_ROWS:g * GQA_VT_ROWS + HEAD64, :] = vgt[g * HEAD64:(g + 1) * HEAD64]
        vtg_ref[0, g * GQA_VT_ROWS + HEAD64:(g + 1) * GQA_VT_ROWS, :] = ones_rows


def _prep_call(h, tail, cos, sin, ng, wa, wuq, wuk, wuv, gt, gcq, gckv):
    B, _, D = h.shape
    Lp = cos.shape[0]
    tm = _pick(Lp, (384, 128))
    row = lambda b, j: (b, j, 0)
    colt = lambda b, j: (b, 0, j)
    tab = pl.BlockSpec((tm, LANES), lambda b, j: (j, 0))
    sds = lambda *s: jax.ShapeDtypeStruct(s, BF16)
    vt_rows = 4 * VT_ROWS
    out_shape = [sds(B, 512, Lp), sds(B, 4, Lp, LANES), sds(B, vt_rows, Lp),
                 sds(B, 1024, Lp), sds(B, MLA_HEADS, Lp, 2 * LANES), sds(B, vt_rows, Lp),
                 sds(B, 512, Lp), sds(B, Lp, LANES), sds(B, 2 * GQA_VT_ROWS, Lp)]
    hrow = lambda b, j: (b, 0, j, 0)
    out_specs = [pl.BlockSpec((1, 512, tm), colt), pl.BlockSpec((1, 4, tm, LANES), hrow), pl.BlockSpec((1, vt_rows, tm), colt),
                 pl.BlockSpec((1, 1024, tm), colt), pl.BlockSpec((1, MLA_HEADS, tm, 2 * LANES), hrow),
                 pl.BlockSpec((1, vt_rows, tm), colt),
                 pl.BlockSpec((1, 512, tm), colt), pl.BlockSpec((1, tm, LANES), row),
                 pl.BlockSpec((1, 2 * GQA_VT_ROWS, tm), colt)]
    stream = [h] if tail is None else [h, tail]
    in_specs = [_stream_spec(h, tm)] + ([] if tail is None else [_resident(tail.shape)]) + [
        tab, tab, _resident(ng.shape), _resident(wa.shape), _resident(wuq.shape), _resident(wuk.shape),
        _resident(wuv.shape), _resident(gt.shape), _resident(gcq.shape), _resident(gckv.shape)]
    return pl.pallas_call(
        functools.partial(_prep_kernel, has_tail=tail is not None), out_shape=out_shape, grid=(B, Lp // tm),
        in_specs=in_specs, out_specs=out_specs,
        compiler_params=_cparams(("parallel", "parallel")), name="prep",
    )(*stream, cos, sin, ng, wa, wuq, wuk, wuv, gt, gcq, gckv)


def _dense_attn_kernel(*refs, n_real, tk, differential, lam_init, heads_per_step):
    if differential:
        qt_ref, k_ref, vt_ref, lam_ref, sg_ref, o_ref = refs[:6]
    else:
        (qt_ref, k_ref, vt_ref, o_ref), lam_ref, sg_ref = refs[:4], None, None
    s_refs = refs[len(refs) - heads_per_step:]
    dq = qt_ref.shape[1] // heads_per_step
    for hh in range(heads_per_step):
        _attend_head(qt_ref.at[0, hh * dq:(hh + 1) * dq], k_ref.at[0, hh], vt_ref.at[0, hh * VT_ROWS:(hh + 1) * VT_ROWS],
                     lam_ref, sg_ref, o_ref.at[0, :, hh * LANES:(hh + 1) * LANES], s_refs[hh],
                     n_real=n_real, tk=tk, differential=differential, lam_init=lam_init)


def _attend_head(qt_ref, k_ref, vt_ref, lam_ref, sg_ref, o_ref, s_ref, *, n_real, tk, differential, lam_init):
    tq = qt_ref.shape[1]
    dv = o_ref.shape[1]
    qt = qt_ref[...]
    if differential:
        z = jnp.zeros((HEAD64, tq), BF16)
        wq = jnp.concatenate([jnp.concatenate([qt[:HEAD64], z], axis=1),
                              jnp.concatenate([z, qt[HEAD64:]], axis=1)], axis=0)
    else:
        wq = qt
    width = wq.shape[1]

    def scores(kc):
        return jnp.dot(kc, wq, preferred_element_type=F32)

    def absorb(m, acc, s, vtc):
        m_new = jnp.maximum(m, jnp.max(s, axis=0, keepdims=True))
        p = jnp.exp2(s - m_new).astype(BF16)
        alpha = jnp.exp2(m - m_new)
        return m_new, alpha * acc + jnp.dot(vtc, p, preferred_element_type=F32)

    def kchunk(c):
        return k_ref[pl.ds(pl.multiple_of(c * tk, tk), tk), :]

    def vchunk(c):
        return vt_ref[:, pl.ds(pl.multiple_of(c * tk, tk), tk)]

    def group(j, carry):
        m, acc = carry
        for g in range(CHUNK_GROUP):
            c = CHUNK_GROUP * j + g
            s_ref[(g + 1) % 2] = scores(kchunk(c + 1))
            m, acc = absorb(m, acc, s_ref[g % 2], vchunk(c))
        return m, acc

    n_chunks = n_real // tk
    n_groups = (n_chunks - 1) // CHUNK_GROUP
    s_ref[0] = scores(kchunk(0))
    carry = (jnp.full((1, width), NEG, F32), jnp.zeros((dv + ONES_ROWS, width), F32))
    m, acc = lax.fori_loop(0, n_groups, group, carry)
    valid = lax.broadcasted_iota(jnp.int32, (TAIL, width), 0) < N_META
    first = CHUNK_GROUP * n_groups
    for c in range(first, n_chunks):
        if c + 1 < n_chunks:
            s_ref[(c - first + 1) % 2] = scores(kchunk(c + 1))
        else:
            s_tail = jnp.where(valid, scores(k_ref[n_real:n_real + TAIL, :]), NEG)
        m, acc = absorb(m, acc, s_ref[(c - first) % 2], vchunk(c))
    _, acc = absorb(m, acc, s_tail, vt_ref[:, n_real:n_real + TAIL])

    ot = acc[:dv] * (1.0 / acc[dv:dv + 1])
    if differential:
        lt = lam_ref[...]
        lam = (jnp.exp(jnp.sum(lt[0:1, :] * lt[1:2, :], axis=-1, keepdims=True))
               - jnp.exp(jnp.sum(lt[2:3, :] * lt[3:4, :], axis=-1, keepdims=True)) + lam_init)
        o = (ot[:, :tq] - lam * ot[:, tq:]).T
        o = ((o * _rms_scale(o, dv)) * sg_ref[...]) * (1.0 - lam_init)
    else:
        o = ot.T
    o_ref[...] = o.astype(o_ref.dtype)


def _dense_attn_call(qt, k, vt, n_real, *, differential, lam_tab=None, sub_g=None, lam_init=0.0, name):
    B, heads, Lp, dq = k.shape
    if differential:
        tq = _pick(Lp, (384, 128))
    else:
        tq = LATENT_TQ if Lp >= LATENT_TQ else _pick(n_real, (512, 256, 128))
    tk = _pick(n_real, (512, 256, 128))
    width = (2 if differential else 1) * tq
    per_head = 2 * 2 * Lp * (dq + VT_ROWS) + 4 * 2 * tk * width
    hps = 2 if heads % 2 == 0 and 2 * per_head <= ATTN_VMEM_BUDGET else 1
    in_specs = [pl.BlockSpec((1, hps * dq, tq), lambda b, h, i: (b, h, i)),
                pl.BlockSpec((1, hps, Lp, dq), lambda b, h, i: (b, h, 0, 0)),
                pl.BlockSpec((1, hps * VT_ROWS, Lp), lambda b, h, i: (b, h, 0))]
    args = [qt, k, vt]
    if differential:
        in_specs += [_resident(lam_tab.shape), _resident(sub_g.shape)]
        args += [lam_tab, sub_g]
    kern = functools.partial(_dense_attn_kernel, n_real=n_real, tk=tk, differential=differential, lam_init=lam_init,
                             heads_per_step=hps)
    return pl.pallas_call(
        kern, out_shape=jax.ShapeDtypeStruct((B, Lp, heads * LANES), BF16), grid=(B, heads // hps, pl.cdiv(Lp, tq)),
        in_specs=in_specs, out_specs=pl.BlockSpec((1, tq, hps * LANES), lambda b, h, i: (b, i, h)),
        scratch_shapes=[pltpu.VMEM((2, tk, width), F32)] * hps,
        compiler_params=_cparams(("parallel", "parallel", "parallel")), name=name,
    )(*args)


def _window_attn_kernel(sink_ref, qt_ref, k_ref, vt_ref, bias_ref, o_ref, *, n_real):
    tq = qt_ref.shape[2]
    t = pl.program_id(1)
    is_meta = t == n_real // tq
    span = tq + 2 * BLOCK
    start = pl.multiple_of(jnp.where(is_meta, 0, jnp.maximum(t * tq - BLOCK, 0)), BLOCK)
    kwin = jnp.concatenate([k_ref[0, pl.ds(start, span), :], k_ref[0, n_real:n_real + TAIL, :]], axis=0)
    vwin = jnp.concatenate([vt_ref[0, :, pl.ds(start, span)], vt_ref[0, :, n_real:n_real + TAIL]], axis=1)
    n_tiles = n_real // tq
    kind = jnp.where(is_meta, 4, (t == 0).astype(jnp.int32) + 2 * (t == n_tiles - 1).astype(jnp.int32))
    bias = jnp.concatenate([bias_ref[kind]] * GQA_GROUP, axis=1)
    qt = qt_ref[0]
    n_groups = GQA_HEADS // GQA_GROUP
    gw = GQA_GROUP * tq
    z = jnp.zeros((HEAD64, gw), BF16)
    rows = []
    for g in range(n_groups):
        qrow = jnp.concatenate([qt[h * HEAD64:(h + 1) * HEAD64] for h in range(g * GQA_GROUP, (g + 1) * GQA_GROUP)], axis=1)
        rows.append(jnp.concatenate([qrow if i == g else z for i in range(n_groups)], axis=1))
    wq = jnp.concatenate(rows, axis=0)
    sink = jnp.concatenate([jnp.full((1, tq), sink_ref[h] * LOG2E, F32) for h in range(GQA_HEADS)], axis=1)
    s = jnp.dot(kwin, wq, preferred_element_type=F32) + jnp.concatenate([bias, bias], axis=1)
    m = jnp.maximum(jnp.max(s, axis=0, keepdims=True), sink)
    p = jnp.exp2(s - m).astype(BF16)
    extra = jnp.exp2(sink - m)
    for g in range(n_groups):
        acc = jnp.dot(vwin[g * GQA_VT_ROWS:(g + 1) * GQA_VT_ROWS], p[:, g * gw:(g + 1) * gw], preferred_element_type=F32)
        ot = acc[:HEAD64] * (1.0 / (acc[HEAD64:HEAD64 + 1] + extra[:, g * gw:(g + 1) * gw]))
        for i in range(GQA_GROUP // 2):
            pair = jnp.concatenate([ot[:, 2 * i * tq:(2 * i + 1) * tq], ot[:, (2 * i + 1) * tq:(2 * i + 2) * tq]], axis=0)
            c = g * (GQA_GROUP // 2) + i
            o_ref[0, :, c * LANES:(c + 1) * LANES] = pair.T.astype(o_ref.dtype)


def _window_bias(n_real, tq):
    span = tq + 2 * BLOCK
    ks = jnp.arange(span + TAIL, dtype=jnp.int32)[:, None]
    qi = jnp.arange(tq, dtype=jnp.int32)[None, :]
    in_span = ks < span
    tail_ok = (ks >= span) & (ks < span + N_META)
    n_tiles = n_real // tq
    tables = []
    for first, last in ((False, False), (True, False), (False, True), (True, True)):
        t = 0 if first else (n_tiles - 1 if last else 1)
        kpos = max(t * tq - BLOCK, 0) + ks
        dist = kpos - (t * tq + qi)
        tables.append((dist <= WINDOW) & (dist >= -WINDOW) & in_span & (kpos < n_real) | tail_ok)
    tables.append(in_span & (ks <= qi + (WINDOW - N_META)) | tail_ok)
    return jnp.where(jnp.stack(tables), 0.0, NEG).astype(F32)


def _window_attn_call(qt, k, vt, sink, n_real):
    B, Lp, _ = k.shape
    tq = 256 if n_real % 256 == 0 and n_real >= 384 else BLOCK
    assert n_real % tq == 0 and tq + 2 * BLOCK <= Lp
    bias = _window_bias(n_real, tq)
    grid_spec = pltpu.PrefetchScalarGridSpec(
        num_scalar_prefetch=1, grid=(B, n_real // tq + 1),
        in_specs=[pl.BlockSpec((1, GQA_HEADS * HEAD64, tq), lambda b, t, s: (b, 0, t)),
                  pl.BlockSpec((1, Lp, LANES), lambda b, t, s: (b, 0, 0)),
                  pl.BlockSpec((1, 2 * GQA_VT_ROWS, Lp), lambda b, t, s: (b, 0, 0)),
                  _resident(bias.shape)],
        out_specs=pl.BlockSpec((1, tq, 512), lambda b, t, s: (b, t, 0)))
    return pl.pallas_call(
        functools.partial(_window_attn_kernel, n_real=n_real),
        out_shape=jax.ShapeDtypeStruct((B, Lp, 512), BF16), grid_spec=grid_spec,
        compiler_params=_cparams(("parallel", "parallel")), name="window_attn",
    )(sink, qt, k, vt, bias)


def _post_kernel(*refs, has_tail):
    h_ref, tail_ref = (refs[0], refs[1]) if has_tail else (refs[0], None)
    oda_ref, om_ref, og_ref, ng_ref, wg_ref, wb_ref, wo_ref, mg_ref, wu_ref, wd_ref, out_ref = refs[2 if has_tail else 1:]
    x = _row_tile(h_ref, tail_ref)
    xn = ((x * _rms_scale(x, D_MODEL)) * ng_ref[...]).astype(BF16)
    gl = jnp.dot(xn, wg_ref[...], preferred_element_type=F32)
    merged = None
    for g, o_ref in enumerate((oda_ref, om_ref, og_ref)):
        proj = jnp.dot(o_ref[0], wb_ref[g], preferred_element_type=F32)
        term = jax.nn.sigmoid(gl[:, g * D_MODEL:(g + 1) * D_MODEL]) * proj
        merged = term if merged is None else merged + term
    h1 = x + jnp.dot(merged.astype(BF16), wo_ref[...], preferred_element_type=F32)
    h1n = ((h1 * _rms_scale(h1, D_MODEL)) * mg_ref[...]).astype(BF16)
    u = jnp.dot(h1n, wu_ref[...], preferred_element_type=F32)
    a = jnp.square(jnp.maximum(u, 0.0)).astype(BF16)
    out_ref[0] = h1 + jnp.dot(a, wd_ref[...], preferred_element_type=F32)


def _post_call(h, tail, oda, om, og, ng, wg, wb, wo, mg, wu, wd, out_rows):
    B, Lp, _ = oda.shape
    D = h.shape[2]
    tm = _pick(Lp, (POST_TM, 128))
    row = lambda b, j: (b, j, 0)
    stream = [h] if tail is None else [h, tail]
    in_specs = [_stream_spec(h, tm)] + ([] if tail is None else [_resident(tail.shape)]) + [
        pl.BlockSpec((1, tm, BRANCH_WIDTH), row)] * 3 + [_resident(a.shape) for a in (ng, wg, wb, wo, mg, wu, wd)]
    return pl.pallas_call(
        functools.partial(_post_kernel, has_tail=tail is not None),
        out_shape=jax.ShapeDtypeStruct((B, out_rows, D), F32), grid=(B, pl.cdiv(out_rows, tm)),
        in_specs=in_specs, out_specs=pl.BlockSpec((1, tm, D), row),
        compiler_params=_cparams(("parallel", "parallel")), name="post",
    )(*stream, oda, om, og, ng, wg, wb, wo, mg, wu, wd)


def _swap_halves(m):
    return jnp.flip(m.reshape(m.shape[:-1] + (m.shape[-1] // HEAD64, 2, HEAD64 // 2)), axis=-2).reshape(m.shape)


def _pair_gain(g):
    g = jnp.tile(g.astype(F32), 2)[None, :]
    return jnp.concatenate([g, _swap_halves(g)], axis=0)


def _pad_gain(g):
    return jnp.concatenate([g.astype(F32), jnp.zeros((LANES - g.shape[0],), F32)])[None, :]


def _layer_params(l, attn_norm_g, w_in, da_q_norm_g, da_k_norm_g, da_lam_q1, da_lam_k1, da_lam_q2, da_lam_k2,
                  da_subln_g, mla_cq_norm_g, mla_ckv_norm_g, mla_w_uq, mla_w_ukv, mla_q_norm_g, mla_k_norm_g,
                  gqa_q_norm_g, gqa_k_norm_g, gqa_sink, w_branch, w_out, mlp_norm_g, w_up, w_down):
    w = w_in[l]
    sec = [w[:, a:b] for a, b in zip(IN_OFF[:-1], IN_OFF[1:])]
    da_q, da_k, da_v, cq, ckv, kr, gq, gk, gv, gate = sec
    pad64 = lambda m: jnp.pad(m, ((0, 0),) * (m.ndim - 1) + ((0, LANES - HEAD64),))
    wa = jnp.concatenate([da_q, da_k, da_v, cq, ckv,
                          pad64(kr), pad64(_swap_halves(kr)), gq, _swap_halves(gq),
                          gk, _swap_halves(gk), gv], axis=1).astype(BF16)
    wuq = mla_w_uq[l].reshape(MLA_Q_RANK, MLA_HEADS, MLA_QK)
    wuq_r = wuq[:, :, MLA_NOPE:]
    wuq = jnp.concatenate([wuq[:, :, :MLA_NOPE], pad64(wuq_r), pad64(_swap_halves(wuq_r))], axis=-1)
    wuq = wuq.reshape(MLA_Q_RANK, MLA_HEADS * 3 * LANES).astype(BF16)
    wukv = mla_w_ukv[l].reshape(MLA_KV_RANK, MLA_HEADS, 2 * LANES)
    wuk = wukv[:, :, :MLA_NOPE].reshape(MLA_KV_RANK, MLA_HEADS * LANES).astype(BF16)
    wuv = wukv[:, :, MLA_NOPE:].reshape(MLA_KV_RANK, MLA_HEADS * LANES).astype(BF16)
    mq, mk = mla_q_norm_g[l], mla_k_norm_g[l]
    gt = jnp.concatenate([_pair_gain(da_q_norm_g[l]), _pair_gain(da_k_norm_g[l]),
                          _pair_gain(gqa_q_norm_g[l]), _pair_gain(gqa_k_norm_g[l]),
                          mq[None, :MLA_NOPE].astype(F32), _pad_gain(mq[MLA_NOPE:]), _pad_gain(_swap_halves(mq[MLA_NOPE:])),
                          mk[None, :MLA_NOPE].astype(F32), _pad_gain(mk[MLA_NOPE:]), _pad_gain(_swap_halves(mk[MLA_NOPE:])),
                          jnp.zeros((2, LANES), F32)], axis=0)
    lam_tab = jnp.concatenate([_pad_gain(v[l]) for v in (da_lam_q1, da_lam_k1, da_lam_q2, da_lam_k2)], axis=0)
    return dict(
        ng=attn_norm_g[l][None, :].astype(F32), wa=wa, wuq=wuq, wuk=wuk, wuv=wuv, gt=gt,
        gcq=mla_cq_norm_g[l][None, :].astype(F32), gckv=mla_ckv_norm_g[l][None, :].astype(F32),
        lam_tab=lam_tab, sub_g=da_subln_g[l][None, :].astype(F32), sink=gqa_sink[l].astype(F32),
        wg=gate.astype(BF16), wb=w_branch[l].astype(BF16), wo=w_out[l].astype(BF16),
        mg=mlp_norm_g[l][None, :].astype(F32), wu=w_up[l].astype(BF16), wd=w_down[l].astype(BF16))


def _rope_tables(n_real):
    r = jnp.arange(n_real + TAIL, dtype=jnp.int32)
    pos = jnp.where(r < n_real, r + N_META, jnp.where(r < n_real + N_META, r - n_real, 0)).astype(F32)
    inv_freq = 1.0 / (ROPE_THETA ** (jnp.arange(0, HEAD64, 2, dtype=F32) / HEAD64))
    ang = pos[:, None] * inv_freq[None, :]
    cos, sin = jnp.cos(ang), jnp.sin(ang)
    return jnp.tile(cos, (1, 4)), jnp.tile(jnp.concatenate([-sin, sin], axis=1), (1, 2))


def _trunk(x, meta_tokens, layers):
    B, S, D = x.shape
    assert D == D_MODEL and S % BLOCK == 0
    tail = jnp.concatenate([meta_tokens.astype(x.dtype), jnp.zeros((TAIL - N_META, D), x.dtype)], axis=0)
    h = x
    cos, sin = _rope_tables(S)
    for l, p in enumerate(layers):
        lam_init = 0.8 - 0.6 * math.exp(-0.3 * l)
        qtda, kda, vtda, qtm, km, vtm, qtg, kg, vtg = _prep_call(
            h, tail, cos, sin, p["ng"], p["wa"], p["wuq"], p["wuk"], p["wuv"], p["gt"], p["gcq"], p["gckv"])
        oda = _dense_attn_call(qtda, kda, vtda, S, differential=True, lam_tab=p["lam_tab"], sub_g=p["sub_g"],
                               lam_init=lam_init, name="diff_attn")
        om = _dense_attn_call(qtm, km, vtm, S, differential=False, name="latent_attn")
        og = _window_attn_call(qtg, kg, vtg, p["sink"], S)
        h = _post_call(h, tail, oda, om, og, p["ng"], p["wg"], p["wb"], p["wo"], p["mg"], p["wu"], p["wd"],
                       out_rows=S if l == len(layers) - 1 else S + TAIL)
        tail = None
    return h


def kernel(x_prompt, x_sample, meta_tokens, attn_norm_g, w_in, da_q_norm_g, da_k_norm_g, da_lam_q1, da_lam_k1, da_lam_q2, da_lam_k2, da_subln_g, mla_cq_norm_g, mla_ckv_norm_g, mla_w_uq, mla_w_ukv, mla_q_norm_g, mla_k_norm_g, gqa_q_norm_g, gqa_k_norm_g, gqa_sink, w_branch, w_out, mlp_norm_g, w_up, w_down):
    depth = w_in.shape[0]
    layers = [_layer_params(l, attn_norm_g, w_in, da_q_norm_g, da_k_norm_g, da_lam_q1, da_lam_k1, da_lam_q2,
                            da_lam_k2, da_subln_g, mla_cq_norm_g, mla_ckv_norm_g, mla_w_uq, mla_w_ukv,
                            mla_q_norm_g, mla_k_norm_g, gqa_q_norm_g, gqa_k_norm_g, gqa_sink, w_branch, w_out,
                            mlp_norm_g, w_up, w_down) for l in range(depth)]
    return (_trunk(x_prompt, meta_tokens, layers), _trunk(x_sample, meta_tokens, layers))
```

```python
import functools
import math

import jax
import jax.numpy as jnp
from jax import lax
from jax.experimental import pallas as pl
from jax.experimental.pallas import tpu as pltpu

F32 = jnp.float32
BF16 = jnp.bfloat16

D_MODEL = 1024
N_META = 16
BLOCK = 128
WINDOW = 128
ROPE_THETA = 10000.0
EPS = 1e-6
NEG = -1e30
LOG2E = 1.4426950408889634

HEAD64 = 64
MLA_HEADS = 4
MLA_NOPE = 128
MLA_QK = 192
MLA_Q_RANK = 384
MLA_KV_RANK = 256
GQA_HEADS = 8
GQA_GROUP = 4
BRANCH_WIDTH = 512
LANES = 128
TAIL = 128
ONES_ROWS = 16
VT_ROWS = LANES + ONES_ROWS
GQA_VT_ROWS = HEAD64 + ONES_ROWS
CHUNK_GROUP = 8
POST_TM = 384
LATENT_TQ = 768

A_DAQ, A_DAK, A_DAV, A_CQ, A_CKV, A_KR, A_KRS, A_GQ, A_GQS, A_GK, A_GKS, A_GV, A_END = (
    0, 512, 1024, 1536, 1920, 2176, 2304, 2432, 2944, 3456, 3584, 3712, 3840)
IN_OFF = (0, 512, 1024, 1536, 1920, 2176, 2240, 2752, 2880, 3008, 6080)

VMEM_LIMIT = 56 * 1024 * 1024
ATTN_VMEM_BUDGET = 44 * 1024 * 1024


def _pick(n, candidates):
    for c in candidates:
        if n % c == 0:
            return c
    raise ValueError(f"no tile in {candidates} divides {n}")


def _cparams(sem):
    return pltpu.CompilerParams(dimension_semantics=sem, vmem_limit_bytes=VMEM_LIMIT)


def _resident(shape):
    zeros = (0,) * len(shape)
    return pl.BlockSpec(shape, lambda *_: zeros, pipeline_mode=pl.Buffered(1))


def _rms_scale(x, width):
    return lax.rsqrt(jnp.sum(x * x, axis=-1, keepdims=True) * (1.0 / width) + EPS)


def _pair_rms_scale(xc, lo_mask):
    sq = xc * xc
    lo = jnp.sum(jnp.where(lo_mask, sq, 0.0), axis=-1, keepdims=True)
    hi = jnp.sum(jnp.where(lo_mask, 0.0, sq), axis=-1, keepdims=True)
    return jnp.where(lo_mask, lax.rsqrt(lo * (1.0 / HEAD64) + EPS), lax.rsqrt(hi * (1.0 / HEAD64) + EPS))


def _rope(xc, swapped, cos, sin_signed):
    return xc * cos + swapped * sin_signed


def _stream_spec(h, tm):
    last = (h.shape[1] - 1) // tm
    return pl.BlockSpec((1, tm, h.shape[2]), lambda b, j: (b, jnp.minimum(j, last), 0))


def _row_tile(h_ref, tail_ref):
    x = h_ref[0]
    if tail_ref is None:
        return x
    tm = x.shape[0]
    tail = jnp.concatenate([jnp.zeros((tm - TAIL, x.shape[1]), x.dtype), tail_ref[...]], axis=0)
    in_tail = lax.broadcasted_iota(jnp.int32, x.shape, 0) >= tm - TAIL
    is_last = pl.program_id(1) == pl.num_programs(1) - 1
    return jnp.where(jnp.logical_and(is_last, in_tail), tail, x)


def _prep_kernel(*refs, has_tail):
    h_ref, tail_ref = (refs[0], refs[1]) if has_tail else (refs[0], None)
    (cos_ref, sin_ref, ng_ref, wa_ref, wuq_ref, wuk_ref, wuv_ref, gt_ref, gcq_ref, gckv_ref,
     qtda_ref, kda_ref, vtda_ref, qtm_ref, km_ref, vtm_ref, qtg_ref, kg_ref, vtg_ref) = refs[2 if has_tail else 1:]
    x = _row_tile(h_ref, tail_ref)
    xn = ((x * _rms_scale(x, D_MODEL)) * ng_ref[...]).astype(BF16)
    xp = jnp.dot(xn, wa_ref[...], preferred_element_type=F32)
    cos = cos_ref[...]
    sin_s = sin_ref[...]
    lane = lax.broadcasted_iota(jnp.int32, cos.shape, 1)
    lo64 = lane < HEAD64
    s64 = (HEAD64 ** -0.5) * LOG2E
    s192 = (MLA_QK ** -0.5) * LOG2E

    first_half = (lane & 32) == 0

    def head_pair(col, col_swapped, gains):
        xc = xp[:, col:col + LANES]
        r = _pair_rms_scale(xc, lo64)
        y = (xc * r) * gains[0:1, :]
        if col_swapped is None:
            ys = jnp.where(first_half, pltpu.roll(y, LANES - 32, 1), pltpu.roll(y, 32, 1))
        else:
            ys = (xp[:, col_swapped:col_swapped + LANES] * r) * gains[1:2, :]
        return _rope(y, ys, cos, sin_s)

    ones_rows = jnp.ones((ONES_ROWS, x.shape[0]), BF16)

    def store_vt(vt_ref, hd, v_slab):
        vt_ref[0, hd * VT_ROWS:hd * VT_ROWS + LANES, :] = v_slab.T.astype(BF16)
        vt_ref[0, hd * VT_ROWS + LANES:(hd + 1) * VT_ROWS, :] = ones_rows

    g_q, g_k = gt_ref[0:2, :], gt_ref[2:4, :]
    for c in range(4):
        sl = slice(c * LANES, (c + 1) * LANES)
        qtda_ref[0, sl, :] = (head_pair(A_DAQ + c * LANES, None, g_q) * s64).T.astype(BF16)
        kda_ref[0, c] = head_pair(A_DAK + c * LANES, None, g_k).astype(BF16)
        store_vt(vtda_ref, c, xp[:, A_DAV + c * LANES:A_DAV + (c + 1) * LANES])

    cq = xp[:, A_CQ:A_CQ + MLA_Q_RANK]
    cqn = ((cq * _rms_scale(cq, MLA_Q_RANK)) * gcq_ref[...]).astype(BF16)
    qm = jnp.dot(cqn, wuq_ref[...], preferred_element_type=F32)
    ckv = xp[:, A_CKV:A_CKV + MLA_KV_RANK]
    ckvn = ((ckv * _rms_scale(ckv, MLA_KV_RANK)) * gckv_ref[...]).astype(BF16)
    kn_all = jnp.dot(ckvn, wuk_ref[...], preferred_element_type=F32)
    vm_all = jnp.dot(ckvn, wuv_ref[...], preferred_element_type=F32)
    kr = xp[:, A_KR:A_KR + LANES]
    krs = xp[:, A_KRS:A_KRS + LANES]
    kr_ss = jnp.sum(kr * kr, axis=-1, keepdims=True)
    gq_n, gq_r, gq_rs = gt_ref[8:9, :], gt_ref[9:10, :], gt_ref[10:11, :]
    gk_n, gk_r, gk_rs = gt_ref[11:12, :], gt_ref[12:13, :], gt_ref[13:14, :]
    for hd in range(MLA_HEADS):
        b = hd * 2 * LANES
        qb = hd * 3 * LANES
        qn = qm[:, qb:qb + LANES]
        qr = qm[:, qb + LANES:qb + 2 * LANES]
        qrs = qm[:, qb + 2 * LANES:qb + 3 * LANES]
        r = lax.rsqrt((jnp.sum(qn * qn, axis=-1, keepdims=True) + jnp.sum(qr * qr, axis=-1, keepdims=True))
                      * (1.0 / MLA_QK) + EPS)
        qtm_ref[0, b:b + LANES, :] = (((qn * r) * gq_n) * s192).T.astype(BF16)
        qtm_ref[0, b + LANES:b + 2 * LANES, :] = (
            _rope((qr * r) * gq_r, (qrs * r) * gq_rs, cos, sin_s) * s192).T.astype(BF16)
        kn = kn_all[:, hd * LANES:(hd + 1) * LANES]
        rk = lax.rsqrt((jnp.sum(kn * kn, axis=-1, keepdims=True) + kr_ss) * (1.0 / MLA_QK) + EPS)
        km_ref[0, hd, :, :LANES] = ((kn * rk) * gk_n).astype(BF16)
        km_ref[0, hd, :, LANES:] = _rope((kr * rk) * gk_r, (krs * rk) * gk_rs, cos, sin_s).astype(BF16)
        store_vt(vtm_ref, hd, vm_all[:, hd * LANES:(hd + 1) * LANES])

    g_q, g_k = gt_ref[4:6, :], gt_ref[6:8, :]
    for c in range(4):
        sl = slice(c * LANES, (c + 1) * LANES)
        qtg_ref[0, sl, :] = (head_pair(A_GQ + c * LANES, A_GQS + c * LANES, g_q) * s64).T.astype(BF16)
    kg_ref[0] = head_pair(A_GK, A_GKS, g_k).astype(BF16)
    vgt = xp[:, A_GV:A_GV + LANES].T.astype(BF16)
    for g in range(GQA_HEADS // GQA_GROUP):
        vtg_ref[0, g * GQA_VT_ROWS:g * GQA_VT_ROWS + HEAD64, :] = vgt[g * HEAD64:(g + 1) * HEAD64]
        vtg_ref[0, g * GQA_VT_ROWS + HEAD64:(g + 1) * GQA_VT_ROWS, :] = ones_rows


def _prep_call(h, tail, cos, sin, ng, wa, wuq, wuk, wuv, gt, gcq, gckv):
    B, _, D = h.shape
    Lp = cos.shape[0]
    tm = _pick(Lp, (384, 128))
    row = lambda b, j: (b, j, 0)
    colt = lambda b, j: (b, 0, j)
    tab = pl.BlockSpec((tm, LANES), lambda b, j: (j, 0))
    sds = lambda *s: jax.ShapeDtypeStruct(s, BF16)
    vt_rows = 4 * VT_ROWS
    out_shape = [sds(B, 512, Lp), sds(B, 4, Lp, LANES), sds(B, vt_rows, Lp),
                 sds(B, 1024, Lp), sds(B, MLA_HEADS, Lp, 2 * LANES), sds(B, vt_rows, Lp),
                 sds(B, 512, Lp), sds(B, Lp, LANES), sds(B, 2 * GQA_VT_ROWS, Lp)]
    hrow = lambda b, j: (b, 0, j, 0)
    out_specs = [pl.BlockSpec((1, 512, tm), colt), pl.BlockSpec((1, 4, tm, LANES), hrow), pl.BlockSpec((1, vt_rows, tm), colt),
                 pl.BlockSpec((1, 1024, tm), colt), pl.BlockSpec((1, MLA_HEADS, tm, 2 * LANES), hrow),
                 pl.BlockSpec((1, vt_rows, tm), colt),
                 pl.BlockSpec((1, 512, tm), colt), pl.BlockSpec((1, tm, LANES), row),
                 pl.BlockSpec((1, 2 * GQA_VT_ROWS, tm), colt)]
    stream = [h] if tail is None else [h, tail]
    in_specs = [_stream_spec(h, tm)] + ([] if tail is None else [_resident(tail.shape)]) + [
        tab, tab, _resident(ng.shape), _resident(wa.shape), _resident(wuq.shape), _resident(wuk.shape),
        _resident(wuv.shape), _resident(gt.shape), _resident(gcq.shape), _resident(gckv.shape)]
    return pl.pallas_call(
        functools.partial(_prep_kernel, has_tail=tail is not None), out_shape=out_shape, grid=(B, Lp // tm),
        in_specs=in_specs, out_specs=out_specs,
        compiler_params=_cparams(("parallel", "parallel")), name="prep",
    )(*stream, cos, sin, ng, wa, wuq, wuk, wuv, gt, gcq, gckv)


def _dense_attn_kernel(*refs, n_real, tk, differential, lam_init, heads_per_step):
    if differential:
        qt_ref, k_ref, vt_ref, lam_ref, sg_ref, o_ref = refs[:6]
    else:
        (qt_ref, k_ref, vt_ref, o_ref), lam_ref, sg_ref = refs[:4], None, None
    s_refs = refs[len(refs) - heads_per_step:]
    dq = qt_ref.shape[1] // heads_per_step
    for hh in range(heads_per_step):
        _attend_head(qt_ref.at[0, hh * dq:(hh + 1) * dq], k_ref.at[0, hh], vt_ref.at[0, hh * VT_ROWS:(hh + 1) * VT_ROWS],
                     lam_ref, sg_ref, o_ref.at[0, :, hh * LANES:(hh + 1) * LANES], s_refs[hh],
                     n_real=n_real, tk=tk, differential=differential, lam_init=lam_init)


def _attend_head(qt_ref, k_ref, vt_ref, lam_ref, sg_ref, o_ref, s_ref, *, n_real, tk, differential, lam_init):
    tq = qt_ref.shape[1]
    dv = o_ref.shape[1]
    qt = qt_ref[...]
    if differential:
        z = jnp.zeros((HEAD64, tq), BF16)
        wq = jnp.concatenate([jnp.concatenate([qt[:HEAD64], z], axis=1),
                              jnp.concatenate([z, qt[HEAD64:]], axis=1)], axis=0)
    else:
        wq = qt
    width = wq.shape[1]

    def scores(kc):
        return jnp.dot(kc, wq, preferred_element_type=F32)

    def absorb(m, acc, s, vtc):
        m_new = jnp.maximum(m, jnp.max(s, axis=0, keepdims=True))
        p = jnp.exp2(s - m_new).astype(BF16)
        alpha = jnp.exp2(m - m_new)
        return m_new, alpha * acc + jnp.dot(vtc, p, preferred_element_type=F32)

    def kchunk(c):
        return k_ref[pl.ds(pl.multiple_of(c * tk, tk), tk), :]

    def vchunk(c):
        return vt_ref[:, pl.ds(pl.multiple_of(c * tk, tk), tk)]

    def group(j, carry):
        m, acc = carry
        for g in range(CHUNK_GROUP):
            c = CHUNK_GROUP * j + g
            s_ref[(g + 1) % 2] = scores(kchunk(c + 1))
            m, acc = absorb(m, acc, s_ref[g % 2], vchunk(c))
        return m, acc

    n_chunks = n_real // tk
    n_groups = (n_chunks - 1) // CHUNK_GROUP
    s_ref[0] = scores(kchunk(0))
    carry = (jnp.full((1, width), NEG, F32), jnp.zeros((dv + ONES_ROWS, width), F32))
    m, acc = lax.fori_loop(0, n_groups, group, carry)
    valid = lax.broadcasted_iota(jnp.int32, (TAIL, width), 0) < N_META
    first = CHUNK_GROUP * n_groups
    for c in range(first, n_chunks):
        if c + 1 < n_chunks:
            s_ref[(c - first + 1) % 2] = scores(kchunk(c + 1))
        else:
            s_tail = jnp.where(valid, scores(k_ref[n_real:n_real + TAIL, :]), NEG)
        m, acc = absorb(m, acc, s_ref[(c - first) % 2], vchunk(c))
    _, acc = absorb(m, acc, s_tail, vt_ref[:, n_real:n_real + TAIL])

    ot = acc[:dv] * (1.0 / acc[dv:dv + 1])
    if differential:
        lt = lam_ref[...]
        lam = (jnp.exp(jnp.sum(lt[0:1, :] * lt[1:2, :], axis=-1, keepdims=True))
               - jnp.exp(jnp.sum(lt[2:3, :] * lt[3:4, :], axis=-1, keepdims=True)) + lam_init)
        o = (ot[:, :tq] - lam * ot[:, tq:]).T
        o = ((o * _rms_scale(o, dv)) * sg_ref[...]) * (1.0 - lam_init)
    else:
        o = ot.T
    o_ref[...] = o.astype(o_ref.dtype)


def _dense_attn_call(qt, k, vt, n_real, *, differential, lam_tab=None, sub_g=None, lam_init=0.0, name):
    B, heads, Lp, dq = k.shape
    if differential:
        tq = _pick(Lp, (384, 128))
    else:
        tq = LATENT_TQ if Lp >= LATENT_TQ else _pick(n_real, (512, 256, 128))
    tk = _pick(n_real, (512, 256, 128))
    width = (2 if differential else 1) * tq
    per_head = 2 * 2 * Lp * (dq + VT_ROWS) + 4 * 2 * tk * width
    hps = 2 if heads % 2 == 0 and 2 * per_head <= ATTN_VMEM_BUDGET else 1
    in_specs = [pl.BlockSpec((1, hps * dq, tq), lambda b, h, i: (b, h, i)),
                pl.BlockSpec((1, hps, Lp, dq), lambda b, h, i: (b, h, 0, 0)),
                pl.BlockSpec((1, hps * VT_ROWS, Lp), lambda b, h, i: (b, h, 0))]
    args = [qt, k, vt]
    if differential:
        in_specs += [_resident(lam_tab.shape), _resident(sub_g.shape)]
        args += [lam_tab, sub_g]
    kern = functools.partial(_dense_attn_kernel, n_real=n_real, tk=tk, differential=differential, lam_init=lam_init,
                             heads_per_step=hps)
    return pl.pallas_call(
        kern, out_shape=jax.ShapeDtypeStruct((B, Lp, heads * LANES), BF16), grid=(B, heads // hps, pl.cdiv(Lp, tq)),
        in_specs=in_specs, out_specs=pl.BlockSpec((1, tq, hps * LANES), lambda b, h, i: (b, i, h)),
        scratch_shapes=[pltpu.VMEM((2, tk, width), F32)] * hps,
        compiler_params=_cparams(("parallel", "parallel", "parallel")), name=name,
    )(*args)


def _window_attn_kernel(sink_ref, qt_ref, k_ref, vt_ref, bias_ref, o_ref, *, n_real, tq):
    n_tiles = n_real // tq
    span = tq + 2 * BLOCK
    n_groups = GQA_HEADS // GQA_GROUP
    gw = GQA_GROUP * tq
    z = jnp.zeros((HEAD64, gw), BF16)
    sink = jnp.concatenate([jnp.full((1, tq), sink_ref[h] * LOG2E, F32) for h in range(GQA_HEADS)], axis=1)
    per_step = qt_ref.shape[2] // tq
    for u in range(per_step):
        t = pl.program_id(1) * per_step + u
        past = t >= n_tiles
        start = pl.multiple_of(jnp.where(past, 0, jnp.maximum(t * tq - BLOCK, 0)), BLOCK)
        kwin = jnp.concatenate([k_ref[0, pl.ds(start, span), :], k_ref[0, n_real:n_real + TAIL, :]], axis=0)
        vwin = jnp.concatenate([vt_ref[0, :, pl.ds(start, span)], vt_ref[0, :, n_real:n_real + TAIL]], axis=1)
        kind = jnp.where(past, 4, (t == 0).astype(jnp.int32) + 2 * (t == n_tiles - 1).astype(jnp.int32))
        bias = jnp.concatenate([bias_ref[kind]] * GQA_HEADS, axis=1)
        qt = qt_ref[0, :, u * tq:(u + 1) * tq]
        rows = []
        for g in range(n_groups):
            qrow = jnp.concatenate([qt[h * HEAD64:(h + 1) * HEAD64] for h in range(g * GQA_GROUP, (g + 1) * GQA_GROUP)], axis=1)
            rows.append(jnp.concatenate([qrow if i == g else z for i in range(n_groups)], axis=1))
        wq = jnp.concatenate(rows, axis=0)
        s = jnp.dot(kwin, wq, preferred_element_type=F32) + bias
        m = jnp.maximum(jnp.max(s, axis=0, keepdims=True), sink)
        p = jnp.exp2(s - m).astype(BF16)
        extra = jnp.exp2(sink - m)
        for g in range(n_groups):
            acc = jnp.dot(vwin[g * GQA_VT_ROWS:(g + 1) * GQA_VT_ROWS], p[:, g * gw:(g + 1) * gw], preferred_element_type=F32)
            ot = acc[:HEAD64] * (1.0 / (acc[HEAD64:HEAD64 + 1] + extra[:, g * gw:(g + 1) * gw]))
            for i in range(GQA_GROUP // 2):
                pair = jnp.concatenate([ot[:, 2 * i * tq:(2 * i + 1) * tq], ot[:, (2 * i + 1) * tq:(2 * i + 2) * tq]], axis=0)
                c = g * (GQA_GROUP // 2) + i
                o_ref[0, u * tq:(u + 1) * tq, c * LANES:(c + 1) * LANES] = pair.T.astype(o_ref.dtype)


def _window_bias(n_real, tq):
    span = tq + 2 * BLOCK
    ks = jnp.arange(span + TAIL, dtype=jnp.int32)[:, None]
    qi = jnp.arange(tq, dtype=jnp.int32)[None, :]
    in_span = ks < span
    tail_ok = (ks >= span) & (ks < span + N_META)
    n_tiles = n_real // tq
    tables = []
    for first, last in ((False, False), (True, False), (False, True), (True, True)):
        t = 0 if first else (n_tiles - 1 if last else 1)
        kpos = max(t * tq - BLOCK, 0) + ks
        dist = kpos - (t * tq + qi)
        tables.append((dist <= WINDOW) & (dist >= -WINDOW) & in_span & (kpos < n_real) | tail_ok)
    tables.append(in_span & (ks <= qi + (WINDOW - N_META)) | tail_ok)
    return jnp.where(jnp.stack(tables), 0.0, NEG).astype(F32)


def _window_attn_call(qt, k, vt, sink, n_real):
    B, Lp, _ = k.shape
    tq = 256 if n_real % 256 == 0 and n_real >= 384 else BLOCK
    assert n_real % tq == 0 and tq + 2 * BLOCK <= Lp
    bias = _window_bias(n_real, tq)
    per_step = 2 if (n_real // tq) % 2 == 0 else 1
    ts = per_step * tq
    grid_spec = pltpu.PrefetchScalarGridSpec(
        num_scalar_prefetch=1, grid=(B, n_real // ts + 1),
        in_specs=[pl.BlockSpec((1, GQA_HEADS * HEAD64, ts), lambda b, t, s: (b, 0, t)),
                  pl.BlockSpec((1, Lp, LANES), lambda b, t, s: (b, 0, 0)),
                  pl.BlockSpec((1, 2 * GQA_VT_ROWS, Lp), lambda b, t, s: (b, 0, 0)),
                  _resident(bias.shape)],
        out_specs=pl.BlockSpec((1, ts, 512), lambda b, t, s: (b, t, 0)))
    return pl.pallas_call(
        functools.partial(_window_attn_kernel, n_real=n_real, tq=tq),
        out_shape=jax.ShapeDtypeStruct((B, Lp, 512), BF16), grid_spec=grid_spec,
        compiler_params=_cparams(("parallel", "parallel")), name="window_attn",
    )(sink, qt, k, vt, bias)


def _post_kernel(*refs, has_tail):
    h_ref, tail_ref = (refs[0], refs[1]) if has_tail else (refs[0], None)
    oda_ref, om_ref, og_ref, ng_ref, wg_ref, wb_ref, wo_ref, mg_ref, wu_ref, wd_ref, out_ref = refs[2 if has_tail else 1:]
    x = _row_tile(h_ref, tail_ref)
    xn = ((x * _rms_scale(x, D_MODEL)) * ng_ref[...]).astype(BF16)
    gl = jnp.dot(xn, wg_ref[...], preferred_element_type=F32)
    merged = None
    for g, o_ref in enumerate((oda_ref, om_ref, og_ref)):
        proj = jnp.dot(o_ref[0], wb_ref[g], preferred_element_type=F32)
        term = jax.nn.sigmoid(gl[:, g * D_MODEL:(g + 1) * D_MODEL]) * proj
        merged = term if merged is None else merged + term
    h1 = x + jnp.dot(merged.astype(BF16), wo_ref[...], preferred_element_type=F32)
    h1n = ((h1 * _rms_scale(h1, D_MODEL)) * mg_ref[...]).astype(BF16)
    u = jnp.dot(h1n, wu_ref[...], preferred_element_type=F32)
    a = jnp.square(jnp.maximum(u, 0.0)).astype(BF16)
    out_ref[0] = h1 + jnp.dot(a, wd_ref[...], preferred_element_type=F32)


def _post_call(h, tail, oda, om, og, ng, wg, wb, wo, mg, wu, wd, out_rows):
    B, Lp, _ = oda.shape
    D = h.shape[2]
    tm = _pick(Lp, (POST_TM, 128))
    row = lambda b, j: (b, j, 0)
    stream = [h] if tail is None else [h, tail]
    in_specs = [_stream_spec(h, tm)] + ([] if tail is None else [_resident(tail.shape)]) + [
        pl.BlockSpec((1, tm, BRANCH_WIDTH), row)] * 3 + [_resident(a.shape) for a in (ng, wg, wb, wo, mg, wu, wd)]
    return pl.pallas_call(
        functools.partial(_post_kernel, has_tail=tail is not None),
        out_shape=jax.ShapeDtypeStruct((B, out_rows, D), F32), grid=(B, pl.cdiv(out_rows, tm)),
        in_specs=in_specs, out_specs=pl.BlockSpec((1, tm, D), row),
        compiler_params=_cparams(("parallel", "parallel")), name="post",
    )(*stream, oda, om, og, ng, wg, wb, wo, mg, wu, wd)


def _swap_halves(m):
    return jnp.flip(m.reshape(m.shape[:-1] + (m.shape[-1] // HEAD64, 2, HEAD64 // 2)), axis=-2).reshape(m.shape)


def _pair_gain(g):
    g = jnp.tile(g.astype(F32), 2)[None, :]
    return jnp.concatenate([g, _swap_halves(g)], axis=0)


def _pad_gain(g):
    return jnp.concatenate([g.astype(F32), jnp.zeros((LANES - g.shape[0],), F32)])[None, :]


def _layer_params(l, attn_norm_g, w_in, da_q_norm_g, da_k_norm_g, da_lam_q1, da_lam_k1, da_lam_q2, da_lam_k2,
                  da_subln_g, mla_cq_norm_g, mla_ckv_norm_g, mla_w_uq, mla_w_ukv, mla_q_norm_g, mla_k_norm_g,
                  gqa_q_norm_g, gqa_k_norm_g, gqa_sink, w_branch, w_out, mlp_norm_g, w_up, w_down):
    w = w_in[l]
    sec = [w[:, a:b] for a, b in zip(IN_OFF[:-1], IN_OFF[1:])]
    da_q, da_k, da_v, cq, ckv, kr, gq, gk, gv, gate = sec
    pad64 = lambda m: jnp.pad(m, ((0, 0),) * (m.ndim - 1) + ((0, LANES - HEAD64),))
    wa = jnp.concatenate([da_q, da_k, da_v, cq, ckv,
                          pad64(kr), pad64(_swap_halves(kr)), gq, _swap_halves(gq),
                          gk, _swap_halves(gk), gv], axis=1).astype(BF16)
    wuq = mla_w_uq[l].reshape(MLA_Q_RANK, MLA_HEADS, MLA_QK)
    wuq_r = wuq[:, :, MLA_NOPE:]
    wuq = jnp.concatenate([wuq[:, :, :MLA_NOPE], pad64(wuq_r), pad64(_swap_halves(wuq_r))], axis=-1)
    wuq = wuq.reshape(MLA_Q_RANK, MLA_HEADS * 3 * LANES).astype(BF16)
    wukv = mla_w_ukv[l].reshape(MLA_KV_RANK, MLA_HEADS, 2 * LANES)
    wuk = wukv[:, :, :MLA_NOPE].reshape(MLA_KV_RANK, MLA_HEADS * LANES).astype(BF16)
    wuv = wukv[:, :, MLA_NOPE:].reshape(MLA_KV_RANK, MLA_HEADS * LANES).astype(BF16)
    mq, mk = mla_q_norm_g[l], mla_k_norm_g[l]
    gt = jnp.concatenate([_pair_gain(da_q_norm_g[l]), _pair_gain(da_k_norm_g[l]),
                          _pair_gain(gqa_q_norm_g[l]), _pair_gain(gqa_k_norm_g[l]),
                          mq[None, :MLA_NOPE].astype(F32), _pad_gain(mq[MLA_NOPE:]), _pad_gain(_swap_halves(mq[MLA_NOPE:])),
                          mk[None, :MLA_NOPE].astype(F32), _pad_gain(mk[MLA_NOPE:]), _pad_gain(_swap_halves(mk[MLA_NOPE:])),
                          jnp.zeros((2, LANES), F32)], axis=0)
    lam_tab = jnp.concatenate([_pad_gain(v[l]) for v in (da_lam_q1, da_lam_k1, da_lam_q2, da_lam_k2)], axis=0)
    return dict(
        ng=attn_norm_g[l][None, :].astype(F32), wa=wa, wuq=wuq, wuk=wuk, wuv=wuv, gt=gt,
        gcq=mla_cq_norm_g[l][None, :].astype(F32), gckv=mla_ckv_norm_g[l][None, :].astype(F32),
        lam_tab=lam_tab, sub_g=da_subln_g[l][None, :].astype(F32), sink=gqa_sink[l].astype(F32),
        wg=gate.astype(BF16), wb=w_branch[l].astype(BF16), wo=w_out[l].astype(BF16),
        mg=mlp_norm_g[l][None, :].astype(F32), wu=w_up[l].astype(BF16), wd=w_down[l].astype(BF16))


def _rope_tables(n_real):
    r = jnp.arange(n_real + TAIL, dtype=jnp.int32)
    pos = jnp.where(r < n_real, r + N_META, jnp.where(r < n_real + N_META, r - n_real, 0)).astype(F32)
    inv_freq = 1.0 / (ROPE_THETA ** (jnp.arange(0, HEAD64, 2, dtype=F32) / HEAD64))
    ang = pos[:, None] * inv_freq[None, :]
    cos, sin = jnp.cos(ang), jnp.sin(ang)
    return jnp.tile(cos, (1, 4)), jnp.tile(jnp.concatenate([-sin, sin], axis=1), (1, 2))


def _trunk(x, meta_tokens, layers):
    B, S, D = x.shape
    assert D == D_MODEL and S % BLOCK == 0
    tail = jnp.concatenate([meta_tokens.astype(x.dtype), jnp.zeros((TAIL - N_META, D), x.dtype)], axis=0)
    h = x
    cos, sin = _rope_tables(S)
    for l, p in enumerate(layers):
        lam_init = 0.8 - 0.6 * math.exp(-0.3 * l)
        qtda, kda, vtda, qtm, km, vtm, qtg, kg, vtg = _prep_call(
            h, tail, cos, sin, p["ng"], p["wa"], p["wuq"], p["wuk"], p["wuv"], p["gt"], p["gcq"], p["gckv"])
        oda = _dense_attn_call(qtda, kda, vtda, S, differential=True, lam_tab=p["lam_tab"], sub_g=p["sub_g"],
                               lam_init=lam_init, name="diff_attn")
        om = _dense_attn_call(qtm, km, vtm, S, differential=False, name="latent_attn")
        og = _window_attn_call(qtg, kg, vtg, p["sink"], S)
        h = _post_call(h, tail, oda, om, og, p["ng"], p["wg"], p["wb"], p["wo"], p["mg"], p["wu"], p["wd"],
                       out_rows=S if l == len(layers) - 1 else S + TAIL)
        tail = None
    return h


def kernel(x_prompt, x_sample, meta_tokens, attn_norm_g, w_in, da_q_norm_g, da_k_norm_g, da_lam_q1, da_lam_k1, da_lam_q2, da_lam_k2, da_subln_g, mla_cq_norm_g, mla_ckv_norm_g, mla_w_uq, mla_w_ukv, mla_q_norm_g, mla_k_norm_g, gqa_q_norm_g, gqa_k_norm_g, gqa_sink, w_branch, w_out, mlp_norm_g, w_up, w_down):
    depth = w_in.shape[0]
    layers = [_layer_params(l, attn_norm_g, w_in, da_q_norm_g, da_k_norm_g, da_lam_q1, da_lam_k1, da_lam_q2,
                            da_lam_k2, da_subln_g, mla_cq_norm_g, mla_ckv_norm_g, mla_w_uq, mla_w_ukv,
                            mla_q_norm_g, mla_k_norm_g, gqa_q_norm_g, gqa_k_norm_g, gqa_sink, w_branch, w_out,
                            mlp_norm_g, w_up, w_down) for l in range(depth)]
    return (_trunk(x_prompt, meta_tokens, layers), _trunk(x_sample, meta_tokens, layers))
```

```python
import functools
import math

import jax
import jax.numpy as jnp
from jax import lax
from jax.experimental import pallas as pl
from jax.experimental.pallas import tpu as pltpu

F32 = jnp.float32
BF16 = jnp.bfloat16

D_MODEL = 1024
N_META = 16
BLOCK = 128
WINDOW = 128
ROPE_THETA = 10000.0
EPS = 1e-6
NEG = -1e30
LOG2E = 1.4426950408889634

HEAD64 = 64
MLA_HEADS = 4
MLA_NOPE = 128
MLA_QK = 192
MLA_Q_RANK = 384
MLA_KV_RANK = 256
GQA_HEADS = 8
GQA_GROUP = 4
BRANCH_WIDTH = 512
LANES = 128
TAIL = 128
ONES_ROWS = 16
VT_ROWS = LANES + ONES_ROWS
GQA_VT_ROWS = HEAD64 + ONES_ROWS
CHUNK_GROUP = 8
POST_TM = 384
LATENT_TQ = 768

A_DAQ, A_DAK, A_DAV, A_CQ, A_CKV, A_KR, A_KRS, A_GQ, A_GQS, A_GK, A_GKS, A_GV, A_END = (
    0, 512, 1024, 1536, 1920, 2176, 2304, 2432, 2944, 3456, 3584, 3712, 3840)
IN_OFF = (0, 512, 1024, 1536, 1920, 2176, 2240, 2752, 2880, 3008, 6080)

VMEM_LIMIT = 56 * 1024 * 1024
ATTN_VMEM_BUDGET = 44 * 1024 * 1024


def _pick(n, candidates):
    for c in candidates:
        if n % c == 0:
            return c
    raise ValueError(f"no tile in {candidates} divides {n}")


def _cparams(sem):
    return pltpu.CompilerParams(dimension_semantics=sem, vmem_limit_bytes=VMEM_LIMIT)


def _resident(shape):
    zeros = (0,) * len(shape)
    return pl.BlockSpec(shape, lambda *_: zeros, pipeline_mode=pl.Buffered(1))


def _rms_scale(x, width):
    return lax.rsqrt(jnp.sum(x * x, axis=-1, keepdims=True) * (1.0 / width) + EPS)


def _pair_rms_scale(xc, lo_mask):
    sq = xc * xc
    lo = jnp.sum(jnp.where(lo_mask, sq, 0.0), axis=-1, keepdims=True)
    hi = jnp.sum(jnp.where(lo_mask, 0.0, sq), axis=-1, keepdims=True)
    return jnp.where(lo_mask, lax.rsqrt(lo * (1.0 / HEAD64) + EPS), lax.rsqrt(hi * (1.0 / HEAD64) + EPS))


def _rope(xc, swapped, cos, sin_signed):
    return xc * cos + swapped * sin_signed


def _stream_spec(h, tm):
    last = (h.shape[1] - 1) // tm
    return pl.BlockSpec((1, tm, h.shape[2]), lambda b, j: (b, jnp.minimum(j, last), 0))


def _row_tile(h_ref, tail_ref):
    x = h_ref[0]
    if tail_ref is None:
        return x
    tm = x.shape[0]
    tail = jnp.concatenate([jnp.zeros((tm - TAIL, x.shape[1]), x.dtype), tail_ref[...]], axis=0)
    in_tail = lax.broadcasted_iota(jnp.int32, x.shape, 0) >= tm - TAIL
    is_last = pl.program_id(1) == pl.num_programs(1) - 1
    return jnp.where(jnp.logical_and(is_last, in_tail), tail, x)


def _prep_kernel(*refs, has_tail):
    h_ref, tail_ref = (refs[0], refs[1]) if has_tail else (refs[0], None)
    (cos_ref, sin_ref, ng_ref, wa_ref, wuq_ref, wuk_ref, wuv_ref, gt_ref, gcq_ref, gckv_ref,
     qtda_ref, kda_ref, vtda_ref, qtm_ref, km_ref, vtm_ref, qtg_ref, kg_ref, vtg_ref) = refs[2 if has_tail else 1:]
    x = _row_tile(h_ref, tail_ref)
    xn = ((x * _rms_scale(x, D_MODEL)) * ng_ref[...]).astype(BF16)
    xp = jnp.dot(xn, wa_ref[...], preferred_element_type=F32)
    cos = cos_ref[...]
    sin_s = sin_ref[...]
    lane = lax.broadcasted_iota(jnp.int32, cos.shape, 1)
    lo64 = lane < HEAD64
    s64 = (HEAD64 ** -0.5) * LOG2E
    s192 = (MLA_QK ** -0.5) * LOG2E

    first_half = (lane & 32) == 0

    def head_pair(col, col_swapped, gains):
        xc = xp[:, col:col + LANES]
        r = _pair_rms_scale(xc, lo64)
        y = (xc * r) * gains[0:1, :]
        if col_swapped is None:
            ys = jnp.where(first_half, pltpu.roll(y, LANES - 32, 1), pltpu.roll(y, 32, 1))
        else:
            ys = (xp[:, col_swapped:col_swapped + LANES] * r) * gains[1:2, :]
        return _rope(y, ys, cos, sin_s)

    ones_rows = jnp.ones((ONES_ROWS, x.shape[0]), BF16)

    def store_vt(vt_ref, hd, v_slab):
        vt_ref[0, hd * VT_ROWS:hd * VT_ROWS + LANES, :] = v_slab.T.astype(BF16)
        vt_ref[0, hd * VT_ROWS + LANES:(hd + 1) * VT_ROWS, :] = ones_rows

    g_q, g_k = gt_ref[0:2, :], gt_ref[2:4, :]
    for c in range(4):
        sl = slice(c * LANES, (c + 1) * LANES)
        qtda_ref[0, sl, :] = (head_pair(A_DAQ + c * LANES, None, g_q) * s64).T.astype(BF16)
        kda_ref[0, c] = head_pair(A_DAK + c * LANES, None, g_k).astype(BF16)
        store_vt(vtda_ref, c, xp[:, A_DAV + c * LANES:A_DAV + (c + 1) * LANES])

    cq = xp[:, A_CQ:A_CQ + MLA_Q_RANK]
    cqn = ((cq * _rms_scale(cq, MLA_Q_RANK)) * gcq_ref[...]).astype(BF16)
    qm = jnp.dot(cqn, wuq_ref[...], preferred_element_type=F32)
    ckv = xp[:, A_CKV:A_CKV + MLA_KV_RANK]
    ckvn = ((ckv * _rms_scale(ckv, MLA_KV_RANK)) * gckv_ref[...]).astype(BF16)
    kn_all = jnp.dot(ckvn, wuk_ref[...], preferred_element_type=F32)
    vm_all = jnp.dot(ckvn, wuv_ref[...], preferred_element_type=F32)
    kr = xp[:, A_KR:A_KR + LANES]
    krs = xp[:, A_KRS:A_KRS + LANES]
    kr_ss = jnp.sum(kr * kr, axis=-1, keepdims=True)
    gq_n, gq_r, gq_rs = gt_ref[8:9, :], gt_ref[9:10, :], gt_ref[10:11, :]
    gk_n, gk_r, gk_rs = gt_ref[11:12, :], gt_ref[12:13, :], gt_ref[13:14, :]
    for hd in range(MLA_HEADS):
        b = hd * 2 * LANES
        qb = hd * 3 * LANES
        qn = qm[:, qb:qb + LANES]
        qr = qm[:, qb + LANES:qb + 2 * LANES]
        qrs = qm[:, qb + 2 * LANES:qb + 3 * LANES]
        r = lax.rsqrt((jnp.sum(qn * qn, axis=-1, keepdims=True) + jnp.sum(qr * qr, axis=-1, keepdims=True))
                      * (1.0 / MLA_QK) + EPS)
        qtm_ref[0, b:b + LANES, :] = (((qn * r) * gq_n) * s192).T.astype(BF16)
        qtm_ref[0, b + LANES:b + 2 * LANES, :] = (
            _rope((qr * r) * gq_r, (qrs * r) * gq_rs, cos, sin_s) * s192).T.astype(BF16)
        kn = kn_all[:, hd * LANES:(hd + 1) * LANES]
        rk = lax.rsqrt((jnp.sum(kn * kn, axis=-1, keepdims=True) + kr_ss) * (1.0 / MLA_QK) + EPS)
        km_ref[0, hd, :, :LANES] = ((kn * rk) * gk_n).astype(BF16)
        km_ref[0, hd, :, LANES:] = _rope((kr * rk) * gk_r, (krs * rk) * gk_rs, cos, sin_s).astype(BF16)
        store_vt(vtm_ref, hd, vm_all[:, hd * LANES:(hd + 1) * LANES])

    g_q, g_k = gt_ref[4:6, :], gt_ref[6:8, :]
    for c in range(4):
        sl = slice(c * LANES, (c + 1) * LANES)
        qtg_ref[0, sl, :] = (head_pair(A_GQ + c * LANES, A_GQS + c * LANES, g_q) * s64).T.astype(BF16)
    kg_ref[0] = head_pair(A_GK, A_GKS, g_k).astype(BF16)
    vgt = xp[:, A_GV:A_GV + LANES].T.astype(BF16)
    for g in range(GQA_HEADS // GQA_GROUP):
        vtg_ref[0, g * GQA_VT_ROWS:g * GQA_VT_ROWS + HEAD64, :] = vgt[g * HEAD64:(g + 1) * HEAD64]
        vtg_ref[0, g * GQA_VT_ROWS + HEAD64:(g + 1) * GQA_VT_ROWS, :] = ones_rows


def _prep_call(h, tail, cos, sin, ng, wa, wuq, wuk, wuv, gt, gcq, gckv):
    B, _, D = h.shape
    Lp = cos.shape[0]
    tm = _pick(Lp, (384, 128))
    row = lambda b, j: (b, j, 0)
    colt = lambda b, j: (b, 0, j)
    tab = pl.BlockSpec((tm, LANES), lambda b, j: (j, 0))
    sds = lambda *s: jax.ShapeDtypeStruct(s, BF16)
    vt_rows = 4 * VT_ROWS
    out_shape = [sds(B, 512, Lp), sds(B, 4, Lp, LANES), sds(B, vt_rows, Lp),
                 sds(B, 1024, Lp), sds(B, MLA_HEADS, Lp, 2 * LANES), sds(B, vt_rows, Lp),
                 sds(B, 512, Lp), sds(B, Lp, LANES), sds(B, 2 * GQA_VT_ROWS, Lp)]
    hrow = lambda b, j: (b, 0, j, 0)
    out_specs = [pl.BlockSpec((1, 512, tm), colt), pl.BlockSpec((1, 4, tm, LANES), hrow), pl.BlockSpec((1, vt_rows, tm), colt),
                 pl.BlockSpec((1, 1024, tm), colt), pl.BlockSpec((1, MLA_HEADS, tm, 2 * LANES), hrow),
                 pl.BlockSpec((1, vt_rows, tm), colt),
                 pl.BlockSpec((1, 512, tm), colt), pl.BlockSpec((1, tm, LANES), row),
                 pl.BlockSpec((1, 2 * GQA_VT_ROWS, tm), colt)]
    stream = [h] if tail is None else [h, tail]
    in_specs = [_stream_spec(h, tm)] + ([] if tail is None else [_resident(tail.shape)]) + [
        tab, tab, _resident(ng.shape), _resident(wa.shape), _resident(wuq.shape), _resident(wuk.shape),
        _resident(wuv.shape), _resident(gt.shape), _resident(gcq.shape), _resident(gckv.shape)]
    return pl.pallas_call(
        functools.partial(_prep_kernel, has_tail=tail is not None), out_shape=out_shape, grid=(B, Lp // tm),
        in_specs=in_specs, out_specs=out_specs,
        compiler_params=_cparams(("parallel", "parallel")), name="prep",
    )(*stream, cos, sin, ng, wa, wuq, wuk, wuv, gt, gcq, gckv)


def _dense_attn_kernel(*refs, n_real, tk, differential, lam_init, heads_per_step, tiles_per_step):
    if differential:
        qt_ref, k_ref, vt_ref, lam_ref, sg_ref, o_ref = refs[:6]
    else:
        (qt_ref, k_ref, vt_ref, o_ref), lam_ref, sg_ref = refs[:4], None, None
    s_refs = refs[len(refs) - heads_per_step * tiles_per_step:]
    dq = qt_ref.shape[1] // heads_per_step
    tq = qt_ref.shape[2] // tiles_per_step
    for hh in range(heads_per_step):
        for u in range(tiles_per_step):
            _attend_head(qt_ref.at[0, hh * dq:(hh + 1) * dq, u * tq:(u + 1) * tq], k_ref.at[0, hh],
                         vt_ref.at[0, hh * VT_ROWS:(hh + 1) * VT_ROWS], lam_ref, sg_ref,
                         o_ref.at[0, u * tq:(u + 1) * tq, hh * LANES:(hh + 1) * LANES], s_refs[hh * tiles_per_step + u],
                         n_real=n_real, tk=tk, differential=differential, lam_init=lam_init)


def _attend_head(qt_ref, k_ref, vt_ref, lam_ref, sg_ref, o_ref, s_ref, *, n_real, tk, differential, lam_init):
    tq = qt_ref.shape[1]
    dv = o_ref.shape[1]
    qt = qt_ref[...]
    if differential:
        z = jnp.zeros((HEAD64, tq), BF16)
        wq = jnp.concatenate([jnp.concatenate([qt[:HEAD64], z], axis=1),
                              jnp.concatenate([z, qt[HEAD64:]], axis=1)], axis=0)
    else:
        wq = qt
    width = wq.shape[1]

    def scores(kc):
        return jnp.dot(kc, wq, preferred_element_type=F32)

    def absorb(m, acc, s, vtc):
        m_new = jnp.maximum(m, jnp.max(s, axis=0, keepdims=True))
        p = jnp.exp2(s - m_new).astype(BF16)
        alpha = jnp.exp2(m - m_new)
        return m_new, alpha * acc + jnp.dot(vtc, p, preferred_element_type=F32)

    def kchunk(c):
        return k_ref[pl.ds(pl.multiple_of(c * tk, tk), tk), :]

    def vchunk(c):
        return vt_ref[:, pl.ds(pl.multiple_of(c * tk, tk), tk)]

    def group(j, carry):
        m, acc = carry
        for g in range(CHUNK_GROUP):
            c = CHUNK_GROUP * j + g
            s_ref[(g + 1) % 2] = scores(kchunk(c + 1))
            m, acc = absorb(m, acc, s_ref[g % 2], vchunk(c))
        return m, acc

    n_chunks = n_real // tk
    n_groups = (n_chunks - 1) // CHUNK_GROUP
    s_ref[0] = scores(kchunk(0))
    carry = (jnp.full((1, width), NEG, F32), jnp.zeros((dv + ONES_ROWS, width), F32))
    m, acc = lax.fori_loop(0, n_groups, group, carry)
    valid = lax.broadcasted_iota(jnp.int32, (TAIL, width), 0) < N_META
    first = CHUNK_GROUP * n_groups
    for c in range(first, n_chunks):
        if c + 1 < n_chunks:
            s_ref[(c - first + 1) % 2] = scores(kchunk(c + 1))
        else:
            s_tail = jnp.where(valid, scores(k_ref[n_real:n_real + TAIL, :]), NEG)
        m, acc = absorb(m, acc, s_ref[(c - first) % 2], vchunk(c))
    _, acc = absorb(m, acc, s_tail, vt_ref[:, n_real:n_real + TAIL])

    ot = acc[:dv] * (1.0 / acc[dv:dv + 1])
    if differential:
        lt = lam_ref[...]
        lam = (jnp.exp(jnp.sum(lt[0:1, :] * lt[1:2, :], axis=-1, keepdims=True))
               - jnp.exp(jnp.sum(lt[2:3, :] * lt[3:4, :], axis=-1, keepdims=True)) + lam_init)
        o = (ot[:, :tq] - lam * ot[:, tq:]).T
        o = ((o * _rms_scale(o, dv)) * sg_ref[...]) * (1.0 - lam_init)
    else:
        o = ot.T
    o_ref[...] = o.astype(o_ref.dtype)


def _dense_attn_call(qt, k, vt, n_real, *, differential, lam_tab=None, sub_g=None, lam_init=0.0, name):
    B, heads, Lp, dq = k.shape
    if differential:
        tq = _pick(Lp, (384, 128))
    else:
        tq = LATENT_TQ if Lp >= LATENT_TQ else _pick(n_real, (512, 256, 128))
    tk = _pick(n_real, (512, 256, 128))
    width = (2 if differential else 1) * tq
    per_head = 2 * 2 * Lp * (dq + VT_ROWS) + 4 * 2 * tk * width
    hps = max(n for n in (4, 2, 1) if heads % n == 0 and (n == 1 or n * per_head <= ATTN_VMEM_BUDGET))
    tps = 2 if hps == 1 else 1
    in_specs = [pl.BlockSpec((1, hps * dq, tps * tq), lambda b, h, i: (b, h, i)),
                pl.BlockSpec((1, hps, Lp, dq), lambda b, h, i: (b, h, 0, 0)),
                pl.BlockSpec((1, hps * VT_ROWS, Lp), lambda b, h, i: (b, h, 0))]
    args = [qt, k, vt]
    if differential:
        in_specs += [_resident(lam_tab.shape), _resident(sub_g.shape)]
        args += [lam_tab, sub_g]
    kern = functools.partial(_dense_attn_kernel, n_real=n_real, tk=tk, differential=differential, lam_init=lam_init,
                             heads_per_step=hps, tiles_per_step=tps)
    return pl.pallas_call(
        kern, out_shape=jax.ShapeDtypeStruct((B, Lp, heads * LANES), BF16),
        grid=(B, heads // hps, pl.cdiv(Lp, tps * tq)),
        in_specs=in_specs, out_specs=pl.BlockSpec((1, tps * tq, hps * LANES), lambda b, h, i: (b, i, h)),
        scratch_shapes=[pltpu.VMEM((2, tk, width), F32)] * (hps * tps),
        compiler_params=_cparams(("parallel", "parallel", "parallel")), name=name,
    )(*args)


def _window_attn_kernel(sink_ref, qt_ref, k_ref, vt_ref, bias_ref, o_ref, *, n_real):
    tq = qt_ref.shape[2]
    t = pl.program_id(1)
    is_meta = t == n_real // tq
    span = tq + 2 * BLOCK
    start = pl.multiple_of(jnp.where(is_meta, 0, jnp.maximum(t * tq - BLOCK, 0)), BLOCK)
    kwin = jnp.concatenate([k_ref[0, pl.ds(start, span), :], k_ref[0, n_real:n_real + TAIL, :]], axis=0)
    vwin = jnp.concatenate([vt_ref[0, :, pl.ds(start, span)], vt_ref[0, :, n_real:n_real + TAIL]], axis=1)
    n_tiles = n_real // tq
    kind = jnp.where(is_meta, 4, (t == 0).astype(jnp.int32) + 2 * (t == n_tiles - 1).astype(jnp.int32))
    bias = jnp.concatenate([bias_ref[kind]] * GQA_GROUP, axis=1)
    qt = qt_ref[0]
    n_groups = GQA_HEADS // GQA_GROUP
    gw = GQA_GROUP * tq
    z = jnp.zeros((HEAD64, gw), BF16)
    rows = []
    for g in range(n_groups):
        qrow = jnp.concatenate([qt[h * HEAD64:(h + 1) * HEAD64] for h in range(g * GQA_GROUP, (g + 1) * GQA_GROUP)], axis=1)
        rows.append(jnp.concatenate([qrow if i == g else z for i in range(n_groups)], axis=1))
    wq = jnp.concatenate(rows, axis=0)
    sink = jnp.concatenate([jnp.full((1, tq), sink_ref[h] * LOG2E, F32) for h in range(GQA_HEADS)], axis=1)
    s = jnp.dot(kwin, wq, preferred_element_type=F32) + jnp.concatenate([bias, bias], axis=1)
    m = jnp.maximum(jnp.max(s, axis=0, keepdims=True), sink)
    p = jnp.exp2(s - m).astype(BF16)
    extra = jnp.exp2(sink - m)
    for g in range(n_groups):
        acc = jnp.dot(vwin[g * GQA_VT_ROWS:(g + 1) * GQA_VT_ROWS], p[:, g * gw:(g + 1) * gw], preferred_element_type=F32)
        ot = acc[:HEAD64] * (1.0 / (acc[HEAD64:HEAD64 + 1] + extra[:, g * gw:(g + 1) * gw]))
        for i in range(GQA_GROUP // 2):
            pair = jnp.concatenate([ot[:, 2 * i * tq:(2 * i + 1) * tq], ot[:, (2 * i + 1) * tq:(2 * i + 2) * tq]], axis=0)
            c = g * (GQA_GROUP // 2) + i
            o_ref[0, :, c * LANES:(c + 1) * LANES] = pair.T.astype(o_ref.dtype)


def _window_bias(n_real, tq):
    span = tq + 2 * BLOCK
    ks = jnp.arange(span + TAIL, dtype=jnp.int32)[:, None]
    qi = jnp.arange(tq, dtype=jnp.int32)[None, :]
    in_span = ks < span
    tail_ok = (ks >= span) & (ks < span + N_META)
    n_tiles = n_real // tq
    tables = []
    for first, last in ((False, False), (True, False), (False, True), (True, True)):
        t = 0 if first else (n_tiles - 1 if last else 1)
        kpos = max(t * tq - BLOCK, 0) + ks
        dist = kpos - (t * tq + qi)
        tables.append((dist <= WINDOW) & (dist >= -WINDOW) & in_span & (kpos < n_real) | tail_ok)
    tables.append(in_span & (ks <= qi + (WINDOW - N_META)) | tail_ok)
    return jnp.where(jnp.stack(tables), 0.0, NEG).astype(F32)


def _window_attn_call(qt, k, vt, sink, n_real):
    B, Lp, _ = k.shape
    tq = 256 if n_real % 256 == 0 and n_real >= 384 else BLOCK
    assert n_real % tq == 0 and tq + 2 * BLOCK <= Lp
    bias = _window_bias(n_real, tq)
    grid_spec = pltpu.PrefetchScalarGridSpec(
        num_scalar_prefetch=1, grid=(B, n_real // tq + 1),
        in_specs=[pl.BlockSpec((1, GQA_HEADS * HEAD64, tq), lambda b, t, s: (b, 0, t)),
                  pl.BlockSpec((1, Lp, LANES), lambda b, t, s: (b, 0, 0)),
                  pl.BlockSpec((1, 2 * GQA_VT_ROWS, Lp), lambda b, t, s: (b, 0, 0)),
                  _resident(bias.shape)],
        out_specs=pl.BlockSpec((1, tq, 512), lambda b, t, s: (b, t, 0)))
    return pl.pallas_call(
        functools.partial(_window_attn_kernel, n_real=n_real),
        out_shape=jax.ShapeDtypeStruct((B, Lp, 512), BF16), grid_spec=grid_spec,
        compiler_params=_cparams(("parallel", "parallel")), name="window_attn",
    )(sink, qt, k, vt, bias)


def _post_kernel(*refs, has_tail):
    h_ref, tail_ref = (refs[0], refs[1]) if has_tail else (refs[0], None)
    oda_ref, om_ref, og_ref, ng_ref, wg_ref, wb_ref, wo_ref, mg_ref, wu_ref, wd_ref, out_ref = refs[2 if has_tail else 1:]
    x = _row_tile(h_ref, tail_ref)
    xn = ((x * _rms_scale(x, D_MODEL)) * ng_ref[...]).astype(BF16)
    gl = jnp.dot(xn, wg_ref[...], preferred_element_type=F32)
    merged = None
    for g, o_ref in enumerate((oda_ref, om_ref, og_ref)):
        proj = jnp.dot(o_ref[0], wb_ref[g], preferred_element_type=F32)
        term = jax.nn.sigmoid(gl[:, g * D_MODEL:(g + 1) * D_MODEL]) * proj
        merged = term if merged is None else merged + term
    h1 = x + jnp.dot(merged.astype(BF16), wo_ref[...], preferred_element_type=F32)
    h1n = ((h1 * _rms_scale(h1, D_MODEL)) * mg_ref[...]).astype(BF16)
    u = jnp.dot(h1n, wu_ref[...], preferred_element_type=F32)
    a = jnp.square(jnp.maximum(u, 0.0)).astype(BF16)
    out_ref[0] = h1 + jnp.dot(a, wd_ref[...], preferred_element_type=F32)


def _post_call(h, tail, oda, om, og, ng, wg, wb, wo, mg, wu, wd, out_rows):
    B, Lp, _ = oda.shape
    D = h.shape[2]
    tm = _pick(Lp, (POST_TM, 128))
    row = lambda b, j: (b, j, 0)
    stream = [h] if tail is None else [h, tail]
    in_specs = [_stream_spec(h, tm)] + ([] if tail is None else [_resident(tail.shape)]) + [
        pl.BlockSpec((1, tm, BRANCH_WIDTH), row)] * 3 + [_resident(a.shape) for a in (ng, wg, wb, wo, mg, wu, wd)]
    return pl.pallas_call(
        functools.partial(_post_kernel, has_tail=tail is not None),
        out_shape=jax.ShapeDtypeStruct((B, out_rows, D), F32), grid=(B, pl.cdiv(out_rows, tm)),
        in_specs=in_specs, out_specs=pl.BlockSpec((1, tm, D), row),
        compiler_params=_cparams(("parallel", "parallel")), name="post",
    )(*stream, oda, om, og, ng, wg, wb, wo, mg, wu, wd)


def _swap_halves(m):
    return jnp.flip(m.reshape(m.shape[:-1] + (m.shape[-1] // HEAD64, 2, HEAD64 // 2)), axis=-2).reshape(m.shape)


def _pair_gain(g):
    g = jnp.tile(g.astype(F32), 2)[None, :]
    return jnp.concatenate([g, _swap_halves(g)], axis=0)


def _pad_gain(g):
    return jnp.concatenate([g.astype(F32), jnp.zeros((LANES - g.shape[0],), F32)])[None, :]


def _layer_params(l, attn_norm_g, w_in, da_q_norm_g, da_k_norm_g, da_lam_q1, da_lam_k1, da_lam_q2, da_lam_k2,
                  da_subln_g, mla_cq_norm_g, mla_ckv_norm_g, mla_w_uq, mla_w_ukv, mla_q_norm_g, mla_k_norm_g,
                  gqa_q_norm_g, gqa_k_norm_g, gqa_sink, w_branch, w_out, mlp_norm_g, w_up, w_down):
    w = w_in[l]
    sec = [w[:, a:b] for a, b in zip(IN_OFF[:-1], IN_OFF[1:])]
    da_q, da_k, da_v, cq, ckv, kr, gq, gk, gv, gate = sec
    pad64 = lambda m: jnp.pad(m, ((0, 0),) * (m.ndim - 1) + ((0, LANES - HEAD64),))
    wa = jnp.concatenate([da_q, da_k, da_v, cq, ckv,
                          pad64(kr), pad64(_swap_halves(kr)), gq, _swap_halves(gq),
                          gk, _swap_halves(gk), gv], axis=1).astype(BF16)
    wuq = mla_w_uq[l].reshape(MLA_Q_RANK, MLA_HEADS, MLA_QK)
    wuq_r = wuq[:, :, MLA_NOPE:]
    wuq = jnp.concatenate([wuq[:, :, :MLA_NOPE], pad64(wuq_r), pad64(_swap_halves(wuq_r))], axis=-1)
    wuq = wuq.reshape(MLA_Q_RANK, MLA_HEADS * 3 * LANES).astype(BF16)
    wukv = mla_w_ukv[l].reshape(MLA_KV_RANK, MLA_HEADS, 2 * LANES)
    wuk = wukv[:, :, :MLA_NOPE].reshape(MLA_KV_RANK, MLA_HEADS * LANES).astype(BF16)
    wuv = wukv[:, :, MLA_NOPE:].reshape(MLA_KV_RANK, MLA_HEADS * LANES).astype(BF16)
    mq, mk = mla_q_norm_g[l], mla_k_norm_g[l]
    gt = jnp.concatenate([_pair_gain(da_q_norm_g[l]), _pair_gain(da_k_norm_g[l]),
                          _pair_gain(gqa_q_norm_g[l]), _pair_gain(gqa_k_norm_g[l]),
                          mq[None, :MLA_NOPE].astype(F32), _pad_gain(mq[MLA_NOPE:]), _pad_gain(_swap_halves(mq[MLA_NOPE:])),
                          mk[None, :MLA_NOPE].astype(F32), _pad_gain(mk[MLA_NOPE:]), _pad_gain(_swap_halves(mk[MLA_NOPE:])),
                          jnp.zeros((2, LANES), F32)], axis=0)
    lam_tab = jnp.concatenate([_pad_gain(v[l]) for v in (da_lam_q1, da_lam_k1, da_lam_q2, da_lam_k2)], axis=0)
    return dict(
        ng=attn_norm_g[l][None, :].astype(F32), wa=wa, wuq=wuq, wuk=wuk, wuv=wuv, gt=gt,
        gcq=mla_cq_norm_g[l][None, :].astype(F32), gckv=mla_ckv_norm_g[l][None, :].astype(F32),
        lam_tab=lam_tab, sub_g=da_subln_g[l][None, :].astype(F32), sink=gqa_sink[l].astype(F32),
        wg=gate.astype(BF16), wb=w_branch[l].astype(BF16), wo=w_out[l].astype(BF16),
        mg=mlp_norm_g[l][None, :].astype(F32), wu=w_up[l].astype(BF16), wd=w_down[l].astype(BF16))


def _rope_tables(n_real):
    r = jnp.arange(n_real + TAIL, dtype=jnp.int32)
    pos = jnp.where(r < n_real, r + N_META, jnp.where(r < n_real + N_META, r - n_real, 0)).astype(F32)
    inv_freq = 1.0 / (ROPE_THETA ** (jnp.arange(0, HEAD64, 2, dtype=F32) / HEAD64))
    ang = pos[:, None] * inv_freq[None, :]
    cos, sin = jnp.cos(ang), jnp.sin(ang)
    return jnp.tile(cos, (1, 4)), jnp.tile(jnp.concatenate([-sin, sin], axis=1), (1, 2))


def _trunk(x, meta_tokens, layers):
    B, S, D = x.shape
    assert D == D_MODEL and S % BLOCK == 0
    tail = jnp.concatenate([meta_tokens.astype(x.dtype), jnp.zeros((TAIL - N_META, D), x.dtype)], axis=0)
    h = x
    cos, sin = _rope_tables(S)
    for l, p in enumerate(layers):
        lam_init = 0.8 - 0.6 * math.exp(-0.3 * l)
        qtda, kda, vtda, qtm, km, vtm, qtg, kg, vtg = _prep_call(
            h, tail, cos, sin, p["ng"], p["wa"], p["wuq"], p["wuk"], p["wuv"], p["gt"], p["gcq"], p["gckv"])
        oda = _dense_attn_call(qtda, kda, vtda, S, differential=True, lam_tab=p["lam_tab"], sub_g=p["sub_g"],
                               lam_init=lam_init, name="diff_attn")
        om = _dense_attn_call(qtm, km, vtm, S, differential=False, name="latent_attn")
        og = _window_attn_call(qtg, kg, vtg, p["sink"], S)
        h = _post_call(h, tail, oda, om, og, p["ng"], p["wg"], p["wb"], p["wo"], p["mg"], p["wu"], p["wd"],
                       out_rows=S if l == len(layers) - 1 else S + TAIL)
        tail = None
    return h


def kernel(x_prompt, x_sample, meta_tokens, attn_norm_g, w_in, da_q_norm_g, da_k_norm_g, da_lam_q1, da_lam_k1, da_lam_q2, da_lam_k2, da_subln_g, mla_cq_norm_g, mla_ckv_norm_g, mla_w_uq, mla_w_ukv, mla_q_norm_g, mla_k_norm_g, gqa_q_norm_g, gqa_k_norm_g, gqa_sink, w_branch, w_out, mlp_norm_g, w_up, w_down):
    depth = w_in.shape[0]
    layers = [_layer_params(l, attn_norm_g, w_in, da_q_norm_g, da_k_norm_g, da_lam_q1, da_lam_k1, da_lam_q2,
                            da_lam_k2, da_subln_g, mla_cq_norm_g, mla_ckv_norm_g, mla_w_uq, mla_w_ukv,
                            mla_q_norm_g, mla_k_norm_g, gqa_q_norm_g, gqa_k_norm_g, gqa_sink, w_branch, w_out,
                            mlp_norm_g, w_up, w_down) for l in range(depth)]
    return (_trunk(x_prompt, meta_tokens, layers), _trunk(x_sample, meta_tokens, layers))
```

```python
import functools
import math

import jax
import jax.numpy as jnp
from jax import lax
from jax.experimental import pallas as pl
from jax.experimental.pallas import tpu as pltpu

F32 = jnp.float32
BF16 = jnp.bfloat16

D_MODEL = 1024
N_META = 16
BLOCK = 128
WINDOW = 128
ROPE_THETA = 10000.0
EPS = 1e-6
NEG = -1e30
LOG2E = 1.4426950408889634

HEAD64 = 64
MLA_HEADS = 4
MLA_NOPE = 128
MLA_QK = 192
MLA_Q_RANK = 384
MLA_KV_RANK = 256
GQA_HEADS = 8
GQA_GROUP = 4
BRANCH_WIDTH = 512
LANES = 128
TAIL = 128
ONES_ROWS = 16
VT_ROWS = LANES + ONES_ROWS
GQA_VT_ROWS = HEAD64 + ONES_ROWS
CHUNK_GROUP = 8
POST_TM = 384
LATENT_TQ = 768

A_DAQ, A_DAK, A_DAV, A_CQ, A_CKV, A_KR, A_KRS, A_GQ, A_GQS, A_GK, A_GKS, A_GV, A_END = (
    0, 512, 1024, 1536, 1920, 2176, 2304, 2432, 2944, 3456, 3584, 3712, 3840)
IN_OFF = (0, 512, 1024, 1536, 1920, 2176, 2240, 2752, 2880, 3008, 6080)

VMEM_LIMIT = 56 * 1024 * 1024
ATTN_VMEM_BUDGET = 44 * 1024 * 1024


def _pick(n, candidates):
    for c in candidates:
        if n % c == 0:
            return c
    raise ValueError(f"no tile in {candidates} divides {n}")


def _cparams(sem):
    return pltpu.CompilerParams(dimension_semantics=sem, vmem_limit_bytes=VMEM_LIMIT)


def _resident(shape):
    zeros = (0,) * len(shape)
    return pl.BlockSpec(shape, lambda *_: zeros, pipeline_mode=pl.Buffered(1))


def _rms_scale(x, width):
    return lax.rsqrt(jnp.sum(x * x, axis=-1, keepdims=True) * (1.0 / width) + EPS)


def _pair_rms_scale(xc, lo_mask):
    sq = xc * xc
    lo = jnp.sum(jnp.where(lo_mask, sq, 0.0), axis=-1, keepdims=True)
    hi = jnp.sum(jnp.where(lo_mask, 0.0, sq), axis=-1, keepdims=True)
    return jnp.where(lo_mask, lax.rsqrt(lo * (1.0 / HEAD64) + EPS), lax.rsqrt(hi * (1.0 / HEAD64) + EPS))


def _rope(xc, swapped, cos, sin_signed):
    return xc * cos + swapped * sin_signed


def _stream_spec(h, tm):
    last = (h.shape[1] - 1) // tm
    return pl.BlockSpec((1, tm, h.shape[2]), lambda b, j: (b, jnp.minimum(j, last), 0))


def _row_tile(h_ref, tail_ref):
    x = h_ref[0]
    if tail_ref is None:
        return x
    tm = x.shape[0]
    tail = jnp.concatenate([jnp.zeros((tm - TAIL, x.shape[1]), x.dtype), tail_ref[...]], axis=0)
    in_tail = lax.broadcasted_iota(jnp.int32, x.shape, 0) >= tm - TAIL
    is_last = pl.program_id(1) == pl.num_programs(1) - 1
    return jnp.where(jnp.logical_and(is_last, in_tail), tail, x)


def _prep_kernel(*refs, has_tail):
    h_ref, tail_ref = (refs[0], refs[1]) if has_tail else (refs[0], None)
    (cos_ref, sin_ref, ng_ref, wa_ref, wuq_ref, wuk_ref, wuv_ref, gt_ref, gcq_ref, gckv_ref,
     qtda_ref, kda_ref, vtda_ref, qtm_ref, km_ref, vtm_ref, qtg_ref, kg_ref, vtg_ref) = refs[2 if has_tail else 1:]
    x = _row_tile(h_ref, tail_ref)
    xn = ((x * _rms_scale(x, D_MODEL)) * ng_ref[...]).astype(BF16)
    xp = jnp.dot(xn, wa_ref[...], preferred_element_type=F32)
    cos = cos_ref[...]
    sin_s = sin_ref[...]
    lane = lax.broadcasted_iota(jnp.int32, cos.shape, 1)
    lo64 = lane < HEAD64
    s64 = (HEAD64 ** -0.5) * LOG2E
    s192 = (MLA_QK ** -0.5) * LOG2E

    first_half = (lane & 32) == 0

    def head_pair(col, col_swapped, gains):
        xc = xp[:, col:col + LANES]
        r = _pair_rms_scale(xc, lo64)
        y = (xc * r) * gains[0:1, :]
        if col_swapped is None:
            ys = jnp.where(first_half, pltpu.roll(y, LANES - 32, 1), pltpu.roll(y, 32, 1))
        else:
            ys = (xp[:, col_swapped:col_swapped + LANES] * r) * gains[1:2, :]
        return _rope(y, ys, cos, sin_s)

    ones_rows = jnp.ones((ONES_ROWS, x.shape[0]), BF16)

    def store_vt(vt_ref, hd, v_slab):
        vt_ref[0, hd * VT_ROWS:hd * VT_ROWS + LANES, :] = v_slab.T.astype(BF16)
        vt_ref[0, hd * VT_ROWS + LANES:(hd + 1) * VT_ROWS, :] = ones_rows

    g_q, g_k = gt_ref[0:2, :], gt_ref[2:4, :]
    for c in range(4):
        sl = slice(c * LANES, (c + 1) * LANES)
        qtda_ref[0, sl, :] = (head_pair(A_DAQ + c * LANES, None, g_q) * s64).T.astype(BF16)
        kda_ref[0, c] = head_pair(A_DAK + c * LANES, None, g_k).astype(BF16)
        store_vt(vtda_ref, c, xp[:, A_DAV + c * LANES:A_DAV + (c + 1) * LANES])

    cq = xp[:, A_CQ:A_CQ + MLA_Q_RANK]
    cqn = ((cq * _rms_scale(cq, MLA_Q_RANK)) * gcq_ref[...]).astype(BF16)
    qm = jnp.dot(cqn, wuq_ref[...], preferred_element_type=F32)
    ckv = xp[:, A_CKV:A_CKV + MLA_KV_RANK]
    ckvn = ((ckv * _rms_scale(ckv, MLA_KV_RANK)) * gckv_ref[...]).astype(BF16)
    kn_all = jnp.dot(ckvn, wuk_ref[...], preferred_element_type=F32)
    vm_all = jnp.dot(ckvn, wuv_ref[...], preferred_element_type=F32)
    kr = xp[:, A_KR:A_KR + LANES]
    krs = xp[:, A_KRS:A_KRS + LANES]
    kr_ss = jnp.sum(kr * kr, axis=-1, keepdims=True)
    gq_n, gq_r, gq_rs = gt_ref[8:9, :], gt_ref[9:10, :], gt_ref[10:11, :]
    gk_n, gk_r, gk_rs = gt_ref[11:12, :], gt_ref[12:13, :], gt_ref[13:14, :]
    for hd in range(MLA_HEADS):
        b = hd * 2 * LANES
        qb = hd * 3 * LANES
        qn = qm[:, qb:qb + LANES]
        qr = qm[:, qb + LANES:qb + 2 * LANES]
        qrs = qm[:, qb + 2 * LANES:qb + 3 * LANES]
        r = lax.rsqrt((jnp.sum(qn * qn, axis=-1, keepdims=True) + jnp.sum(qr * qr, axis=-1, keepdims=True))
                      * (1.0 / MLA_QK) + EPS)
        qtm_ref[0, b:b + LANES, :] = (((qn * r) * gq_n) * s192).T.astype(BF16)
        qtm_ref[0, b + LANES:b + 2 * LANES, :] = (
            _rope((qr * r) * gq_r, (qrs * r) * gq_rs, cos, sin_s) * s192).T.astype(BF16)
        kn = kn_all[:, hd * LANES:(hd + 1) * LANES]
        rk = lax.rsqrt((jnp.sum(kn * kn, axis=-1, keepdims=True) + kr_ss) * (1.0 / MLA_QK) + EPS)
        km_ref[0, hd, :, :LANES] = ((kn * rk) * gk_n).astype(BF16)
        km_ref[0, hd, :, LANES:] = _rope((kr * rk) * gk_r, (krs * rk) * gk_rs, cos, sin_s).astype(BF16)
        store_vt(vtm_ref, hd, vm_all[:, hd * LANES:(hd + 1) * LANES])

    g_q, g_k = gt_ref[4:6, :], gt_ref[6:8, :]
    for c in range(4):
        sl = slice(c * LANES, (c + 1) * LANES)
        qtg_ref[0, sl, :] = (head_pair(A_GQ + c * LANES, A_GQS + c * LANES, g_q) * s64).T.astype(BF16)
    kg_ref[0] = head_pair(A_GK, A_GKS, g_k).astype(BF16)
    vgt = xp[:, A_GV:A_GV + LANES].T.astype(BF16)
    for g in range(GQA_HEADS // GQA_GROUP):
        vtg_ref[0, g * GQA_VT_ROWS:g * GQA_VT_ROWS + HEAD64, :] = vgt[g * HEAD64:(g + 1) * HEAD64]
        vtg_ref[0, g * GQA_VT_ROWS + HEAD64:(g + 1) * GQA_VT_ROWS, :] = ones_rows


def _prep_call(h, tail, cos, sin, ng, wa, wuq, wuk, wuv, gt, gcq, gckv):
    B, _, D = h.shape
    Lp = cos.shape[0]
    tm = _pick(Lp, (384, 128))
    row = lambda b, j: (b, j, 0)
    colt = lambda b, j: (b, 0, j)
    tab = pl.BlockSpec((tm, LANES), lambda b, j: (j, 0))
    sds = lambda *s: jax.ShapeDtypeStruct(s, BF16)
    vt_rows = 4 * VT_ROWS
    out_shape = [sds(B, 512, Lp), sds(B, 4, Lp, LANES), sds(B, vt_rows, Lp),
                 sds(B, 1024, Lp), sds(B, MLA_HEADS, Lp, 2 * LANES), sds(B, vt_rows, Lp),
                 sds(B, 512, Lp), sds(B, Lp, LANES), sds(B, 2 * GQA_VT_ROWS, Lp)]
    hrow = lambda b, j: (b, 0, j, 0)
    out_specs = [pl.BlockSpec((1, 512, tm), colt), pl.BlockSpec((1, 4, tm, LANES), hrow), pl.BlockSpec((1, vt_rows, tm), colt),
                 pl.BlockSpec((1, 1024, tm), colt), pl.BlockSpec((1, MLA_HEADS, tm, 2 * LANES), hrow),
                 pl.BlockSpec((1, vt_rows, tm), colt),
                 pl.BlockSpec((1, 512, tm), colt), pl.BlockSpec((1, tm, LANES), row),
                 pl.BlockSpec((1, 2 * GQA_VT_ROWS, tm), colt)]
    stream = [h] if tail is None else [h, tail]
    in_specs = [_stream_spec(h, tm)] + ([] if tail is None else [_resident(tail.shape)]) + [
        tab, tab, _resident(ng.shape), _resident(wa.shape), _resident(wuq.shape), _resident(wuk.shape),
        _resident(wuv.shape), _resident(gt.shape), _resident(gcq.shape), _resident(gckv.shape)]
    return pl.pallas_call(
        functools.partial(_prep_kernel, has_tail=tail is not None), out_shape=out_shape, grid=(B, Lp // tm),
        in_specs=in_specs, out_specs=out_specs,
        compiler_params=_cparams(("parallel", "parallel")), name="prep",
    )(*stream, cos, sin, ng, wa, wuq, wuk, wuv, gt, gcq, gckv)


def _dense_attn_kernel(*refs, n_real, tk, differential, lam_init, heads_per_step, tiles_per_step):
    if differential:
        qt_ref, k_ref, vt_ref, lam_ref, sg_ref, o_ref = refs[:6]
    else:
        (qt_ref, k_ref, vt_ref, o_ref), lam_ref, sg_ref = refs[:4], None, None
    s_refs = refs[len(refs) - heads_per_step * tiles_per_step:]
    dq = qt_ref.shape[1] // heads_per_step
    tq = qt_ref.shape[2] // tiles_per_step
    for hh in range(heads_per_step):
        for u in range(tiles_per_step):
            _attend_head(qt_ref.at[0, hh * dq:(hh + 1) * dq, u * tq:(u + 1) * tq], k_ref.at[0, hh],
                         vt_ref.at[0, hh * VT_ROWS:(hh + 1) * VT_ROWS], lam_ref, sg_ref,
                         o_ref.at[0, u * tq:(u + 1) * tq, hh * LANES:(hh + 1) * LANES], s_refs[hh * tiles_per_step + u],
                         n_real=n_real, tk=tk, differential=differential, lam_init=lam_init)


def _attend_head(qt_ref, k_ref, vt_ref, lam_ref, sg_ref, o_ref, s_ref, *, n_real, tk, differential, lam_init):
    tq = qt_ref.shape[1]
    dv = o_ref.shape[1]
    qt = qt_ref[...]
    if differential:
        z = jnp.zeros((HEAD64, tq), BF16)
        wq = jnp.concatenate([jnp.concatenate([qt[:HEAD64], z], axis=1),
                              jnp.concatenate([z, qt[HEAD64:]], axis=1)], axis=0)
    else:
        wq = qt
    width = wq.shape[1]

    def scores(kc):
        return jnp.dot(kc, wq, preferred_element_type=F32)

    def absorb(m, acc, s, vtc):
        m_new = jnp.maximum(m, jnp.max(s, axis=0, keepdims=True))
        p = jnp.exp2(s - m_new).astype(BF16)
        alpha = jnp.exp2(m - m_new)
        return m_new, alpha * acc + jnp.dot(vtc, p, preferred_element_type=F32)

    def kchunk(c):
        return k_ref[pl.ds(pl.multiple_of(c * tk, tk), tk), :]

    def vchunk(c):
        return vt_ref[:, pl.ds(pl.multiple_of(c * tk, tk), tk)]

    def group(j, carry):
        m, acc = carry
        for g in range(CHUNK_GROUP):
            c = CHUNK_GROUP * j + g
            s_ref[(g + 1) % 2] = scores(kchunk(c + 1))
            m, acc = absorb(m, acc, s_ref[g % 2], vchunk(c))
        return m, acc

    n_chunks = n_real // tk
    n_groups = (n_chunks - 1) // CHUNK_GROUP
    s_ref[0] = scores(kchunk(0))
    carry = (jnp.full((1, width), NEG, F32), jnp.zeros((dv + ONES_ROWS, width), F32))
    m, acc = lax.fori_loop(0, n_groups, group, carry)
    valid = lax.broadcasted_iota(jnp.int32, (TAIL, width), 0) < N_META
    first = CHUNK_GROUP * n_groups
    for c in range(first, n_chunks):
        if c + 1 < n_chunks:
            s_ref[(c - first + 1) % 2] = scores(kchunk(c + 1))
        else:
            s_tail = jnp.where(valid, scores(k_ref[n_real:n_real + TAIL, :]), NEG)
        m, acc = absorb(m, acc, s_ref[(c - first) % 2], vchunk(c))
    _, acc = absorb(m, acc, s_tail, vt_ref[:, n_real:n_real + TAIL])

    ot = acc[:dv] * (1.0 / acc[dv:dv + 1])
    if differential:
        lt = lam_ref[...]
        lam = (jnp.exp(jnp.sum(lt[0:1, :] * lt[1:2, :], axis=-1, keepdims=True))
               - jnp.exp(jnp.sum(lt[2:3, :] * lt[3:4, :], axis=-1, keepdims=True)) + lam_init)
        o = (ot[:, :tq] - lam * ot[:, tq:]).T
        o = ((o * _rms_scale(o, dv)) * sg_ref[...]) * (1.0 - lam_init)
    else:
        o = ot.T
    o_ref[...] = o.astype(o_ref.dtype)


def _dense_attn_call(qt, k, vt, n_real, *, differential, lam_tab=None, sub_g=None, lam_init=0.0, name):
    B, heads, Lp, dq = k.shape
    if differential:
        tq = _pick(Lp, (384, 128))
    else:
        tq = LATENT_TQ if Lp >= LATENT_TQ else _pick(n_real, (512, 256, 128))
    tk = _pick(n_real, (512, 256, 128))
    width = (2 if differential else 1) * tq
    per_head = 2 * 2 * Lp * (dq + VT_ROWS) + 4 * 2 * tk * width
    hps = max(n for n in (4, 2, 1) if heads % n == 0 and (n == 1 or n * per_head <= ATTN_VMEM_BUDGET))
    tps = 4 // hps
    in_specs = [pl.BlockSpec((1, hps * dq, tps * tq), lambda b, h, i: (b, h, i)),
                pl.BlockSpec((1, hps, Lp, dq), lambda b, h, i: (b, h, 0, 0)),
                pl.BlockSpec((1, hps * VT_ROWS, Lp), lambda b, h, i: (b, h, 0))]
    args = [qt, k, vt]
    if differential:
        in_specs += [_resident(lam_tab.shape), _resident(sub_g.shape)]
        args += [lam_tab, sub_g]
    kern = functools.partial(_dense_attn_kernel, n_real=n_real, tk=tk, differential=differential, lam_init=lam_init,
                             heads_per_step=hps, tiles_per_step=tps)
    return pl.pallas_call(
        kern, out_shape=jax.ShapeDtypeStruct((B, Lp, heads * LANES), BF16),
        grid=(B, heads // hps, pl.cdiv(Lp, tps * tq)),
        in_specs=in_specs, out_specs=pl.BlockSpec((1, tps * tq, hps * LANES), lambda b, h, i: (b, i, h)),
        scratch_shapes=[pltpu.VMEM((2, tk, width), F32)] * (hps * tps),
        compiler_params=_cparams(("parallel", "parallel", "parallel")), name=name,
    )(*args)


def _window_attn_kernel(sink_ref, qt_ref, k_ref, vt_ref, bias_ref, o_ref, *, n_real):
    tq = qt_ref.shape[2]
    t = pl.program_id(1)
    is_meta = t == n_real // tq
    span = tq + 2 * BLOCK
    start = pl.multiple_of(jnp.where(is_meta, 0, jnp.maximum(t * tq - BLOCK, 0)), BLOCK)
    kwin = jnp.concatenate([k_ref[0, pl.ds(start, span), :], k_ref[0, n_real:n_real + TAIL, :]], axis=0)
    vwin = jnp.concatenate([vt_ref[0, :, pl.ds(start, span)], vt_ref[0, :, n_real:n_real + TAIL]], axis=1)
    n_tiles = n_real // tq
    kind = jnp.where(is_meta, 4, (t == 0).astype(jnp.int32) + 2 * (t == n_tiles - 1).astype(jnp.int32))
    bias = jnp.concatenate([bias_ref[kind]] * GQA_GROUP, axis=1)
    qt = qt_ref[0]
    n_groups = GQA_HEADS // GQA_GROUP
    gw = GQA_GROUP * tq
    z = jnp.zeros((HEAD64, gw), BF16)
    rows = []
    for g in range(n_groups):
        qrow = jnp.concatenate([qt[h * HEAD64:(h + 1) * HEAD64] for h in range(g * GQA_GROUP, (g + 1) * GQA_GROUP)], axis=1)
        rows.append(jnp.concatenate([qrow if i == g else z for i in range(n_groups)], axis=1))
    wq = jnp.concatenate(rows, axis=0)
    sink = jnp.concatenate([jnp.full((1, tq), sink_ref[h] * LOG2E, F32) for h in range(GQA_HEADS)], axis=1)
    s = jnp.dot(kwin, wq, preferred_element_type=F32) + jnp.concatenate([bias, bias], axis=1)
    m = jnp.maximum(jnp.max(s, axis=0, keepdims=True), sink)
    p = jnp.exp2(s - m).astype(BF16)
    extra = jnp.exp2(sink - m)
    for g in range(n_groups):
        acc = jnp.dot(vwin[g * GQA_VT_ROWS:(g + 1) * GQA_VT_ROWS], p[:, g * gw:(g + 1) * gw], preferred_element_type=F32)
        ot = acc[:HEAD64] * (1.0 / (acc[HEAD64:HEAD64 + 1] + extra[:, g * gw:(g + 1) * gw]))
        for i in range(GQA_GROUP // 2):
            pair = jnp.concatenate([ot[:, 2 * i * tq:(2 * i + 1) * tq], ot[:, (2 * i + 1) * tq:(2 * i + 2) * tq]], axis=0)
            c = g * (GQA_GROUP // 2) + i
            o_ref[0, :, c * LANES:(c + 1) * LANES] = pair.T.astype(o_ref.dtype)


def _window_bias(n_real, tq):
    span = tq + 2 * BLOCK
    ks = jnp.arange(span + TAIL, dtype=jnp.int32)[:, None]
    qi = jnp.arange(tq, dtype=jnp.int32)[None, :]
    in_span = ks < span
    tail_ok = (ks >= span) & (ks < span + N_META)
    n_tiles = n_real // tq
    tables = []
    for first, last in ((False, False), (True, False), (False, True), (True, True)):
        t = 0 if first else (n_tiles - 1 if last else 1)
        kpos = max(t * tq - BLOCK, 0) + ks
        dist = kpos - (t * tq + qi)
        tables.append((dist <= WINDOW) & (dist >= -WINDOW) & in_span & (kpos < n_real) | tail_ok)
    tables.append(in_span & (ks <= qi + (WINDOW - N_META)) | tail_ok)
    return jnp.where(jnp.stack(tables), 0.0, NEG).astype(F32)


def _window_attn_call(qt, k, vt, sink, n_real):
    B, Lp, _ = k.shape
    tq = 256 if n_real % 256 == 0 and n_real >= 384 else BLOCK
    assert n_real % tq == 0 and tq + 2 * BLOCK <= Lp
    bias = _window_bias(n_real, tq)
    grid_spec = pltpu.PrefetchScalarGridSpec(
        num_scalar_prefetch=1, grid=(B, n_real // tq + 1),
        in_specs=[pl.BlockSpec((1, GQA_HEADS * HEAD64, tq), lambda b, t, s: (b, 0, t)),
                  pl.BlockSpec((1, Lp, LANES), lambda b, t, s: (b, 0, 0)),
                  pl.BlockSpec((1, 2 * GQA_VT_ROWS, Lp), lambda b, t, s: (b, 0, 0)),
                  _resident(bias.shape)],
        out_specs=pl.BlockSpec((1, tq, 512), lambda b, t, s: (b, t, 0)))
    return pl.pallas_call(
        functools.partial(_window_attn_kernel, n_real=n_real),
        out_shape=jax.ShapeDtypeStruct((B, Lp, 512), BF16), grid_spec=grid_spec,
        compiler_params=_cparams(("parallel", "parallel")), name="window_attn",
    )(sink, qt, k, vt, bias)


def _post_kernel(*refs, has_tail):
    h_ref, tail_ref = (refs[0], refs[1]) if has_tail else (refs[0], None)
    oda_ref, om_ref, og_ref, ng_ref, wg_ref, wb_ref, wo_ref, mg_ref, wu_ref, wd_ref, out_ref = refs[2 if has_tail else 1:]
    x = _row_tile(h_ref, tail_ref)
    xn = ((x * _rms_scale(x, D_MODEL)) * ng_ref[...]).astype(BF16)
    gl = jnp.dot(xn, wg_ref[...], preferred_element_type=F32)
    merged = None
    for g, o_ref in enumerate((oda_ref, om_ref, og_ref)):
        proj = jnp.dot(o_ref[0], wb_ref[g], preferred_element_type=F32)
        term = jax.nn.sigmoid(gl[:, g * D_MODEL:(g + 1) * D_MODEL]) * proj
        merged = term if merged is None else merged + term
    h1 = x + jnp.dot(merged.astype(BF16), wo_ref[...], preferred_element_type=F32)
    h1n = ((h1 * _rms_scale(h1, D_MODEL)) * mg_ref[...]).astype(BF16)
    u = jnp.dot(h1n, wu_ref[...], preferred_element_type=F32)
    a = jnp.square(jnp.maximum(u, 0.0)).astype(BF16)
    out_ref[0] = h1 + jnp.dot(a, wd_ref[...], preferred_element_type=F32)


def _post_call(h, tail, oda, om, og, ng, wg, wb, wo, mg, wu, wd, out_rows):
    B, Lp, _ = oda.shape
    D = h.shape[2]
    tm = _pick(Lp, (POST_TM, 128))
    row = lambda b, j: (b, j, 0)
    stream = [h] if tail is None else [h, tail]
    in_specs = [_stream_spec(h, tm)] + ([] if tail is None else [_resident(tail.shape)]) + [
        pl.BlockSpec((1, tm, BRANCH_WIDTH), row)] * 3 + [_resident(a.shape) for a in (ng, wg, wb, wo, mg, wu, wd)]
    return pl.pallas_call(
        functools.partial(_post_kernel, has_tail=tail is not None),
        out_shape=jax.ShapeDtypeStruct((B, out_rows, D), F32), grid=(B, pl.cdiv(out_rows, tm)),
        in_specs=in_specs, out_specs=pl.BlockSpec((1, tm, D), row),
        compiler_params=_cparams(("parallel", "parallel")), name="post",
    )(*stream, oda, om, og, ng, wg, wb, wo, mg, wu, wd)


def _swap_halves(m):
    return jnp.flip(m.reshape(m.shape[:-1] + (m.shape[-1] // HEAD64, 2, HEAD64 // 2)), axis=-2).reshape(m.shape)


def _pair_gain(g):
    g = jnp.tile(g.astype(F32), 2)[None, :]
    return jnp.concatenate([g, _swap_halves(g)], axis=0)


def _pad_gain(g):
    return jnp.concatenate([g.astype(F32), jnp.zeros((LANES - g.shape[0],), F32)])[None, :]


def _layer_params(l, attn_norm_g, w_in, da_q_norm_g, da_k_norm_g, da_lam_q1, da_lam_k1, da_lam_q2, da_lam_k2,
                  da_subln_g, mla_cq_norm_g, mla_ckv_norm_g, mla_w_uq, mla_w_ukv, mla_q_norm_g, mla_k_norm_g,
                  gqa_q_norm_g, gqa_k_norm_g, gqa_sink, w_branch, w_out, mlp_norm_g, w_up, w_down):
    w = w_in[l]
    sec = [w[:, a:b] for a, b in zip(IN_OFF[:-1], IN_OFF[1:])]
    da_q, da_k, da_v, cq, ckv, kr, gq, gk, gv, gate = sec
    pad64 = lambda m: jnp.pad(m, ((0, 0),) * (m.ndim - 1) + ((0, LANES - HEAD64),))
    wa = jnp.concatenate([da_q, da_k, da_v, cq, ckv,
                          pad64(kr), pad64(_swap_halves(kr)), gq, _swap_halves(gq),
                          gk, _swap_halves(gk), gv], axis=1).astype(BF16)
    wuq = mla_w_uq[l].reshape(MLA_Q_RANK, MLA_HEADS, MLA_QK)
    wuq_r = wuq[:, :, MLA_NOPE:]
    wuq = jnp.concatenate([wuq[:, :, :MLA_NOPE], pad64(wuq_r), pad64(_swap_halves(wuq_r))], axis=-1)
    wuq = wuq.reshape(MLA_Q_RANK, MLA_HEADS * 3 * LANES).astype(BF16)
    wukv = mla_w_ukv[l].reshape(MLA_KV_RANK, MLA_HEADS, 2 * LANES)
    wuk = wukv[:, :, :MLA_NOPE].reshape(MLA_KV_RANK, MLA_HEADS * LANES).astype(BF16)
    wuv = wukv[:, :, MLA_NOPE:].reshape(MLA_KV_RANK, MLA_HEADS * LANES).astype(BF16)
    mq, mk = mla_q_norm_g[l], mla_k_norm_g[l]
    gt = jnp.concatenate([_pair_gain(da_q_norm_g[l]), _pair_gain(da_k_norm_g[l]),
                          _pair_gain(gqa_q_norm_g[l]), _pair_gain(gqa_k_norm_g[l]),
                          mq[None, :MLA_NOPE].astype(F32), _pad_gain(mq[MLA_NOPE:]), _pad_gain(_swap_halves(mq[MLA_NOPE:])),
                          mk[None, :MLA_NOPE].astype(F32), _pad_gain(mk[MLA_NOPE:]), _pad_gain(_swap_halves(mk[MLA_NOPE:])),
                          jnp.zeros((2, LANES), F32)], axis=0)
    lam_tab = jnp.concatenate([_pad_gain(v[l]) for v in (da_lam_q1, da_lam_k1, da_lam_q2, da_lam_k2)], axis=0)
    return dict(
        ng=attn_norm_g[l][None, :].astype(F32), wa=wa, wuq=wuq, wuk=wuk, wuv=wuv, gt=gt,
        gcq=mla_cq_norm_g[l][None, :].astype(F32), gckv=mla_ckv_norm_g[l][None, :].astype(F32),
        lam_tab=lam_tab, sub_g=da_subln_g[l][None, :].astype(F32), sink=gqa_sink[l].astype(F32),
        wg=gate.astype(BF16), wb=w_branch[l].astype(BF16), wo=w_out[l].astype(BF16),
        mg=mlp_norm_g[l][None, :].astype(F32), wu=w_up[l].astype(BF16), wd=w_down[l].astype(BF16))


def _rope_tables(n_real):
    r = jnp.arange(n_real + TAIL, dtype=jnp.int32)
    pos = jnp.where(r < n_real, r + N_META, jnp.where(r < n_real + N_META, r - n_real, 0)).astype(F32)
    inv_freq = 1.0 / (ROPE_THETA ** (jnp.arange(0, HEAD64, 2, dtype=F32) / HEAD64))
    ang = pos[:, None] * inv_freq[None, :]
    cos, sin = jnp.cos(ang), jnp.sin(ang)
    return jnp.tile(cos, (1, 4)), jnp.tile(jnp.concatenate([-sin, sin], axis=1), (1, 2))


def _trunk(x, meta_tokens, layers):
    B, S, D = x.shape
    assert D == D_MODEL and S % BLOCK == 0
    tail = jnp.concatenate([meta_tokens.astype(x.dtype), jnp.zeros((TAIL - N_META, D), x.dtype)], axis=0)
    h = x
    cos, sin = _rope_tables(S)
    for l, p in enumerate(layers):
        lam_init = 0.8 - 0.6 * math.exp(-0.3 * l)
        qtda, kda, vtda, qtm, km, vtm, qtg, kg, vtg = _prep_call(
            h, tail, cos, sin, p["ng"], p["wa"], p["wuq"], p["wuk"], p["wuv"], p["gt"], p["gcq"], p["gckv"])
        oda = _dense_attn_call(qtda, kda, vtda, S, differential=True, lam_tab=p["lam_tab"], sub_g=p["sub_g"],
                               lam_init=lam_init, name="diff_attn")
        om = _dense_attn_call(qtm, km, vtm, S, differential=False, name="latent_attn")
        og = _window_attn_call(qtg, kg, vtg, p["sink"], S)
        h = _post_call(h, tail, oda, om, og, p["ng"], p["wg"], p["wb"], p["wo"], p["mg"], p["wu"], p["wd"],
                       out_rows=S if l == len(layers) - 1 else S + TAIL)
        tail = None
    return h


def kernel(x_prompt, x_sample, meta_tokens, attn_norm_g, w_in, da_q_norm_g, da_k_norm_g, da_lam_q1, da_lam_k1, da_lam_q2, da_lam_k2, da_subln_g, mla_cq_norm_g, mla_ckv_norm_g, mla_w_uq, mla_w_ukv, mla_q_norm_g, mla_k_norm_g, gqa_q_norm_g, gqa_k_norm_g, gqa_sink, w_branch, w_out, mlp_norm_g, w_up, w_down):
    depth = w_in.shape[0]
    layers = [_layer_params(l, attn_norm_g, w_in, da_q_norm_g, da_k_norm_g, da_lam_q1, da_lam_k1, da_lam_q2,
                            da_lam_k2, da_subln_g, mla_cq_norm_g, mla_ckv_norm_g, mla_w_uq, mla_w_ukv,
                            mla_q_norm_g, mla_k_norm_g, gqa_q_norm_g, gqa_k_norm_g, gqa_sink, w_branch, w_out,
                            mlp_norm_g, w_up, w_down) for l in range(depth)]
    return (_trunk(x_prompt, meta_tokens, layers), _trunk(x_sample, meta_tokens, layers))
```

```python
import functools
import math

import jax
import jax.numpy as jnp
from jax import lax
from jax.experimental import pallas as pl
from jax.experimental.pallas import tpu as pltpu

F32 = jnp.float32
BF16 = jnp.bfloat16

D_MODEL = 1024
N_META = 16
BLOCK = 128
WINDOW = 128
ROPE_THETA = 10000.0
EPS = 1e-6
NEG = -1e30
LOG2E = 1.4426950408889634

HEAD64 = 64
MLA_HEADS = 4
MLA_NOPE = 128
MLA_QK = 192
MLA_Q_RANK = 384
MLA_KV_RANK = 256
GQA_HEADS = 8
GQA_GROUP = 4
BRANCH_WIDTH = 512
LANES = 128
TAIL = 128
ONES_ROWS = 16
VT_ROWS = LANES + ONES_ROWS
GQA_VT_ROWS = HEAD64 + ONES_ROWS
CHUNK_GROUP = 8
POST_TM = 384
LATENT_TQ = 768

A_DAQ, A_DAK, A_DAV, A_CQ, A_CKV, A_KR, A_KRS, A_GQ, A_GQS, A_GK, A_GKS, A_GV, A_END = (
    0, 512, 1024, 1536, 1920, 2176, 2304, 2432, 2944, 3456, 3584, 3712, 3840)
IN_OFF = (0, 512, 1024, 1536, 1920, 2176, 2240, 2752, 2880, 3008, 6080)

VMEM_LIMIT = 56 * 1024 * 1024
ATTN_VMEM_BUDGET = 44 * 1024 * 1024


def _pick(n, candidates):
    for c in candidates:
        if n % c == 0:
            return c
    raise ValueError(f"no tile in {candidates} divides {n}")


def _cparams(sem):
    return pltpu.CompilerParams(dimension_semantics=sem, vmem_limit_bytes=VMEM_LIMIT)


def _resident(shape):
    zeros = (0,) * len(shape)
    return pl.BlockSpec(shape, lambda *_: zeros, pipeline_mode=pl.Buffered(1))


def _rms_scale(x, width):
    return lax.rsqrt(jnp.sum(x * x, axis=-1, keepdims=True) * (1.0 / width) + EPS)


def _pair_rms_scale(xc, lo_mask):
    sq = xc * xc
    lo = jnp.sum(jnp.where(lo_mask, sq, 0.0), axis=-1, keepdims=True)
    hi = jnp.sum(jnp.where(lo_mask, 0.0, sq), axis=-1, keepdims=True)
    return jnp.where(lo_mask, lax.rsqrt(lo * (1.0 / HEAD64) + EPS), lax.rsqrt(hi * (1.0 / HEAD64) + EPS))


def _rope(xc, swapped, cos, sin_signed):
    return xc * cos + swapped * sin_signed


def _stream_spec(h, tm):
    last = (h.shape[1] - 1) // tm
    return pl.BlockSpec((1, tm, h.shape[2]), lambda b, j: (b, jnp.minimum(j, last), 0))


def _row_tile(h_ref, tail_ref):
    x = h_ref[0]
    if tail_ref is None:
        return x
    tm = x.shape[0]
    tail = jnp.concatenate([jnp.zeros((tm - TAIL, x.shape[1]), x.dtype), tail_ref[...]], axis=0)
    in_tail = lax.broadcasted_iota(jnp.int32, x.shape, 0) >= tm - TAIL
    is_last = pl.program_id(1) == pl.num_programs(1) - 1
    return jnp.where(jnp.logical_and(is_last, in_tail), tail, x)


def _prep_kernel(*refs, has_tail):
    h_ref, tail_ref = (refs[0], refs[1]) if has_tail else (refs[0], None)
    (cos_ref, sin_ref, ng_ref, wa_ref, wuq_ref, wuk_ref, wuv_ref, gt_ref, gcq_ref, gckv_ref,
     qtda_ref, kda_ref, vtda_ref, qtm_ref, km_ref, vtm_ref, qtg_ref, kg_ref, vtg_ref) = refs[2 if has_tail else 1:]
    x = _row_tile(h_ref, tail_ref)
    xn = ((x * _rms_scale(x, D_MODEL)) * ng_ref[...]).astype(BF16)
    xp = jnp.dot(xn, wa_ref[...], preferred_element_type=F32)
    cos = cos_ref[...]
    sin_s = sin_ref[...]
    lane = lax.broadcasted_iota(jnp.int32, cos.shape, 1)
    lo64 = lane < HEAD64
    s64 = (HEAD64 ** -0.5) * LOG2E
    s192 = (MLA_QK ** -0.5) * LOG2E

    first_half = (lane & 32) == 0

    def head_pair(col, col_swapped, gains):
        xc = xp[:, col:col + LANES]
        r = _pair_rms_scale(xc, lo64)
        y = (xc * r) * gains[0:1, :]
        if col_swapped is None:
            ys = jnp.where(first_half, pltpu.roll(y, LANES - 32, 1), pltpu.roll(y, 32, 1))
        else:
            ys = (xp[:, col_swapped:col_swapped + LANES] * r) * gains[1:2, :]
        return _rope(y, ys, cos, sin_s)

    ones_rows = jnp.ones((ONES_ROWS, x.shape[0]), BF16)

    def store_vt(vt_ref, hd, v_slab):
        vt_ref[0, hd * VT_ROWS:hd * VT_ROWS + LANES, :] = v_slab.astype(BF16).T
        vt_ref[0, hd * VT_ROWS + LANES:(hd + 1) * VT_ROWS, :] = ones_rows

    g_q, g_k = gt_ref[0:2, :], gt_ref[2:4, :]
    for c in range(4):
        sl = slice(c * LANES, (c + 1) * LANES)
        qtda_ref[0, sl, :] = (head_pair(A_DAQ + c * LANES, None, g_q) * s64).astype(BF16).T
        kda_ref[0, c] = head_pair(A_DAK + c * LANES, None, g_k).astype(BF16)
        store_vt(vtda_ref, c, xp[:, A_DAV + c * LANES:A_DAV + (c + 1) * LANES])

    cq = xp[:, A_CQ:A_CQ + MLA_Q_RANK]
    cqn = ((cq * _rms_scale(cq, MLA_Q_RANK)) * gcq_ref[...]).astype(BF16)
    qm = jnp.dot(cqn, wuq_ref[...], preferred_element_type=F32)
    ckv = xp[:, A_CKV:A_CKV + MLA_KV_RANK]
    ckvn = ((ckv * _rms_scale(ckv, MLA_KV_RANK)) * gckv_ref[...]).astype(BF16)
    kn_all = jnp.dot(ckvn, wuk_ref[...], preferred_element_type=F32)
    vm_all = jnp.dot(ckvn, wuv_ref[...], preferred_element_type=F32)
    kr = xp[:, A_KR:A_KR + LANES]
    krs = xp[:, A_KRS:A_KRS + LANES]
    kr_ss = jnp.sum(kr * kr, axis=-1, keepdims=True)
    gq_n, gq_r, gq_rs = gt_ref[8:9, :], gt_ref[9:10, :], gt_ref[10:11, :]
    gk_n, gk_r, gk_rs = gt_ref[11:12, :], gt_ref[12:13, :], gt_ref[13:14, :]
    for hd in range(MLA_HEADS):
        b = hd * 2 * LANES
        qb = hd * 3 * LANES
        qn = qm[:, qb:qb + LANES]
        qr = qm[:, qb + LANES:qb + 2 * LANES]
        qrs = qm[:, qb + 2 * LANES:qb + 3 * LANES]
        r = lax.rsqrt((jnp.sum(qn * qn, axis=-1, keepdims=True) + jnp.sum(qr * qr, axis=-1, keepdims=True))
                      * (1.0 / MLA_QK) + EPS)
        qtm_ref[0, b:b + LANES, :] = (((qn * r) * gq_n) * s192).astype(BF16).T
        qtm_ref[0, b + LANES:b + 2 * LANES, :] = (
            _rope((qr * r) * gq_r, (qrs * r) * gq_rs, cos, sin_s) * s192).astype(BF16).T
        kn = kn_all[:, hd * LANES:(hd + 1) * LANES]
        rk = lax.rsqrt((jnp.sum(kn * kn, axis=-1, keepdims=True) + kr_ss) * (1.0 / MLA_QK) + EPS)
        km_ref[0, hd, :, :LANES] = ((kn * rk) * gk_n).astype(BF16)
        km_ref[0, hd, :, LANES:] = _rope((kr * rk) * gk_r, (krs * rk) * gk_rs, cos, sin_s).astype(BF16)
        store_vt(vtm_ref, hd, vm_all[:, hd * LANES:(hd + 1) * LANES])

    g_q, g_k = gt_ref[4:6, :], gt_ref[6:8, :]
    for c in range(4):
        sl = slice(c * LANES, (c + 1) * LANES)
        qtg_ref[0, sl, :] = (head_pair(A_GQ + c * LANES, A_GQS + c * LANES, g_q) * s64).astype(BF16).T
    kg_ref[0] = head_pair(A_GK, A_GKS, g_k).astype(BF16)
    vgt = xp[:, A_GV:A_GV + LANES].astype(BF16).T
    for g in range(GQA_HEADS // GQA_GROUP):
        vtg_ref[0, g * GQA_VT_ROWS:g * GQA_VT_ROWS + HEAD64, :] = vgt[g * HEAD64:(g + 1) * HEAD64]
        vtg_ref[0, g * GQA_VT_ROWS + HEAD64:(g + 1) * GQA_VT_ROWS, :] = ones_rows


def _prep_call(h, tail, cos, sin, ng, wa, wuq, wuk, wuv, gt, gcq, gckv):
    B, _, D = h.shape
    Lp = cos.shape[0]
    tm = _pick(Lp, (384, 128))
    row = lambda b, j: (b, j, 0)
    colt = lambda b, j: (b, 0, j)
    tab = pl.BlockSpec((tm, LANES), lambda b, j: (j, 0))
    sds = lambda *s: jax.ShapeDtypeStruct(s, BF16)
    vt_rows = 4 * VT_ROWS
    out_shape = [sds(B, 512, Lp), sds(B, 4, Lp, LANES), sds(B, vt_rows, Lp),
                 sds(B, 1024, Lp), sds(B, MLA_HEADS, Lp, 2 * LANES), sds(B, vt_rows, Lp),
                 sds(B, 512, Lp), sds(B, Lp, LANES), sds(B, 2 * GQA_VT_ROWS, Lp)]
    hrow = lambda b, j: (b, 0, j, 0)
    out_specs = [pl.BlockSpec((1, 512, tm), colt), pl.BlockSpec((1, 4, tm, LANES), hrow), pl.BlockSpec((1, vt_rows, tm), colt),
                 pl.BlockSpec((1, 1024, tm), colt), pl.BlockSpec((1, MLA_HEADS, tm, 2 * LANES), hrow),
                 pl.BlockSpec((1, vt_rows, tm), colt),
                 pl.BlockSpec((1, 512, tm), colt), pl.BlockSpec((1, tm, LANES), row),
                 pl.BlockSpec((1, 2 * GQA_VT_ROWS, tm), colt)]
    stream = [h] if tail is None else [h, tail]
    in_specs = [_stream_spec(h, tm)] + ([] if tail is None else [_resident(tail.shape)]) + [
        tab, tab, _resident(ng.shape), _resident(wa.shape), _resident(wuq.shape), _resident(wuk.shape),
        _resident(wuv.shape), _resident(gt.shape), _resident(gcq.shape), _resident(gckv.shape)]
    return pl.pallas_call(
        functools.partial(_prep_kernel, has_tail=tail is not None), out_shape=out_shape, grid=(B, Lp // tm),
        in_specs=in_specs, out_specs=out_specs,
        compiler_params=_cparams(("parallel", "parallel")), name="prep",
    )(*stream, cos, sin, ng, wa, wuq, wuk, wuv, gt, gcq, gckv)


def _dense_attn_kernel(*refs, n_real, tk, differential, lam_init, heads_per_step, tiles_per_step):
    if differential:
        qt_ref, k_ref, vt_ref, lam_ref, sg_ref, o_ref = refs[:6]
    else:
        (qt_ref, k_ref, vt_ref, o_ref), lam_ref, sg_ref = refs[:4], None, None
    s_refs = refs[len(refs) - heads_per_step * tiles_per_step:]
    dq = qt_ref.shape[1] // heads_per_step
    tq = qt_ref.shape[2] // tiles_per_step
    for hh in range(heads_per_step):
        for u in range(tiles_per_step):
            _attend_head(qt_ref.at[0, hh * dq:(hh + 1) * dq, u * tq:(u + 1) * tq], k_ref.at[0, hh],
                         vt_ref.at[0, hh * VT_ROWS:(hh + 1) * VT_ROWS], lam_ref, sg_ref,
                         o_ref.at[0, u * tq:(u + 1) * tq, hh * LANES:(hh + 1) * LANES], s_refs[hh * tiles_per_step + u],
                         n_real=n_real, tk=tk, differential=differential, lam_init=lam_init)


def _attend_head(qt_ref, k_ref, vt_ref, lam_ref, sg_ref, o_ref, s_ref, *, n_real, tk, differential, lam_init):
    tq = qt_ref.shape[1]
    dv = o_ref.shape[1]
    qt = qt_ref[...]
    if differential:
        z = jnp.zeros((HEAD64, tq), BF16)
        wq = jnp.concatenate([jnp.concatenate([qt[:HEAD64], z], axis=1),
                              jnp.concatenate([z, qt[HEAD64:]], axis=1)], axis=0)
    else:
        wq = qt
    width = wq.shape[1]

    def scores(kc):
        return jnp.dot(kc, wq, preferred_element_type=F32)

    def absorb(m, acc, s, vtc):
        m_new = jnp.maximum(m, jnp.max(s, axis=0, keepdims=True))
        p = jnp.exp2(s - m_new).astype(BF16)
        alpha = jnp.exp2(m - m_new)
        return m_new, alpha * acc + jnp.dot(vtc, p, preferred_element_type=F32)

    def kchunk(c):
        return k_ref[pl.ds(pl.multiple_of(c * tk, tk), tk), :]

    def vchunk(c):
        return vt_ref[:, pl.ds(pl.multiple_of(c * tk, tk), tk)]

    def group(j, carry):
        m, acc = carry
        for g in range(CHUNK_GROUP):
            c = CHUNK_GROUP * j + g
            s_ref[(g + 1) % 2] = scores(kchunk(c + 1))
            m, acc = absorb(m, acc, s_ref[g % 2], vchunk(c))
        return m, acc

    n_chunks = n_real // tk
    n_groups = (n_chunks - 1) // CHUNK_GROUP
    s_ref[0] = scores(kchunk(0))
    carry = (jnp.full((1, width), NEG, F32), jnp.zeros((dv + ONES_ROWS, width), F32))
    m, acc = lax.fori_loop(0, n_groups, group, carry)
    valid = lax.broadcasted_iota(jnp.int32, (TAIL, width), 0) < N_META
    first = CHUNK_GROUP * n_groups
    for c in range(first, n_chunks):
        if c + 1 < n_chunks:
            s_ref[(c - first + 1) % 2] = scores(kchunk(c + 1))
        else:
            s_tail = jnp.where(valid, scores(k_ref[n_real:n_real + TAIL, :]), NEG)
        m, acc = absorb(m, acc, s_ref[(c - first) % 2], vchunk(c))
    _, acc = absorb(m, acc, s_tail, vt_ref[:, n_real:n_real + TAIL])

    ot = acc[:dv] * (1.0 / acc[dv:dv + 1])
    if differential:
        lt = lam_ref[...]
        lam = (jnp.exp(jnp.sum(lt[0:1, :] * lt[1:2, :], axis=-1, keepdims=True))
               - jnp.exp(jnp.sum(lt[2:3, :] * lt[3:4, :], axis=-1, keepdims=True)) + lam_init)
        o = (ot[:, :tq] - lam * ot[:, tq:]).T
        o = ((o * _rms_scale(o, dv)) * sg_ref[...]) * (1.0 - lam_init)
    else:
        o = ot.T
    o_ref[...] = o.astype(o_ref.dtype)


def _dense_attn_call(qt, k, vt, n_real, *, differential, lam_tab=None, sub_g=None, lam_init=0.0, name):
    B, heads, Lp, dq = k.shape
    if differential:
        tq = _pick(Lp, (384, 128))
    else:
        tq = LATENT_TQ if Lp >= LATENT_TQ else _pick(n_real, (512, 256, 128))
    tk = _pick(n_real, (512, 256, 128))
    width = (2 if differential else 1) * tq
    per_head = 2 * 2 * Lp * (dq + VT_ROWS) + 4 * 2 * tk * width
    hps = max(n for n in (4, 2, 1) if heads % n == 0 and (n == 1 or n * per_head <= ATTN_VMEM_BUDGET))
    tps = 2 if hps == 1 else 1
    in_specs = [pl.BlockSpec((1, hps * dq, tps * tq), lambda b, h, i: (b, h, i)),
                pl.BlockSpec((1, hps, Lp, dq), lambda b, h, i: (b, h, 0, 0)),
                pl.BlockSpec((1, hps * VT_ROWS, Lp), lambda b, h, i: (b, h, 0))]
    args = [qt, k, vt]
    if differential:
        in_specs += [_resident(lam_tab.shape), _resident(sub_g.shape)]
        args += [lam_tab, sub_g]
    kern = functools.partial(_dense_attn_kernel, n_real=n_real, tk=tk, differential=differential, lam_init=lam_init,
                             heads_per_step=hps, tiles_per_step=tps)
    return pl.pallas_call(
        kern, out_shape=jax.ShapeDtypeStruct((B, Lp, heads * LANES), BF16),
        grid=(B, heads // hps, pl.cdiv(Lp, tps * tq)),
        in_specs=in_specs, out_specs=pl.BlockSpec((1, tps * tq, hps * LANES), lambda b, h, i: (b, i, h)),
        scratch_shapes=[pltpu.VMEM((2, tk, width), F32)] * (hps * tps),
        compiler_params=_cparams(("parallel", "parallel", "parallel")), name=name,
    )(*args)


def _window_attn_kernel(sink_ref, qt_ref, k_ref, vt_ref, bias_ref, o_ref, *, n_real):
    tq = qt_ref.shape[2]
    t = pl.program_id(1)
    is_meta = t == n_real // tq
    span = tq + 2 * BLOCK
    start = pl.multiple_of(jnp.where(is_meta, 0, jnp.maximum(t * tq - BLOCK, 0)), BLOCK)
    kwin = jnp.concatenate([k_ref[0, pl.ds(start, span), :], k_ref[0, n_real:n_real + TAIL, :]], axis=0)
    vwin = jnp.concatenate([vt_ref[0, :, pl.ds(start, span)], vt_ref[0, :, n_real:n_real + TAIL]], axis=1)
    n_tiles = n_real // tq
    kind = jnp.where(is_meta, 4, (t == 0).astype(jnp.int32) + 2 * (t == n_tiles - 1).astype(jnp.int32))
    bias = jnp.concatenate([bias_ref[kind]] * GQA_GROUP, axis=1)
    qt = qt_ref[0]
    n_groups = GQA_HEADS // GQA_GROUP
    gw = GQA_GROUP * tq
    z = jnp.zeros((HEAD64, gw), BF16)
    rows = []
    for g in range(n_groups):
        qrow = jnp.concatenate([qt[h * HEAD64:(h + 1) * HEAD64] for h in range(g * GQA_GROUP, (g + 1) * GQA_GROUP)], axis=1)
        rows.append(jnp.concatenate([qrow if i == g else z for i in range(n_groups)], axis=1))
    wq = jnp.concatenate(rows, axis=0)
    sink = jnp.concatenate([jnp.full((1, tq), sink_ref[h] * LOG2E, F32) for h in range(GQA_HEADS)], axis=1)
    s = jnp.dot(kwin, wq, preferred_element_type=F32) + jnp.concatenate([bias, bias], axis=1)
    m = jnp.maximum(jnp.max(s, axis=0, keepdims=True), sink)
    p = jnp.exp2(s - m).astype(BF16)
    extra = jnp.exp2(sink - m)
    for g in range(n_groups):
        acc = jnp.dot(vwin[g * GQA_VT_ROWS:(g + 1) * GQA_VT_ROWS], p[:, g * gw:(g + 1) * gw], preferred_element_type=F32)
        ot = acc[:HEAD64] * (1.0 / (acc[HEAD64:HEAD64 + 1] + extra[:, g * gw:(g + 1) * gw]))
        for i in range(GQA_GROUP // 2):
            pair = jnp.concatenate([ot[:, 2 * i * tq:(2 * i + 1) * tq], ot[:, (2 * i + 1) * tq:(2 * i + 2) * tq]], axis=0)
            c = g * (GQA_GROUP // 2) + i
            o_ref[0, :, c * LANES:(c + 1) * LANES] = pair.T.astype(o_ref.dtype)


def _window_bias(n_real, tq):
    span = tq + 2 * BLOCK
    ks = jnp.arange(span + TAIL, dtype=jnp.int32)[:, None]
    qi = jnp.arange(tq, dtype=jnp.int32)[None, :]
    in_span = ks < span
    tail_ok = (ks >= span) & (ks < span + N_META)
    n_tiles = n_real // tq
    tables = []
    for first, last in ((False, False), (True, False), (False, True), (True, True)):
        t = 0 if first else (n_tiles - 1 if last else 1)
        kpos = max(t * tq - BLOCK, 0) + ks
        dist = kpos - (t * tq + qi)
        tables.append((dist <= WINDOW) & (dist >= -WINDOW) & in_span & (kpos < n_real) | tail_ok)
    tables.append(in_span & (ks <= qi + (WINDOW - N_META)) | tail_ok)
    return jnp.where(jnp.stack(tables), 0.0, NEG).astype(F32)


def _window_attn_call(qt, k, vt, sink, n_real):
    B, Lp, _ = k.shape
    tq = 256 if n_real % 256 == 0 and n_real >= 384 else BLOCK
    assert n_real % tq == 0 and tq + 2 * BLOCK <= Lp
    bias = _window_bias(n_real, tq)
    grid_spec = pltpu.PrefetchScalarGridSpec(
        num_scalar_prefetch=1, grid=(B, n_real // tq + 1),
        in_specs=[pl.BlockSpec((1, GQA_HEADS * HEAD64, tq), lambda b, t, s: (b, 0, t)),
                  pl.BlockSpec((1, Lp, LANES), lambda b, t, s: (b, 0, 0)),
                  pl.BlockSpec((1, 2 * GQA_VT_ROWS, Lp), lambda b, t, s: (b, 0, 0)),
                  _resident(bias.shape)],
        out_specs=pl.BlockSpec((1, tq, 512), lambda b, t, s: (b, t, 0)))
    return pl.pallas_call(
        functools.partial(_window_attn_kernel, n_real=n_real),
        out_shape=jax.ShapeDtypeStruct((B, Lp, 512), BF16), grid_spec=grid_spec,
        compiler_params=_cparams(("parallel", "parallel")), name="window_attn",
    )(sink, qt, k, vt, bias)


def _post_kernel(*refs, has_tail):
    h_ref, tail_ref = (refs[0], refs[1]) if has_tail else (refs[0], None)
    oda_ref, om_ref, og_ref, ng_ref, wg_ref, wb_ref, wo_ref, mg_ref, wu_ref, wd_ref, out_ref = refs[2 if has_tail else 1:]
    x = _row_tile(h_ref, tail_ref)
    xn = ((x * _rms_scale(x, D_MODEL)) * ng_ref[...]).astype(BF16)
    gl = jnp.dot(xn, wg_ref[...], preferred_element_type=F32)
    merged = None
    for g, o_ref in enumerate((oda_ref, om_ref, og_ref)):
        proj = jnp.dot(o_ref[0], wb_ref[g], preferred_element_type=F32)
        term = jax.nn.sigmoid(gl[:, g * D_MODEL:(g + 1) * D_MODEL]) * proj
        merged = term if merged is None else merged + term
    h1 = x + jnp.dot(merged.astype(BF16), wo_ref[...], preferred_element_type=F32)
    h1n = ((h1 * _rms_scale(h1, D_MODEL)) * mg_ref[...]).astype(BF16)
    u = jnp.dot(h1n, wu_ref[...], preferred_element_type=F32)
    a = jnp.square(jnp.maximum(u, 0.0)).astype(BF16)
    out_ref[0] = h1 + jnp.dot(a, wd_ref[...], preferred_element_type=F32)


def _post_call(h, tail, oda, om, og, ng, wg, wb, wo, mg, wu, wd, out_rows):
    B, Lp, _ = oda.shape
    D = h.shape[2]
    tm = _pick(Lp, (POST_TM, 128))
    row = lambda b, j: (b, j, 0)
    stream = [h] if tail is None else [h, tail]
    in_specs = [_stream_spec(h, tm)] + ([] if tail is None else [_resident(tail.shape)]) + [
        pl.BlockSpec((1, tm, BRANCH_WIDTH), row)] * 3 + [_resident(a.shape) for a in (ng, wg, wb, wo, mg, wu, wd)]
    return pl.pallas_call(
        functools.partial(_post_kernel, has_tail=tail is not None),
        out_shape=jax.ShapeDtypeStruct((B, out_rows, D), F32), grid=(B, pl.cdiv(out_rows, tm)),
        in_specs=in_specs, out_specs=pl.BlockSpec((1, tm, D), row),
        compiler_params=_cparams(("parallel", "parallel")), name="post",
    )(*stream, oda, om, og, ng, wg, wb, wo, mg, wu, wd)


def _swap_halves(m):
    return jnp.flip(m.reshape(m.shape[:-1] + (m.shape[-1] // HEAD64, 2, HEAD64 // 2)), axis=-2).reshape(m.shape)


def _pair_gain(g):
    g = jnp.tile(g.astype(F32), 2)[None, :]
    return jnp.concatenate([g, _swap_halves(g)], axis=0)


def _pad_gain(g):
    return jnp.concatenate([g.astype(F32), jnp.zeros((LANES - g.shape[0],), F32)])[None, :]


def _layer_params(l, attn_norm_g, w_in, da_q_norm_g, da_k_norm_g, da_lam_q1, da_lam_k1, da_lam_q2, da_lam_k2,
                  da_subln_g, mla_cq_norm_g, mla_ckv_norm_g, mla_w_uq, mla_w_ukv, mla_q_norm_g, mla_k_norm_g,
                  gqa_q_norm_g, gqa_k_norm_g, gqa_sink, w_branch, w_out, mlp_norm_g, w_up, w_down):
    w = w_in[l]
    sec = [w[:, a:b] for a, b in zip(IN_OFF[:-1], IN_OFF[1:])]
    da_q, da_k, da_v, cq, ckv, kr, gq, gk, gv, gate = sec
    pad64 = lambda m: jnp.pad(m, ((0, 0),) * (m.ndim - 1) + ((0, LANES - HEAD64),))
    wa = jnp.concatenate([da_q, da_k, da_v, cq, ckv,
                          pad64(kr), pad64(_swap_halves(kr)), gq, _swap_halves(gq),
                          gk, _swap_halves(gk), gv], axis=1).astype(BF16)
    wuq = mla_w_uq[l].reshape(MLA_Q_RANK, MLA_HEADS, MLA_QK)
    wuq_r = wuq[:, :, MLA_NOPE:]
    wuq = jnp.concatenate([wuq[:, :, :MLA_NOPE], pad64(wuq_r), pad64(_swap_halves(wuq_r))], axis=-1)
    wuq = wuq.reshape(MLA_Q_RANK, MLA_HEADS * 3 * LANES).astype(BF16)
    wukv = mla_w_ukv[l].reshape(MLA_KV_RANK, MLA_HEADS, 2 * LANES)
    wuk = wukv[:, :, :MLA_NOPE].reshape(MLA_KV_RANK, MLA_HEADS * LANES).astype(BF16)
    wuv = wukv[:, :, MLA_NOPE:].reshape(MLA_KV_RANK, MLA_HEADS * LANES).astype(BF16)
    mq, mk = mla_q_norm_g[l], mla_k_norm_g[l]
    gt = jnp.concatenate([_pair_gain(da_q_norm_g[l]), _pair_gain(da_k_norm_g[l]),
                          _pair_gain(gqa_q_norm_g[l]), _pair_gain(gqa_k_norm_g[l]),
                          mq[None, :MLA_NOPE].astype(F32), _pad_gain(mq[MLA_NOPE:]), _pad_gain(_swap_halves(mq[MLA_NOPE:])),
                          mk[None, :MLA_NOPE].astype(F32), _pad_gain(mk[MLA_NOPE:]), _pad_gain(_swap_halves(mk[MLA_NOPE:])),
                          jnp.zeros((2, LANES), F32)], axis=0)
    lam_tab = jnp.concatenate([_pad_gain(v[l]) for v in (da_lam_q1, da_lam_k1, da_lam_q2, da_lam_k2)], axis=0)
    return dict(
        ng=attn_norm_g[l][None, :].astype(F32), wa=wa, wuq=wuq, wuk=wuk, wuv=wuv, gt=gt,
        gcq=mla_cq_norm_g[l][None, :].astype(F32), gckv=mla_ckv_norm_g[l][None, :].astype(F32),
        lam_tab=lam_tab, sub_g=da_subln_g[l][None, :].astype(F32), sink=gqa_sink[l].astype(F32),
        wg=gate.astype(BF16), wb=w_branch[l].astype(BF16), wo=w_out[l].astype(BF16),
        mg=mlp_norm_g[l][None, :].astype(F32), wu=w_up[l].astype(BF16), wd=w_down[l].astype(BF16))


def _rope_tables(n_real):
    r = jnp.arange(n_real + TAIL, dtype=jnp.int32)
    pos = jnp.where(r < n_real, r + N_META, jnp.where(r < n_real + N_META, r - n_real, 0)).astype(F32)
    inv_freq = 1.0 / (ROPE_THETA ** (jnp.arange(0, HEAD64, 2, dtype=F32) / HEAD64))
    ang = pos[:, None] * inv_freq[None, :]
    cos, sin = jnp.cos(ang), jnp.sin(ang)
    return jnp.tile(cos, (1, 4)), jnp.tile(jnp.concatenate([-sin, sin], axis=1), (1, 2))


def _trunk(x, meta_tokens, layers):
    B, S, D = x.shape
    assert D == D_MODEL and S % BLOCK == 0
    tail = jnp.concatenate([meta_tokens.astype(x.dtype), jnp.zeros((TAIL - N_META, D), x.dtype)], axis=0)
    h = x
    cos, sin = _rope_tables(S)
    for l, p in enumerate(layers):
        lam_init = 0.8 - 0.6 * math.exp(-0.3 * l)
        qtda, kda, vtda, qtm, km, vtm, qtg, kg, vtg = _prep_call(
            h, tail, cos, sin, p["ng"], p["wa"], p["wuq"], p["wuk"], p["wuv"], p["gt"], p["gcq"], p["gckv"])
        oda = _dense_attn_call(qtda, kda, vtda, S, differential=True, lam_tab=p["lam_tab"], sub_g=p["sub_g"],
                               lam_init=lam_init, name="diff_attn")
        om = _dense_attn_call(qtm, km, vtm, S, differential=False, name="latent_attn")
        og = _window_attn_call(qtg, kg, vtg, p["sink"], S)
        h = _post_call(h, tail, oda, om, og, p["ng"], p["wg"], p["wb"], p["wo"], p["mg"], p["wu"], p["wd"],
                       out_rows=S if l == len(layers) - 1 else S + TAIL)
        tail = None
    return h


def kernel(x_prompt, x_sample, meta_tokens, attn_norm_g, w_in, da_q_norm_g, da_k_norm_g, da_lam_q1, da_lam_k1, da_lam_q2, da_lam_k2, da_subln_g, mla_cq_norm_g, mla_ckv_norm_g, mla_w_uq, mla_w_ukv, mla_q_norm_g, mla_k_norm_g, gqa_q_norm_g, gqa_k_norm_g, gqa_sink, w_branch, w_out, mlp_norm_g, w_up, w_down):
    depth = w_in.shape[0]
    layers = [_layer_params(l, attn_norm_g, w_in, da_q_norm_g, da_k_norm_g, da_lam_q1, da_lam_k1, da_lam_q2,
                            da_lam_k2, da_subln_g, mla_cq_norm_g, mla_ckv_norm_g, mla_w_uq, mla_w_ukv,
                            mla_q_norm_g, mla_k_norm_g, gqa_q_norm_g, gqa_k_norm_g, gqa_sink, w_branch, w_out,
                            mlp_norm_g, w_up, w_down) for l in range(depth)]
    return (_trunk(x_prompt, meta_tokens, layers), _trunk(x_sample, meta_tokens, layers))
```

```python
import functools
import math

import jax
import jax.numpy as jnp
from jax import lax
from jax.experimental import pallas as pl
from jax.experimental.pallas import tpu as pltpu

F32 = jnp.float32
BF16 = jnp.bfloat16

D_MODEL = 1024
N_META = 16
BLOCK = 128
WINDOW = 128
ROPE_THETA = 10000.0
EPS = 1e-6
NEG = -1e30
LOG2E = 1.4426950408889634

HEAD64 = 64
MLA_HEADS = 4
MLA_NOPE = 128
MLA_QK = 192
MLA_Q_RANK = 384
MLA_KV_RANK = 256
GQA_HEADS = 8
GQA_GROUP = 4
BRANCH_WIDTH = 512
LANES = 128
TAIL = 128
ONES_ROWS = 16
VT_ROWS = LANES + ONES_ROWS
GQA_VT_ROWS = HEAD64 + ONES_ROWS
CHUNK_GROUP = 8
POST_TM = 384
LATENT_TQ = 768

A_DAQ, A_DAK, A_DAV, A_CQ, A_CKV, A_KR, A_KRS, A_GQ, A_GQS, A_GK, A_GKS, A_GV, A_END = (
    0, 512, 1024, 1536, 1920, 2176, 2304, 2432, 2944, 3456, 3584, 3712, 3840)
IN_OFF = (0, 512, 1024, 1536, 1920, 2176, 2240, 2752, 2880, 3008, 6080)

VMEM_LIMIT = 56 * 1024 * 1024
ATTN_VMEM_BUDGET = 44 * 1024 * 1024


def _pick(n, candidates):
    for c in candidates:
        if n % c == 0:
            return c
    raise ValueError(f"no tile in {candidates} divides {n}")


def _cparams(sem):
    return pltpu.CompilerParams(dimension_semantics=sem, vmem_limit_bytes=VMEM_LIMIT)


def _resident(shape):
    zeros = (0,) * len(shape)
    return pl.BlockSpec(shape, lambda *_: zeros, pipeline_mode=pl.Buffered(1))


def _rms_scale(x, width):
    return lax.rsqrt(jnp.sum(x * x, axis=-1, keepdims=True) * (1.0 / width) + EPS)


def _pair_rms_scale(xc, lo_mask):
    sq = xc * xc
    lo = jnp.sum(jnp.where(lo_mask, sq, 0.0), axis=-1, keepdims=True)
    hi = jnp.sum(jnp.where(lo_mask, 0.0, sq), axis=-1, keepdims=True)
    return jnp.where(lo_mask, lax.rsqrt(lo * (1.0 / HEAD64) + EPS), lax.rsqrt(hi * (1.0 / HEAD64) + EPS))


def _rope(xc, swapped, cos, sin_signed):
    return xc * cos + swapped * sin_signed


def _stream_spec(h, tm):
    last = (h.shape[1] - 1) // tm
    return pl.BlockSpec((1, tm, h.shape[2]), lambda b, j: (b, jnp.minimum(j, last), 0))


def _row_tile(h_ref, tail_ref):
    x = h_ref[0]
    if tail_ref is None:
        return x
    tm = x.shape[0]
    tail = jnp.concatenate([jnp.zeros((tm - TAIL, x.shape[1]), x.dtype), tail_ref[...]], axis=0)
    in_tail = lax.broadcasted_iota(jnp.int32, x.shape, 0) >= tm - TAIL
    is_last = pl.program_id(1) == pl.num_programs(1) - 1
    return jnp.where(jnp.logical_and(is_last, in_tail), tail, x)


def _prep_kernel(*refs, has_tail):
    h_ref, tail_ref = (refs[0], refs[1]) if has_tail else (refs[0], None)
    (cos_ref, sin_ref, ng_ref, wa_ref, wuq_ref, wuk_ref, wuv_ref, gt_ref, gcq_ref, gckv_ref,
     qtda_ref, kda_ref, vtda_ref, qtm_ref, km_ref, vtm_ref, qtg_ref, kg_ref, vtg_ref) = refs[2 if has_tail else 1:]
    x = _row_tile(h_ref, tail_ref)
    xn = ((x * _rms_scale(x, D_MODEL)) * ng_ref[...]).astype(BF16)
    xp = jnp.dot(xn, wa_ref[...], preferred_element_type=F32)
    cos = cos_ref[...]
    sin_s = sin_ref[...]
    lane = lax.broadcasted_iota(jnp.int32, cos.shape, 1)
    lo64 = lane < HEAD64
    s64 = (HEAD64 ** -0.5) * LOG2E
    s192 = (MLA_QK ** -0.5) * LOG2E

    first_half = (lane & 32) == 0

    def head_pair(col, col_swapped, gains):
        xc = xp[:, col:col + LANES]
        r = _pair_rms_scale(xc, lo64)
        y = (xc * r) * gains[0:1, :]
        if col_swapped is None:
            ys = jnp.where(first_half, pltpu.roll(y, LANES - 32, 1), pltpu.roll(y, 32, 1))
        else:
            ys = (xp[:, col_swapped:col_swapped + LANES] * r) * gains[1:2, :]
        return _rope(y, ys, cos, sin_s)

    ones_rows = jnp.ones((ONES_ROWS, x.shape[0]), BF16)

    def store_vt(vt_ref, hd, v_slab):
        vt_ref[0, hd * VT_ROWS:hd * VT_ROWS + LANES, :] = v_slab.astype(BF16).T
        vt_ref[0, hd * VT_ROWS + LANES:(hd + 1) * VT_ROWS, :] = ones_rows

    g_q, g_k = gt_ref[0:2, :], gt_ref[2:4, :]
    for c in range(4):
        sl = slice(c * LANES, (c + 1) * LANES)
        qtda_ref[0, sl, :] = (head_pair(A_DAQ + c * LANES, None, g_q) * s64).astype(BF16).T
        kda_ref[0, c] = head_pair(A_DAK + c * LANES, None, g_k).astype(BF16)
        store_vt(vtda_ref, c, xp[:, A_DAV + c * LANES:A_DAV + (c + 1) * LANES])

    cq = xp[:, A_CQ:A_CQ + MLA_Q_RANK]
    cqn = ((cq * _rms_scale(cq, MLA_Q_RANK)) * gcq_ref[...]).astype(BF16)
    qm = jnp.dot(cqn, wuq_ref[...], preferred_element_type=F32)
    ckv = xp[:, A_CKV:A_CKV + MLA_KV_RANK]
    ckvn = ((ckv * _rms_scale(ckv, MLA_KV_RANK)) * gckv_ref[...]).astype(BF16)
    kn_all = jnp.dot(ckvn, wuk_ref[...], preferred_element_type=F32)
    vm_all = jnp.dot(ckvn, wuv_ref[...], preferred_element_type=F32)
    kr = xp[:, A_KR:A_KR + LANES]
    krs = xp[:, A_KRS:A_KRS + LANES]
    kr_ss = jnp.sum(kr * kr, axis=-1, keepdims=True)
    gq_n, gq_r, gq_rs = gt_ref[8:9, :], gt_ref[9:10, :], gt_ref[10:11, :]
    gk_n, gk_r, gk_rs = gt_ref[11:12, :], gt_ref[12:13, :], gt_ref[13:14, :]
    for hd in range(MLA_HEADS):
        b = hd * 2 * LANES
        qb = hd * 3 * LANES
        qn = qm[:, qb:qb + LANES]
        qr = qm[:, qb + LANES:qb + 2 * LANES]
        qrs = qm[:, qb + 2 * LANES:qb + 3 * LANES]
        r = lax.rsqrt((jnp.sum(qn * qn, axis=-1, keepdims=True) + jnp.sum(qr * qr, axis=-1, keepdims=True))
                      * (1.0 / MLA_QK) + EPS)
        qtm_ref[0, b:b + LANES, :] = (((qn * r) * gq_n) * s192).astype(BF16).T
        qtm_ref[0, b + LANES:b + 2 * LANES, :] = (
            _rope((qr * r) * gq_r, (qrs * r) * gq_rs, cos, sin_s) * s192).astype(BF16).T
        kn = kn_all[:, hd * LANES:(hd + 1) * LANES]
        rk = lax.rsqrt((jnp.sum(kn * kn, axis=-1, keepdims=True) + kr_ss) * (1.0 / MLA_QK) + EPS)
        km_ref[0, hd, :, :LANES] = ((kn * rk) * gk_n).astype(BF16)
        km_ref[0, hd, :, LANES:] = _rope((kr * rk) * gk_r, (krs * rk) * gk_rs, cos, sin_s).astype(BF16)
        store_vt(vtm_ref, hd, vm_all[:, hd * LANES:(hd + 1) * LANES])

    g_q, g_k = gt_ref[4:6, :], gt_ref[6:8, :]
    for c in range(4):
        sl = slice(c * LANES, (c + 1) * LANES)
        qtg_ref[0, sl, :] = (head_pair(A_GQ + c * LANES, A_GQS + c * LANES, g_q) * s64).astype(BF16).T
    kg_ref[0] = head_pair(A_GK, A_GKS, g_k).astype(BF16)
    vgt = xp[:, A_GV:A_GV + LANES].astype(BF16).T
    for g in range(GQA_HEADS // GQA_GROUP):
        vtg_ref[0, g * GQA_VT_ROWS:g * GQA_VT_ROWS + HEAD64, :] = vgt[g * HEAD64:(g + 1) * HEAD64]
        vtg_ref[0, g * GQA_VT_ROWS + HEAD64:(g + 1) * GQA_VT_ROWS, :] = ones_rows


def _prep_call(h, tail, cos, sin, ng, wa, wuq, wuk, wuv, gt, gcq, gckv):
    B, _, D = h.shape
    Lp = cos.shape[0]
    tm = _pick(Lp, (384, 128))
    row = lambda b, j: (b, j, 0)
    colt = lambda b, j: (b, 0, j)
    tab = pl.BlockSpec((tm, LANES), lambda b, j: (j, 0))
    sds = lambda *s: jax.ShapeDtypeStruct(s, BF16)
    vt_rows = 4 * VT_ROWS
    out_shape = [sds(B, 512, Lp), sds(B, 4, Lp, LANES), sds(B, vt_rows, Lp),
                 sds(B, 1024, Lp), sds(B, MLA_HEADS, Lp, 2 * LANES), sds(B, vt_rows, Lp),
                 sds(B, 512, Lp), sds(B, Lp, LANES), sds(B, 2 * GQA_VT_ROWS, Lp)]
    hrow = lambda b, j: (b, 0, j, 0)
    out_specs = [pl.BlockSpec((1, 512, tm), colt), pl.BlockSpec((1, 4, tm, LANES), hrow), pl.BlockSpec((1, vt_rows, tm), colt),
                 pl.BlockSpec((1, 1024, tm), colt), pl.BlockSpec((1, MLA_HEADS, tm, 2 * LANES), hrow),
                 pl.BlockSpec((1, vt_rows, tm), colt),
                 pl.BlockSpec((1, 512, tm), colt), pl.BlockSpec((1, tm, LANES), row),
                 pl.BlockSpec((1, 2 * GQA_VT_ROWS, tm), colt)]
    stream = [h] if tail is None else [h, tail]
    in_specs = [_stream_spec(h, tm)] + ([] if tail is None else [_resident(tail.shape)]) + [
        tab, tab, _resident(ng.shape), _resident(wa.shape), _resident(wuq.shape), _resident(wuk.shape),
        _resident(wuv.shape), _resident(gt.shape), _resident(gcq.shape), _resident(gckv.shape)]
    return pl.pallas_call(
        functools.partial(_prep_kernel, has_tail=tail is not None), out_shape=out_shape, grid=(B, Lp // tm),
        in_specs=in_specs, out_specs=out_specs,
        compiler_params=_cparams(("parallel", "parallel")), name="prep",
    )(*stream, cos, sin, ng, wa, wuq, wuk, wuv, gt, gcq, gckv)


def _dense_attn_kernel(*refs, n_real, tk, differential, lam_init, heads_per_step, tiles_per_step):
    if differential:
        qt_ref, k_ref, vt_ref, lam_ref, sg_ref, o_ref = refs[:6]
    else:
        (qt_ref, k_ref, vt_ref, o_ref), lam_ref, sg_ref = refs[:4], None, None
    s_refs = refs[len(refs) - heads_per_step * tiles_per_step:]
    dq = qt_ref.shape[1] // heads_per_step
    tq = qt_ref.shape[2] // tiles_per_step
    for hh in range(heads_per_step):
        for u in range(tiles_per_step):
            _attend_head(qt_ref.at[0, hh * dq:(hh + 1) * dq, u * tq:(u + 1) * tq], k_ref.at[0, hh],
                         vt_ref.at[0, hh * VT_ROWS:(hh + 1) * VT_ROWS], lam_ref, sg_ref,
                         o_ref.at[0, u * tq:(u + 1) * tq, hh * LANES:(hh + 1) * LANES], s_refs[hh * tiles_per_step + u],
                         n_real=n_real, tk=tk, differential=differential, lam_init=lam_init)


def _attend_head(qt_ref, k_ref, vt_ref, lam_ref, sg_ref, o_ref, s_ref, *, n_real, tk, differential, lam_init):
    tq = qt_ref.shape[1]
    dv = o_ref.shape[1]
    qt = qt_ref[...]
    if differential:
        z = jnp.zeros((HEAD64, tq), BF16)
        wq = jnp.concatenate([jnp.concatenate([qt[:HEAD64], z], axis=1),
                              jnp.concatenate([z, qt[HEAD64:]], axis=1)], axis=0)
    else:
        wq = qt
    width = wq.shape[1]

    def scores(kc):
        return jnp.dot(kc, wq, preferred_element_type=F32)

    def absorb(m, acc, s, vtc):
        m_new = jnp.maximum(m, jnp.max(s, axis=0, keepdims=True))
        p = jnp.exp2(s - m_new).astype(BF16)
        alpha = jnp.exp2(m - m_new)
        return m_new, alpha * acc + jnp.dot(vtc, p, preferred_element_type=F32)

    def kchunk(c):
        return k_ref[pl.ds(pl.multiple_of(c * tk, tk), tk), :]

    def vchunk(c):
        return vt_ref[:, pl.ds(pl.multiple_of(c * tk, tk), tk)]

    def group(j, carry):
        m, acc = carry
        for g in range(CHUNK_GROUP):
            c = CHUNK_GROUP * j + g
            s_ref[(g + 1) % 2] = scores(kchunk(c + 1))
            m, acc = absorb(m, acc, s_ref[g % 2], vchunk(c))
        return m, acc

    n_chunks = n_real // tk
    n_groups = (n_chunks - 1) // CHUNK_GROUP
    s_ref[0] = scores(kchunk(0))
    carry = (jnp.full((1, width), NEG, F32), jnp.zeros((dv + ONES_ROWS, width), F32))
    m, acc = lax.fori_loop(0, n_groups, group, carry)
    valid = lax.broadcasted_iota(jnp.int32, (TAIL, width), 0) < N_META
    first = CHUNK_GROUP * n_groups
    for c in range(first, n_chunks):
        if c + 1 < n_chunks:
            s_ref[(c - first + 1) % 2] = scores(kchunk(c + 1))
        else:
            s_tail = jnp.where(valid, scores(k_ref[n_real:n_real + TAIL, :]), NEG)
        m, acc = absorb(m, acc, s_ref[(c - first) % 2], vchunk(c))
    _, acc = absorb(m, acc, s_tail, vt_ref[:, n_real:n_real + TAIL])

    ot = acc[:dv] * (1.0 / acc[dv:dv + 1])
    if differential:
        lt = lam_ref[...]
        lam = (jnp.exp(jnp.sum(lt[0:1, :] * lt[1:2, :], axis=-1, keepdims=True))
               - jnp.exp(jnp.sum(lt[2:3, :] * lt[3:4, :], axis=-1, keepdims=True)) + lam_init)
        ot = ot[:, :tq] - lam * ot[:, tq:]
        r = lax.rsqrt(jnp.sum(ot * ot, axis=0, keepdims=True) * (1.0 / dv) + EPS)
        gain = jnp.concatenate([sg_ref[...]] * (tq // LANES), axis=1)
        ot = ((ot * r) * gain) * (1.0 - lam_init)
    o_ref[...] = ot.astype(o_ref.dtype).T


def _dense_attn_call(qt, k, vt, n_real, *, differential, lam_tab=None, sub_g=None, lam_init=0.0, name):
    B, heads, Lp, dq = k.shape
    if differential:
        tq = _pick(Lp, (384, 128))
    else:
        tq = LATENT_TQ if Lp >= LATENT_TQ else _pick(n_real, (512, 256, 128))
    tk = _pick(n_real, (512, 256, 128))
    width = (2 if differential else 1) * tq
    per_head = 2 * 2 * Lp * (dq + VT_ROWS) + 4 * 2 * tk * width
    hps = max(n for n in (4, 2, 1) if heads % n == 0 and (n == 1 or n * per_head <= ATTN_VMEM_BUDGET))
    tps = 2 if hps == 1 else 1
    in_specs = [pl.BlockSpec((1, hps * dq, tps * tq), lambda b, h, i: (b, h, i)),
                pl.BlockSpec((1, hps, Lp, dq), lambda b, h, i: (b, h, 0, 0)),
                pl.BlockSpec((1, hps * VT_ROWS, Lp), lambda b, h, i: (b, h, 0))]
    args = [qt, k, vt]
    if differential:
        in_specs += [_resident(lam_tab.shape), _resident(sub_g.shape)]
        args += [lam_tab, sub_g]
    kern = functools.partial(_dense_attn_kernel, n_real=n_real, tk=tk, differential=differential, lam_init=lam_init,
                             heads_per_step=hps, tiles_per_step=tps)
    return pl.pallas_call(
        kern, out_shape=jax.ShapeDtypeStruct((B, Lp, heads * LANES), BF16),
        grid=(B, heads // hps, pl.cdiv(Lp, tps * tq)),
        in_specs=in_specs, out_specs=pl.BlockSpec((1, tps * tq, hps * LANES), lambda b, h, i: (b, i, h)),
        scratch_shapes=[pltpu.VMEM((2, tk, width), F32)] * (hps * tps),
        compiler_params=_cparams(("parallel", "parallel", "parallel")), name=name,
    )(*args)


def _window_attn_kernel(sink_ref, qt_ref, k_ref, vt_ref, bias_ref, o_ref, *, n_real):
    tq = qt_ref.shape[2]
    t = pl.program_id(1)
    is_meta = t == n_real // tq
    span = tq + 2 * BLOCK
    start = pl.multiple_of(jnp.where(is_meta, 0, jnp.maximum(t * tq - BLOCK, 0)), BLOCK)
    kwin = jnp.concatenate([k_ref[0, pl.ds(start, span), :], k_ref[0, n_real:n_real + TAIL, :]], axis=0)
    vwin = jnp.concatenate([vt_ref[0, :, pl.ds(start, span)], vt_ref[0, :, n_real:n_real + TAIL]], axis=1)
    n_tiles = n_real // tq
    kind = jnp.where(is_meta, 4, (t == 0).astype(jnp.int32) + 2 * (t == n_tiles - 1).astype(jnp.int32))
    bias = jnp.concatenate([bias_ref[kind]] * GQA_GROUP, axis=1)
    qt = qt_ref[0]
    n_groups = GQA_HEADS // GQA_GROUP
    gw = GQA_GROUP * tq
    z = jnp.zeros((HEAD64, gw), BF16)
    rows = []
    for g in range(n_groups):
        qrow = jnp.concatenate([qt[h * HEAD64:(h + 1) * HEAD64] for h in range(g * GQA_GROUP, (g + 1) * GQA_GROUP)], axis=1)
        rows.append(jnp.concatenate([qrow if i == g else z for i in range(n_groups)], axis=1))
    wq = jnp.concatenate(rows, axis=0)
    sink = jnp.concatenate([jnp.full((1, tq), sink_ref[h] * LOG2E, F32) for h in range(GQA_HEADS)], axis=1)
    s = jnp.dot(kwin, wq, preferred_element_type=F32) + jnp.concatenate([bias, bias], axis=1)
    m = jnp.maximum(jnp.max(s, axis=0, keepdims=True), sink)
    p = jnp.exp2(s - m).astype(BF16)
    extra = jnp.exp2(sink - m)
    for g in range(n_groups):
        acc = jnp.dot(vwin[g * GQA_VT_ROWS:(g + 1) * GQA_VT_ROWS], p[:, g * gw:(g + 1) * gw], preferred_element_type=F32)
        ot = acc[:HEAD64] * (1.0 / (acc[HEAD64:HEAD64 + 1] + extra[:, g * gw:(g + 1) * gw]))
        for i in range(GQA_GROUP // 2):
            pair = jnp.concatenate([ot[:, 2 * i * tq:(2 * i + 1) * tq], ot[:, (2 * i + 1) * tq:(2 * i + 2) * tq]], axis=0)
            c = g * (GQA_GROUP // 2) + i
            o_ref[0, :, c * LANES:(c + 1) * LANES] = pair.T.astype(o_ref.dtype)


def _window_bias(n_real, tq):
    span = tq + 2 * BLOCK
    ks = jnp.arange(span + TAIL, dtype=jnp.int32)[:, None]
    qi = jnp.arange(tq, dtype=jnp.int32)[None, :]
    in_span = ks < span
    tail_ok = (ks >= span) & (ks < span + N_META)
    n_tiles = n_real // tq
    tables = []
    for first, last in ((False, False), (True, False), (False, True), (True, True)):
        t = 0 if first else (n_tiles - 1 if last else 1)
        kpos = max(t * tq - BLOCK, 0) + ks
        dist = kpos - (t * tq + qi)
        tables.append((dist <= WINDOW) & (dist >= -WINDOW) & in_span & (kpos < n_real) | tail_ok)
    tables.append(in_span & (ks <= qi + (WINDOW - N_META)) | tail_ok)
    return jnp.where(jnp.stack(tables), 0.0, NEG).astype(F32)


def _window_attn_call(qt, k, vt, sink, n_real):
    B, Lp, _ = k.shape
    tq = 256 if n_real % 256 == 0 and n_real >= 384 else BLOCK
    assert n_real % tq == 0 and tq + 2 * BLOCK <= Lp
    bias = _window_bias(n_real, tq)
    grid_spec = pltpu.PrefetchScalarGridSpec(
        num_scalar_prefetch=1, grid=(B, n_real // tq + 1),
        in_specs=[pl.BlockSpec((1, GQA_HEADS * HEAD64, tq), lambda b, t, s: (b, 0, t)),
                  pl.BlockSpec((1, Lp, LANES), lambda b, t, s: (b, 0, 0)),
                  pl.BlockSpec((1, 2 * GQA_VT_ROWS, Lp), lambda b, t, s: (b, 0, 0)),
                  _resident(bias.shape)],
        out_specs=pl.BlockSpec((1, tq, 512), lambda b, t, s: (b, t, 0)))
    return pl.pallas_call(
        functools.partial(_window_attn_kernel, n_real=n_real),
        out_shape=jax.ShapeDtypeStruct((B, Lp, 512), BF16), grid_spec=grid_spec,
        compiler_params=_cparams(("parallel", "parallel")), name="window_attn",
    )(sink, qt, k, vt, bias)


def _post_kernel(*refs, has_tail):
    h_ref, tail_ref = (refs[0], refs[1]) if has_tail else (refs[0], None)
    oda_ref, om_ref, og_ref, ng_ref, wg_ref, wb_ref, wo_ref, mg_ref, wu_ref, wd_ref, out_ref = refs[2 if has_tail else 1:]
    x = _row_tile(h_ref, tail_ref)
    xn = ((x * _rms_scale(x, D_MODEL)) * ng_ref[...]).astype(BF16)
    gl = jnp.dot(xn, wg_ref[...], preferred_element_type=F32)
    merged = None
    for g, o_ref in enumerate((oda_ref, om_ref, og_ref)):
        proj = jnp.dot(o_ref[0], wb_ref[g], preferred_element_type=F32)
        term = jax.nn.sigmoid(gl[:, g * D_MODEL:(g + 1) * D_MODEL]) * proj
        merged = term if merged is None else merged + term
    h1 = x + jnp.dot(merged.astype(BF16), wo_ref[...], preferred_element_type=F32)
    h1n = ((h1 * _rms_scale(h1, D_MODEL)) * mg_ref[...]).astype(BF16)
    u = jnp.dot(h1n, wu_ref[...], preferred_element_type=F32)
    a = jnp.square(jnp.maximum(u, 0.0)).astype(BF16)
    out_ref[0] = h1 + jnp.dot(a, wd_ref[...], preferred_element_type=F32)


def _post_call(h, tail, oda, om, og, ng, wg, wb, wo, mg, wu, wd, out_rows):
    B, Lp, _ = oda.shape
    D = h.shape[2]
    tm = _pick(Lp, (POST_TM, 128))
    row = lambda b, j: (b, j, 0)
    stream = [h] if tail is None else [h, tail]
    in_specs = [_stream_spec(h, tm)] + ([] if tail is None else [_resident(tail.shape)]) + [
        pl.BlockSpec((1, tm, BRANCH_WIDTH), row)] * 3 + [_resident(a.shape) for a in (ng, wg, wb, wo, mg, wu, wd)]
    return pl.pallas_call(
        functools.partial(_post_kernel, has_tail=tail is not None),
        out_shape=jax.ShapeDtypeStruct((B, out_rows, D), F32), grid=(B, pl.cdiv(out_rows, tm)),
        in_specs=in_specs, out_specs=pl.BlockSpec((1, tm, D), row),
        compiler_params=_cparams(("parallel", "parallel")), name="post",
    )(*stream, oda, om, og, ng, wg, wb, wo, mg, wu, wd)


def _swap_halves(m):
    return jnp.flip(m.reshape(m.shape[:-1] + (m.shape[-1] // HEAD64, 2, HEAD64 // 2)), axis=-2).reshape(m.shape)


def _pair_gain(g):
    g = jnp.tile(g.astype(F32), 2)[None, :]
    return jnp.concatenate([g, _swap_halves(g)], axis=0)


def _pad_gain(g):
    return jnp.concatenate([g.astype(F32), jnp.zeros((LANES - g.shape[0],), F32)])[None, :]


def _layer_params(l, attn_norm_g, w_in, da_q_norm_g, da_k_norm_g, da_lam_q1, da_lam_k1, da_lam_q2, da_lam_k2,
                  da_subln_g, mla_cq_norm_g, mla_ckv_norm_g, mla_w_uq, mla_w_ukv, mla_q_norm_g, mla_k_norm_g,
                  gqa_q_norm_g, gqa_k_norm_g, gqa_sink, w_branch, w_out, mlp_norm_g, w_up, w_down):
    w = w_in[l]
    sec = [w[:, a:b] for a, b in zip(IN_OFF[:-1], IN_OFF[1:])]
    da_q, da_k, da_v, cq, ckv, kr, gq, gk, gv, gate = sec
    pad64 = lambda m: jnp.pad(m, ((0, 0),) * (m.ndim - 1) + ((0, LANES - HEAD64),))
    wa = jnp.concatenate([da_q, da_k, da_v, cq, ckv,
                          pad64(kr), pad64(_swap_halves(kr)), gq, _swap_halves(gq),
                          gk, _swap_halves(gk), gv], axis=1).astype(BF16)
    wuq = mla_w_uq[l].reshape(MLA_Q_RANK, MLA_HEADS, MLA_QK)
    wuq_r = wuq[:, :, MLA_NOPE:]
    wuq = jnp.concatenate([wuq[:, :, :MLA_NOPE], pad64(wuq_r), pad64(_swap_halves(wuq_r))], axis=-1)
    wuq = wuq.reshape(MLA_Q_RANK, MLA_HEADS * 3 * LANES).astype(BF16)
    wukv = mla_w_ukv[l].reshape(MLA_KV_RANK, MLA_HEADS, 2 * LANES)
    wuk = wukv[:, :, :MLA_NOPE].reshape(MLA_KV_RANK, MLA_HEADS * LANES).astype(BF16)
    wuv = wukv[:, :, MLA_NOPE:].reshape(MLA_KV_RANK, MLA_HEADS * LANES).astype(BF16)
    mq, mk = mla_q_norm_g[l], mla_k_norm_g[l]
    gt = jnp.concatenate([_pair_gain(da_q_norm_g[l]), _pair_gain(da_k_norm_g[l]),
                          _pair_gain(gqa_q_norm_g[l]), _pair_gain(gqa_k_norm_g[l]),
                          mq[None, :MLA_NOPE].astype(F32), _pad_gain(mq[MLA_NOPE:]), _pad_gain(_swap_halves(mq[MLA_NOPE:])),
                          mk[None, :MLA_NOPE].astype(F32), _pad_gain(mk[MLA_NOPE:]), _pad_gain(_swap_halves(mk[MLA_NOPE:])),
                          jnp.zeros((2, LANES), F32)], axis=0)
    lam_tab = jnp.concatenate([_pad_gain(v[l]) for v in (da_lam_q1, da_lam_k1, da_lam_q2, da_lam_k2)], axis=0)
    return dict(
        ng=attn_norm_g[l][None, :].astype(F32), wa=wa, wuq=wuq, wuk=wuk, wuv=wuv, gt=gt,
        gcq=mla_cq_norm_g[l][None, :].astype(F32), gckv=mla_ckv_norm_g[l][None, :].astype(F32),
        lam_tab=lam_tab, sub_g=jnp.broadcast_to(da_subln_g[l].astype(F32)[:, None], (LANES, LANES)),
        sink=gqa_sink[l].astype(F32),
        wg=gate.astype(BF16), wb=w_branch[l].astype(BF16), wo=w_out[l].astype(BF16),
        mg=mlp_norm_g[l][None, :].astype(F32), wu=w_up[l].astype(BF16), wd=w_down[l].astype(BF16))


def _rope_tables(n_real):
    r = jnp.arange(n_real + TAIL, dtype=jnp.int32)
    pos = jnp.where(r < n_real, r + N_META, jnp.where(r < n_real + N_META, r - n_real, 0)).astype(F32)
    inv_freq = 1.0 / (ROPE_THETA ** (jnp.arange(0, HEAD64, 2, dtype=F32) / HEAD64))
    ang = pos[:, None] * inv_freq[None, :]
    cos, sin = jnp.cos(ang), jnp.sin(ang)
    return jnp.tile(cos, (1, 4)), jnp.tile(jnp.concatenate([-sin, sin], axis=1), (1, 2))


def _trunk(x, meta_tokens, layers):
    B, S, D = x.shape
    assert D == D_MODEL and S % BLOCK == 0
    tail = jnp.concatenate([meta_tokens.astype(x.dtype), jnp.zeros((TAIL - N_META, D), x.dtype)], axis=0)
    h = x
    cos, sin = _rope_tables(S)
    for l, p in enumerate(layers):
        lam_init = 0.8 - 0.6 * math.exp(-0.3 * l)
        qtda, kda, vtda, qtm, km, vtm, qtg, kg, vtg = _prep_call(
            h, tail, cos, sin, p["ng"], p["wa"], p["wuq"], p["wuk"], p["wuv"], p["gt"], p["gcq"], p["gckv"])
        oda = _dense_attn_call(qtda, kda, vtda, S, differential=True, lam_tab=p["lam_tab"], sub_g=p["sub_g"],
                               lam_init=lam_init, name="diff_attn")
        om = _dense_attn_call(qtm, km, vtm, S, differential=False, name="latent_attn")
        og = _window_attn_call(qtg, kg, vtg, p["sink"], S)
        h = _post_call(h, tail, oda, om, og, p["ng"], p["wg"], p["wb"], p["wo"], p["mg"], p["wu"], p["wd"],
                       out_rows=S if l == len(layers) - 1 else S + TAIL)
        tail = None
    return h


def kernel(x_prompt, x_sample, meta_tokens, attn_norm_g, w_in, da_q_norm_g, da_k_norm_g, da_lam_q1, da_lam_k1, da_lam_q2, da_lam_k2, da_subln_g, mla_cq_norm_g, mla_ckv_norm_g, mla_w_uq, mla_w_ukv, mla_q_norm_g, mla_k_norm_g, gqa_q_norm_g, gqa_k_norm_g, gqa_sink, w_branch, w_out, mlp_norm_g, w_up, w_down):
    depth = w_in.shape[0]
    layers = [_layer_params(l, attn_norm_g, w_in, da_q_norm_g, da_k_norm_g, da_lam_q1, da_lam_k1, da_lam_q2,
                            da_lam_k2, da_subln_g, mla_cq_norm_g, mla_ckv_norm_g, mla_w_uq, mla_w_ukv,
                            mla_q_norm_g, mla_k_norm_g, gqa_q_norm_g, gqa_k_norm_g, gqa_sink, w_branch, w_out,
                            mlp_norm_g, w_up, w_down) for l in range(depth)]
    return (_trunk(x_prompt, meta_tokens, layers), _trunk(x_sample, meta_tokens, layers))
```

```python
import functools
import math

import jax
import jax.numpy as jnp
from jax import lax
from jax.experimental import pallas as pl
from jax.experimental.pallas import tpu as pltpu

F32 = jnp.float32
BF16 = jnp.bfloat16

D_MODEL = 1024
N_META = 16
BLOCK = 128
WINDOW = 128
ROPE_THETA = 10000.0
EPS = 1e-6
NEG = -1e30
LOG2E = 1.4426950408889634

HEAD64 = 64
MLA_HEADS = 4
MLA_NOPE = 128
MLA_QK = 192
MLA_Q_RANK = 384
MLA_KV_RANK = 256
GQA_HEADS = 8
GQA_GROUP = 4
BRANCH_WIDTH = 512
LANES = 128
TAIL = 128
ONES_ROWS = 16
VT_ROWS = LANES + ONES_ROWS
GQA_VT_ROWS = HEAD64 + ONES_ROWS
CHUNK_GROUP = 8
POST_TM = 384
LATENT_TQ = 768

A_DAQ, A_DAK, A_DAV, A_CQ, A_CKV, A_KR, A_KRS, A_GQ, A_GQS, A_GK, A_GKS, A_GV, A_END = (
    0, 512, 1024, 1536, 1920, 2176, 2304, 2432, 2944, 3456, 3584, 3712, 3840)
IN_OFF = (0, 512, 1024, 1536, 1920, 2176, 2240, 2752, 2880, 3008, 6080)

VMEM_LIMIT = 56 * 1024 * 1024
ATTN_VMEM_BUDGET = 44 * 1024 * 1024
ATTN_VMEM_BUDGET_SINGLE = 50 * 1024 * 1024


def _pick(n, candidates):
    for c in candidates:
        if n % c == 0:
            return c
    raise ValueError(f"no tile in {candidates} divides {n}")


def _cparams(sem):
    return pltpu.CompilerParams(dimension_semantics=sem, vmem_limit_bytes=VMEM_LIMIT)


def _resident(shape):
    zeros = (0,) * len(shape)
    return pl.BlockSpec(shape, lambda *_: zeros, pipeline_mode=pl.Buffered(1))


def _rms_scale(x, width):
    return lax.rsqrt(jnp.sum(x * x, axis=-1, keepdims=True) * (1.0 / width) + EPS)


def _pair_rms_scale(xc, lo_mask):
    sq = xc * xc
    lo = jnp.sum(jnp.where(lo_mask, sq, 0.0), axis=-1, keepdims=True)
    hi = jnp.sum(jnp.where(lo_mask, 0.0, sq), axis=-1, keepdims=True)
    return jnp.where(lo_mask, lax.rsqrt(lo * (1.0 / HEAD64) + EPS), lax.rsqrt(hi * (1.0 / HEAD64) + EPS))


def _rope(xc, swapped, cos, sin_signed):
    return xc * cos + swapped * sin_signed


def _stream_spec(h, tm):
    last = (h.shape[1] - 1) // tm
    return pl.BlockSpec((1, tm, h.shape[2]), lambda b, j: (b, jnp.minimum(j, last), 0))


def _row_tile(h_ref, tail_ref):
    x = h_ref[0]
    if tail_ref is None:
        return x
    tm = x.shape[0]
    tail = jnp.concatenate([jnp.zeros((tm - TAIL, x.shape[1]), x.dtype), tail_ref[...]], axis=0)
    in_tail = lax.broadcasted_iota(jnp.int32, x.shape, 0) >= tm - TAIL
    is_last = pl.program_id(1) == pl.num_programs(1) - 1
    return jnp.where(jnp.logical_and(is_last, in_tail), tail, x)


def _prep_kernel(*refs, has_tail):
    h_ref, tail_ref = (refs[0], refs[1]) if has_tail else (refs[0], None)
    (cos_ref, sin_ref, ng_ref, wa_ref, wuq_ref, wuk_ref, wuv_ref, gt_ref, gcq_ref, gckv_ref,
     qtda_ref, kda_ref, vtda_ref, qtm_ref, km_ref, vtm_ref, qtg_ref, kg_ref, vtg_ref) = refs[2 if has_tail else 1:]
    x = _row_tile(h_ref, tail_ref)
    xn = ((x * _rms_scale(x, D_MODEL)) * ng_ref[...]).astype(BF16)
    xp = jnp.dot(xn, wa_ref[...], preferred_element_type=F32)
    cos = cos_ref[...]
    sin_s = sin_ref[...]
    lane = lax.broadcasted_iota(jnp.int32, cos.shape, 1)
    lo64 = lane < HEAD64
    s64 = (HEAD64 ** -0.5) * LOG2E
    s192 = (MLA_QK ** -0.5) * LOG2E

    first_half = (lane & 32) == 0

    def head_pair(col, col_swapped, gains):
        xc = xp[:, col:col + LANES]
        r = _pair_rms_scale(xc, lo64)
        y = (xc * r) * gains[0:1, :]
        if col_swapped is None:
            ys = jnp.where(first_half, pltpu.roll(y, LANES - 32, 1), pltpu.roll(y, 32, 1))
        else:
            ys = (xp[:, col_swapped:col_swapped + LANES] * r) * gains[1:2, :]
        return _rope(y, ys, cos, sin_s)

    ones_rows = jnp.ones((ONES_ROWS, x.shape[0]), BF16)

    def store_vt(vt_ref, hd, v_slab):
        vt_ref[0, hd * VT_ROWS:hd * VT_ROWS + LANES, :] = v_slab.astype(BF16).T
        vt_ref[0, hd * VT_ROWS + LANES:(hd + 1) * VT_ROWS, :] = ones_rows

    g_q, g_k = gt_ref[0:2, :], gt_ref[2:4, :]
    for c in range(4):
        sl = slice(c * LANES, (c + 1) * LANES)
        qtda_ref[0, sl, :] = (head_pair(A_DAQ + c * LANES, None, g_q) * s64).astype(BF16).T
        kda_ref[0, c] = head_pair(A_DAK + c * LANES, None, g_k).astype(BF16)
        store_vt(vtda_ref, c, xp[:, A_DAV + c * LANES:A_DAV + (c + 1) * LANES])

    cq = xp[:, A_CQ:A_CQ + MLA_Q_RANK]
    cqn = ((cq * _rms_scale(cq, MLA_Q_RANK)) * gcq_ref[...]).astype(BF16)
    qm = jnp.dot(cqn, wuq_ref[...], preferred_element_type=F32)
    ckv = xp[:, A_CKV:A_CKV + MLA_KV_RANK]
    ckvn = ((ckv * _rms_scale(ckv, MLA_KV_RANK)) * gckv_ref[...]).astype(BF16)
    kn_all = jnp.dot(ckvn, wuk_ref[...], preferred_element_type=F32)
    vm_all = jnp.dot(ckvn, wuv_ref[...], preferred_element_type=F32)
    kr = xp[:, A_KR:A_KR + LANES]
    krs = xp[:, A_KRS:A_KRS + LANES]
    kr_ss = jnp.sum(kr * kr, axis=-1, keepdims=True)
    gq_n, gq_r, gq_rs = gt_ref[8:9, :], gt_ref[9:10, :], gt_ref[10:11, :]
    gk_n, gk_r, gk_rs = gt_ref[11:12, :], gt_ref[12:13, :], gt_ref[13:14, :]
    for hd in range(MLA_HEADS):
        b = hd * 2 * LANES
        qb = hd * 3 * LANES
        qn = qm[:, qb:qb + LANES]
        qr = qm[:, qb + LANES:qb + 2 * LANES]
        qrs = qm[:, qb + 2 * LANES:qb + 3 * LANES]
        r = lax.rsqrt((jnp.sum(qn * qn, axis=-1, keepdims=True) + jnp.sum(qr * qr, axis=-1, keepdims=True))
                      * (1.0 / MLA_QK) + EPS)
        qtm_ref[0, b:b + LANES, :] = (((qn * r) * gq_n) * s192).astype(BF16).T
        qtm_ref[0, b + LANES:b + 2 * LANES, :] = (
            _rope((qr * r) * gq_r, (qrs * r) * gq_rs, cos, sin_s) * s192).astype(BF16).T
        kn = kn_all[:, hd * LANES:(hd + 1) * LANES]
        rk = lax.rsqrt((jnp.sum(kn * kn, axis=-1, keepdims=True) + kr_ss) * (1.0 / MLA_QK) + EPS)
        km_ref[0, hd, :, :LANES] = ((kn * rk) * gk_n).astype(BF16)
        km_ref[0, hd, :, LANES:] = _rope((kr * rk) * gk_r, (krs * rk) * gk_rs, cos, sin_s).astype(BF16)
        store_vt(vtm_ref, hd, vm_all[:, hd * LANES:(hd + 1) * LANES])

    g_q, g_k = gt_ref[4:6, :], gt_ref[6:8, :]
    for c in range(4):
        sl = slice(c * LANES, (c + 1) * LANES)
        qtg_ref[0, sl, :] = (head_pair(A_GQ + c * LANES, A_GQS + c * LANES, g_q) * s64).astype(BF16).T
    kg_ref[0] = head_pair(A_GK, A_GKS, g_k).astype(BF16)
    vgt = xp[:, A_GV:A_GV + LANES].astype(BF16).T
    for g in range(GQA_HEADS // GQA_GROUP):
        vtg_ref[0, g * GQA_VT_ROWS:g * GQA_VT_ROWS + HEAD64, :] = vgt[g * HEAD64:(g + 1) * HEAD64]
        vtg_ref[0, g * GQA_VT_ROWS + HEAD64:(g + 1) * GQA_VT_ROWS, :] = ones_rows


def _prep_call(h, tail, cos, sin, ng, wa, wuq, wuk, wuv, gt, gcq, gckv):
    B, _, D = h.shape
    Lp = cos.shape[0]
    tm = _pick(Lp, (384, 128))
    row = lambda b, j: (b, j, 0)
    colt = lambda b, j: (b, 0, j)
    tab = pl.BlockSpec((tm, LANES), lambda b, j: (j, 0))
    sds = lambda *s: jax.ShapeDtypeStruct(s, BF16)
    vt_rows = 4 * VT_ROWS
    out_shape = [sds(B, 512, Lp), sds(B, 4, Lp, LANES), sds(B, vt_rows, Lp),
                 sds(B, 1024, Lp), sds(B, MLA_HEADS, Lp, 2 * LANES), sds(B, vt_rows, Lp),
                 sds(B, 512, Lp), sds(B, Lp, LANES), sds(B, 2 * GQA_VT_ROWS, Lp)]
    hrow = lambda b, j: (b, 0, j, 0)
    out_specs = [pl.BlockSpec((1, 512, tm), colt), pl.BlockSpec((1, 4, tm, LANES), hrow), pl.BlockSpec((1, vt_rows, tm), colt),
                 pl.BlockSpec((1, 1024, tm), colt), pl.BlockSpec((1, MLA_HEADS, tm, 2 * LANES), hrow),
                 pl.BlockSpec((1, vt_rows, tm), colt),
                 pl.BlockSpec((1, 512, tm), colt), pl.BlockSpec((1, tm, LANES), row),
                 pl.BlockSpec((1, 2 * GQA_VT_ROWS, tm), colt)]
    stream = [h] if tail is None else [h, tail]
    in_specs = [_stream_spec(h, tm)] + ([] if tail is None else [_resident(tail.shape)]) + [
        tab, tab, _resident(ng.shape), _resident(wa.shape), _resident(wuq.shape), _resident(wuk.shape),
        _resident(wuv.shape), _resident(gt.shape), _resident(gcq.shape), _resident(gckv.shape)]
    return pl.pallas_call(
        functools.partial(_prep_kernel, has_tail=tail is not None), out_shape=out_shape, grid=(B, Lp // tm),
        in_specs=in_specs, out_specs=out_specs,
        compiler_params=_cparams(("parallel", "parallel")), name="prep",
    )(*stream, cos, sin, ng, wa, wuq, wuk, wuv, gt, gcq, gckv)


def _dense_attn_kernel(*refs, n_real, tk, differential, lam_init, heads_per_step, tiles_per_step):
    if differential:
        qt_ref, k_ref, vt_ref, lam_ref, sg_ref, o_ref = refs[:6]
    else:
        (qt_ref, k_ref, vt_ref, o_ref), lam_ref, sg_ref = refs[:4], None, None
    s_refs = refs[len(refs) - heads_per_step * tiles_per_step:]
    dq = qt_ref.shape[1] // heads_per_step
    tq = qt_ref.shape[2] // tiles_per_step
    for hh in range(heads_per_step):
        for u in range(tiles_per_step):
            _attend_head(qt_ref.at[0, hh * dq:(hh + 1) * dq, u * tq:(u + 1) * tq], k_ref.at[0, hh],
                         vt_ref.at[0, hh * VT_ROWS:(hh + 1) * VT_ROWS], lam_ref, sg_ref,
                         o_ref.at[0, u * tq:(u + 1) * tq, hh * LANES:(hh + 1) * LANES], s_refs[hh * tiles_per_step + u],
                         n_real=n_real, tk=tk, differential=differential, lam_init=lam_init)


def _attend_head(qt_ref, k_ref, vt_ref, lam_ref, sg_ref, o_ref, s_ref, *, n_real, tk, differential, lam_init):
    tq = qt_ref.shape[1]
    dv = o_ref.shape[1]
    qt = qt_ref[...]
    if differential:
        z = jnp.zeros((HEAD64, tq), BF16)
        wq = jnp.concatenate([jnp.concatenate([qt[:HEAD64], z], axis=1),
                              jnp.concatenate([z, qt[HEAD64:]], axis=1)], axis=0)
    else:
        wq = qt
    width = wq.shape[1]

    def scores(kc):
        return jnp.dot(kc, wq, preferred_element_type=F32)

    def absorb(m, acc, s, vtc):
        m_new = jnp.maximum(m, jnp.max(s, axis=0, keepdims=True))
        p = jnp.exp2(s - m_new).astype(BF16)
        alpha = jnp.exp2(m - m_new)
        return m_new, alpha * acc + jnp.dot(vtc, p, preferred_element_type=F32)

    def kchunk(c):
        return k_ref[pl.ds(pl.multiple_of(c * tk, tk), tk), :]

    def vchunk(c):
        return vt_ref[:, pl.ds(pl.multiple_of(c * tk, tk), tk)]

    def group(j, carry):
        m, acc = carry
        for g in range(CHUNK_GROUP):
            c = CHUNK_GROUP * j + g
            s_ref[(g + 1) % 2] = scores(kchunk(c + 1))
            m, acc = absorb(m, acc, s_ref[g % 2], vchunk(c))
        return m, acc

    n_chunks = n_real // tk
    n_groups = (n_chunks - 1) // CHUNK_GROUP
    s_ref[0] = scores(kchunk(0))
    carry = (jnp.full((1, width), NEG, F32), jnp.zeros((dv + ONES_ROWS, width), F32))
    m, acc = lax.fori_loop(0, n_groups, group, carry)
    valid = lax.broadcasted_iota(jnp.int32, (TAIL, width), 0) < N_META
    first = CHUNK_GROUP * n_groups
    for c in range(first, n_chunks):
        if c + 1 < n_chunks:
            s_ref[(c - first + 1) % 2] = scores(kchunk(c + 1))
        else:
            s_tail = jnp.where(valid, scores(k_ref[n_real:n_real + TAIL, :]), NEG)
        m, acc = absorb(m, acc, s_ref[(c - first) % 2], vchunk(c))
    _, acc = absorb(m, acc, s_tail, vt_ref[:, n_real:n_real + TAIL])

    ot = acc[:dv] * (1.0 / acc[dv:dv + 1])
    if differential:
        lt = lam_ref[...]
        lam = (jnp.exp(jnp.sum(lt[0:1, :] * lt[1:2, :], axis=-1, keepdims=True))
               - jnp.exp(jnp.sum(lt[2:3, :] * lt[3:4, :], axis=-1, keepdims=True)) + lam_init)
        ot = ot[:, :tq] - lam * ot[:, tq:]
        r = lax.rsqrt(jnp.sum(ot * ot, axis=0, keepdims=True) * (1.0 / dv) + EPS)
        gain = jnp.concatenate([sg_ref[...]] * (tq // LANES), axis=1)
        ot = ((ot * r) * gain) * (1.0 - lam_init)
    o_ref[...] = ot.astype(o_ref.dtype).T


def _dense_attn_call(qt, k, vt, n_real, *, differential, lam_tab=None, sub_g=None, lam_init=0.0, name):
    B, heads, Lp, dq = k.shape
    if differential:
        tq = _pick(Lp, (384, 128))
    else:
        tq = LATENT_TQ if Lp >= LATENT_TQ else _pick(n_real, (512, 256, 128))
    tk = _pick(n_real, (512, 256, 128))
    width = (2 if differential else 1) * tq
    per_head = 2 * 2 * Lp * (dq + VT_ROWS) + 4 * 2 * tk * width
    hps = max(n for n in (4, 2, 1) if heads % n == 0 and (n == 1 or n * per_head <= ATTN_VMEM_BUDGET))
    per_head_single = 2 * Lp * (dq + VT_ROWS) + 4 * 2 * tk * width
    single = B == 1 and hps < heads and heads * per_head_single <= ATTN_VMEM_BUDGET_SINGLE
    hps = heads if single else hps
    tps = 2 if hps == 1 else 1
    kv_mode = dict(pipeline_mode=pl.Buffered(1)) if single else {}
    in_specs = [pl.BlockSpec((1, hps * dq, tps * tq), lambda b, h, i: (b, h, i)),
                pl.BlockSpec((1, hps, Lp, dq), lambda b, h, i: (b, h, 0, 0), **kv_mode),
                pl.BlockSpec((1, hps * VT_ROWS, Lp), lambda b, h, i: (b, h, 0), **kv_mode)]
    args = [qt, k, vt]
    if differential:
        in_specs += [_resident(lam_tab.shape), _resident(sub_g.shape)]
        args += [lam_tab, sub_g]
    kern = functools.partial(_dense_attn_kernel, n_real=n_real, tk=tk, differential=differential, lam_init=lam_init,
                             heads_per_step=hps, tiles_per_step=tps)
    return pl.pallas_call(
        kern, out_shape=jax.ShapeDtypeStruct((B, Lp, heads * LANES), BF16),
        grid=(B, heads // hps, pl.cdiv(Lp, tps * tq)),
        in_specs=in_specs, out_specs=pl.BlockSpec((1, tps * tq, hps * LANES), lambda b, h, i: (b, i, h)),
        scratch_shapes=[pltpu.VMEM((2, tk, width), F32)] * (hps * tps),
        compiler_params=_cparams(("parallel", "parallel", "parallel")), name=name,
    )(*args)


def _window_attn_kernel(sink_ref, qt_ref, k_ref, vt_ref, bias_ref, o_ref, *, n_real):
    tq = qt_ref.shape[2]
    t = pl.program_id(1)
    is_meta = t == n_real // tq
    span = tq + 2 * BLOCK
    start = pl.multiple_of(jnp.where(is_meta, 0, jnp.maximum(t * tq - BLOCK, 0)), BLOCK)
    kwin = jnp.concatenate([k_ref[0, pl.ds(start, span), :], k_ref[0, n_real:n_real + TAIL, :]], axis=0)
    vwin = jnp.concatenate([vt_ref[0, :, pl.ds(start, span)], vt_ref[0, :, n_real:n_real + TAIL]], axis=1)
    n_tiles = n_real // tq
    kind = jnp.where(is_meta, 4, (t == 0).astype(jnp.int32) + 2 * (t == n_tiles - 1).astype(jnp.int32))
    bias = jnp.concatenate([bias_ref[kind]] * GQA_GROUP, axis=1)
    qt = qt_ref[0]
    n_groups = GQA_HEADS // GQA_GROUP
    gw = GQA_GROUP * tq
    z = jnp.zeros((HEAD64, gw), BF16)
    rows = []
    for g in range(n_groups):
        qrow = jnp.concatenate([qt[h * HEAD64:(h + 1) * HEAD64] for h in range(g * GQA_GROUP, (g + 1) * GQA_GROUP)], axis=1)
        rows.append(jnp.concatenate([qrow if i == g else z for i in range(n_groups)], axis=1))
    wq = jnp.concatenate(rows, axis=0)
    sink = jnp.concatenate([jnp.full((1, tq), sink_ref[h] * LOG2E, F32) for h in range(GQA_HEADS)], axis=1)
    s = jnp.dot(kwin, wq, preferred_element_type=F32) + jnp.concatenate([bias, bias], axis=1)
    m = jnp.maximum(jnp.max(s, axis=0, keepdims=True), sink)
    p = jnp.exp2(s - m).astype(BF16)
    extra = jnp.exp2(sink - m)
    for g in range(n_groups):
        acc = jnp.dot(vwin[g * GQA_VT_ROWS:(g + 1) * GQA_VT_ROWS], p[:, g * gw:(g + 1) * gw], preferred_element_type=F32)
        ot = acc[:HEAD64] * (1.0 / (acc[HEAD64:HEAD64 + 1] + extra[:, g * gw:(g + 1) * gw]))
        for i in range(GQA_GROUP // 2):
            pair = jnp.concatenate([ot[:, 2 * i * tq:(2 * i + 1) * tq], ot[:, (2 * i + 1) * tq:(2 * i + 2) * tq]], axis=0)
            c = g * (GQA_GROUP // 2) + i
            o_ref[0, :, c * LANES:(c + 1) * LANES] = pair.T.astype(o_ref.dtype)


def _window_bias(n_real, tq):
    span = tq + 2 * BLOCK
    ks = jnp.arange(span + TAIL, dtype=jnp.int32)[:, None]
    qi = jnp.arange(tq, dtype=jnp.int32)[None, :]
    in_span = ks < span
    tail_ok = (ks >= span) & (ks < span + N_META)
    n_tiles = n_real // tq
    tables = []
    for first, last in ((False, False), (True, False), (False, True), (True, True)):
        t = 0 if first else (n_tiles - 1 if last else 1)
        kpos = max(t * tq - BLOCK, 0) + ks
        dist = kpos - (t * tq + qi)
        tables.append((dist <= WINDOW) & (dist >= -WINDOW) & in_span & (kpos < n_real) | tail_ok)
    tables.append(in_span & (ks <= qi + (WINDOW - N_META)) | tail_ok)
    return jnp.where(jnp.stack(tables), 0.0, NEG).astype(F32)


def _window_attn_call(qt, k, vt, sink, n_real):
    B, Lp, _ = k.shape
    tq = 256 if n_real % 256 == 0 and n_real >= 384 else BLOCK
    assert n_real % tq == 0 and tq + 2 * BLOCK <= Lp
    bias = _window_bias(n_real, tq)
    grid_spec = pltpu.PrefetchScalarGridSpec(
        num_scalar_prefetch=1, grid=(B, n_real // tq + 1),
        in_specs=[pl.BlockSpec((1, GQA_HEADS * HEAD64, tq), lambda b, t, s: (b, 0, t)),
                  pl.BlockSpec((1, Lp, LANES), lambda b, t, s: (b, 0, 0)),
                  pl.BlockSpec((1, 2 * GQA_VT_ROWS, Lp), lambda b, t, s: (b, 0, 0)),
                  _resident(bias.shape)],
        out_specs=pl.BlockSpec((1, tq, 512), lambda b, t, s: (b, t, 0)))
    return pl.pallas_call(
        functools.partial(_window_attn_kernel, n_real=n_real),
        out_shape=jax.ShapeDtypeStruct((B, Lp, 512), BF16), grid_spec=grid_spec,
        compiler_params=_cparams(("parallel", "parallel")), name="window_attn",
    )(sink, qt, k, vt, bias)


def _post_kernel(*refs, has_tail):
    h_ref, tail_ref = (refs[0], refs[1]) if has_tail else (refs[0], None)
    oda_ref, om_ref, og_ref, ng_ref, wg_ref, wb_ref, wo_ref, mg_ref, wu_ref, wd_ref, out_ref = refs[2 if has_tail else 1:]
    x = _row_tile(h_ref, tail_ref)
    xn = ((x * _rms_scale(x, D_MODEL)) * ng_ref[...]).astype(BF16)
    gl = jnp.dot(xn, wg_ref[...], preferred_element_type=F32)
    merged = None
    for g, o_ref in enumerate((oda_ref, om_ref, og_ref)):
        proj = jnp.dot(o_ref[0], wb_ref[g], preferred_element_type=F32)
        term = jax.nn.sigmoid(gl[:, g * D_MODEL:(g + 1) * D_MODEL]) * proj
        merged = term if merged is None else merged + term
    h1 = x + jnp.dot(merged.astype(BF16), wo_ref[...], preferred_element_type=F32)
    h1n = ((h1 * _rms_scale(h1, D_MODEL)) * mg_ref[...]).astype(BF16)
    u = jnp.dot(h1n, wu_ref[...], preferred_element_type=F32)
    a = jnp.square(jnp.maximum(u, 0.0)).astype(BF16)
    out_ref[0] = h1 + jnp.dot(a, wd_ref[...], preferred_element_type=F32)


def _post_call(h, tail, oda, om, og, ng, wg, wb, wo, mg, wu, wd, out_rows):
    B, Lp, _ = oda.shape
    D = h.shape[2]
    tm = _pick(Lp, (POST_TM, 128))
    row = lambda b, j: (b, j, 0)
    stream = [h] if tail is None else [h, tail]
    in_specs = [_stream_spec(h, tm)] + ([] if tail is None else [_resident(tail.shape)]) + [
        pl.BlockSpec((1, tm, BRANCH_WIDTH), row)] * 3 + [_resident(a.shape) for a in (ng, wg, wb, wo, mg, wu, wd)]
    return pl.pallas_call(
        functools.partial(_post_kernel, has_tail=tail is not None),
        out_shape=jax.ShapeDtypeStruct((B, out_rows, D), F32), grid=(B, pl.cdiv(out_rows, tm)),
        in_specs=in_specs, out_specs=pl.BlockSpec((1, tm, D), row),
        compiler_params=_cparams(("parallel", "parallel")), name="post",
    )(*stream, oda, om, og, ng, wg, wb, wo, mg, wu, wd)


def _swap_halves(m):
    return jnp.flip(m.reshape(m.shape[:-1] + (m.shape[-1] // HEAD64, 2, HEAD64 // 2)), axis=-2).reshape(m.shape)


def _pair_gain(g):
    g = jnp.tile(g.astype(F32), 2)[None, :]
    return jnp.concatenate([g, _swap_halves(g)], axis=0)


def _pad_gain(g):
    return jnp.concatenate([g.astype(F32), jnp.zeros((LANES - g.shape[0],), F32)])[None, :]


def _layer_params(l, attn_norm_g, w_in, da_q_norm_g, da_k_norm_g, da_lam_q1, da_lam_k1, da_lam_q2, da_lam_k2,
                  da_subln_g, mla_cq_norm_g, mla_ckv_norm_g, mla_w_uq, mla_w_ukv, mla_q_norm_g, mla_k_norm_g,
                  gqa_q_norm_g, gqa_k_norm_g, gqa_sink, w_branch, w_out, mlp_norm_g, w_up, w_down):
    w = w_in[l]
    sec = [w[:, a:b] for a, b in zip(IN_OFF[:-1], IN_OFF[1:])]
    da_q, da_k, da_v, cq, ckv, kr, gq, gk, gv, gate = sec
    pad64 = lambda m: jnp.pad(m, ((0, 0),) * (m.ndim - 1) + ((0, LANES - HEAD64),))
    wa = jnp.concatenate([da_q, da_k, da_v, cq, ckv,
                          pad64(kr), pad64(_swap_halves(kr)), gq, _swap_halves(gq),
                          gk, _swap_halves(gk), gv], axis=1).astype(BF16)
    wuq = mla_w_uq[l].reshape(MLA_Q_RANK, MLA_HEADS, MLA_QK)
    wuq_r = wuq[:, :, MLA_NOPE:]
    wuq = jnp.concatenate([wuq[:, :, :MLA_NOPE], pad64(wuq_r), pad64(_swap_halves(wuq_r))], axis=-1)
    wuq = wuq.reshape(MLA_Q_RANK, MLA_HEADS * 3 * LANES).astype(BF16)
    wukv = mla_w_ukv[l].reshape(MLA_KV_RANK, MLA_HEADS, 2 * LANES)
    wuk = wukv[:, :, :MLA_NOPE].reshape(MLA_KV_RANK, MLA_HEADS * LANES).astype(BF16)
    wuv = wukv[:, :, MLA_NOPE:].reshape(MLA_KV_RANK, MLA_HEADS * LANES).astype(BF16)
    mq, mk = mla_q_norm_g[l], mla_k_norm_g[l]
    gt = jnp.concatenate([_pair_gain(da_q_norm_g[l]), _pair_gain(da_k_norm_g[l]),
                          _pair_gain(gqa_q_norm_g[l]), _pair_gain(gqa_k_norm_g[l]),
                          mq[None, :MLA_NOPE].astype(F32), _pad_gain(mq[MLA_NOPE:]), _pad_gain(_swap_halves(mq[MLA_NOPE:])),
                          mk[None, :MLA_NOPE].astype(F32), _pad_gain(mk[MLA_NOPE:]), _pad_gain(_swap_halves(mk[MLA_NOPE:])),
                          jnp.zeros((2, LANES), F32)], axis=0)
    lam_tab = jnp.concatenate([_pad_gain(v[l]) for v in (da_lam_q1, da_lam_k1, da_lam_q2, da_lam_k2)], axis=0)
    return dict(
        ng=attn_norm_g[l][None, :].astype(F32), wa=wa, wuq=wuq, wuk=wuk, wuv=wuv, gt=gt,
        gcq=mla_cq_norm_g[l][None, :].astype(F32), gckv=mla_ckv_norm_g[l][None, :].astype(F32),
        lam_tab=lam_tab, sub_g=jnp.broadcast_to(da_subln_g[l].astype(F32)[:, None], (LANES, LANES)),
        sink=gqa_sink[l].astype(F32),
        wg=gate.astype(BF16), wb=w_branch[l].astype(BF16), wo=w_out[l].astype(BF16),
        mg=mlp_norm_g[l][None, :].astype(F32), wu=w_up[l].astype(BF16), wd=w_down[l].astype(BF16))


def _rope_tables(n_real):
    r = jnp.arange(n_real + TAIL, dtype=jnp.int32)
    pos = jnp.where(r < n_real, r + N_META, jnp.where(r < n_real + N_META, r - n_real, 0)).astype(F32)
    inv_freq = 1.0 / (ROPE_THETA ** (jnp.arange(0, HEAD64, 2, dtype=F32) / HEAD64))
    ang = pos[:, None] * inv_freq[None, :]
    cos, sin = jnp.cos(ang), jnp.sin(ang)
    return jnp.tile(cos, (1, 4)), jnp.tile(jnp.concatenate([-sin, sin], axis=1), (1, 2))


def _trunk(x, meta_tokens, layers):
    B, S, D = x.shape
    assert D == D_MODEL and S % BLOCK == 0
    tail = jnp.concatenate([meta_tokens.astype(x.dtype), jnp.zeros((TAIL - N_META, D), x.dtype)], axis=0)
    h = x
    cos, sin = _rope_tables(S)
    for l, p in enumerate(layers):
        lam_init = 0.8 - 0.6 * math.exp(-0.3 * l)
        qtda, kda, vtda, qtm, km, vtm, qtg, kg, vtg = _prep_call(
            h, tail, cos, sin, p["ng"], p["wa"], p["wuq"], p["wuk"], p["wuv"], p["gt"], p["gcq"], p["gckv"])
        oda = _dense_attn_call(qtda, kda, vtda, S, differential=True, lam_tab=p["lam_tab"], sub_g=p["sub_g"],
                               lam_init=lam_init, name="diff_attn")
        om = _dense_attn_call(qtm, km, vtm, S, differential=False, name="latent_attn")
        og = _window_attn_call(qtg, kg, vtg, p["sink"], S)
        h = _post_call(h, tail, oda, om, og, p["ng"], p["wg"], p["wb"], p["wo"], p["mg"], p["wu"], p["wd"],
                       out_rows=S if l == len(layers) - 1 else S + TAIL)
        tail = None
    return h


def kernel(x_prompt, x_sample, meta_tokens, attn_norm_g, w_in, da_q_norm_g, da_k_norm_g, da_lam_q1, da_lam_k1, da_lam_q2, da_lam_k2, da_subln_g, mla_cq_norm_g, mla_ckv_norm_g, mla_w_uq, mla_w_ukv, mla_q_norm_g, mla_k_norm_g, gqa_q_norm_g, gqa_k_norm_g, gqa_sink, w_branch, w_out, mlp_norm_g, w_up, w_down):
    depth = w_in.shape[0]
    layers = [_layer_params(l, attn_norm_g, w_in, da_q_norm_g, da_k_norm_g, da_lam_q1, da_lam_k1, da_lam_q2,
                            da_lam_k2, da_subln_g, mla_cq_norm_g, mla_ckv_norm_g, mla_w_uq, mla_w_ukv,
                            mla_q_norm_g, mla_k_norm_g, gqa_q_norm_g, gqa_k_norm_g, gqa_sink, w_branch, w_out,
                            mlp_norm_g, w_up, w_down) for l in range(depth)]
    return (_trunk(x_prompt, meta_tokens, layers), _trunk(x_sample, meta_tokens, layers))
```

```python
import functools
import math

import jax
import jax.numpy as jnp
from jax import lax
from jax.experimental import pallas as pl
from jax.experimental.pallas import tpu as pltpu

F32 = jnp.float32
BF16 = jnp.bfloat16

D_MODEL = 1024
N_META = 16
BLOCK = 128
WINDOW = 128
ROPE_THETA = 10000.0
EPS = 1e-6
NEG = -1e30
LOG2E = 1.4426950408889634

HEAD64 = 64
MLA_HEADS = 4
MLA_NOPE = 128
MLA_QK = 192
MLA_Q_RANK = 384
MLA_KV_RANK = 256
GQA_HEADS = 8
GQA_GROUP = 4
BRANCH_WIDTH = 512
LANES = 128
TAIL = 128
ONES_ROWS = 16
VT_ROWS = LANES + ONES_ROWS
GQA_VT_ROWS = HEAD64 + ONES_ROWS
CHUNK_GROUP = 8
POST_TM = 384
LATENT_TQ = 768

A_DAQ, A_DAK, A_DAV, A_CQ, A_CKV, A_KR, A_KRS, A_GQ, A_GQS, A_GK, A_GKS, A_GV, A_END = (
    0, 512, 1024, 1536, 1920, 2176, 2304, 2432, 2944, 3456, 3584, 3712, 3840)
IN_OFF = (0, 512, 1024, 1536, 1920, 2176, 2240, 2752, 2880, 3008, 6080)

VMEM_LIMIT = 56 * 1024 * 1024
ATTN_VMEM_BUDGET = 44 * 1024 * 1024


def _pick(n, candidates):
    for c in candidates:
        if n % c == 0:
            return c
    raise ValueError(f"no tile in {candidates} divides {n}")


def _cparams(sem):
    return pltpu.CompilerParams(dimension_semantics=sem, vmem_limit_bytes=VMEM_LIMIT)


def _resident(shape):
    zeros = (0,) * len(shape)
    return pl.BlockSpec(shape, lambda *_: zeros, pipeline_mode=pl.Buffered(1))


def _rms_scale(x, width):
    return lax.rsqrt(jnp.sum(x * x, axis=-1, keepdims=True) * (1.0 / width) + EPS)


def _pair_rms_scale(xc, lo_mask):
    sq = xc * xc
    lo = jnp.sum(jnp.where(lo_mask, sq, 0.0), axis=-1, keepdims=True)
    hi = jnp.sum(jnp.where(lo_mask, 0.0, sq), axis=-1, keepdims=True)
    return jnp.where(lo_mask, lax.rsqrt(lo * (1.0 / HEAD64) + EPS), lax.rsqrt(hi * (1.0 / HEAD64) + EPS))


def _rope(xc, swapped, cos, sin_signed):
    return xc * cos + swapped * sin_signed


def _stream_spec(h, tm):
    last = (h.shape[1] - 1) // tm
    return pl.BlockSpec((1, tm, h.shape[2]), lambda b, j: (b, jnp.minimum(j, last), 0))


def _row_tile(h_ref, tail_ref):
    x = h_ref[0]
    if tail_ref is None:
        return x
    tm = x.shape[0]
    tail = jnp.concatenate([jnp.zeros((tm - TAIL, x.shape[1]), x.dtype), tail_ref[...]], axis=0)
    in_tail = lax.broadcasted_iota(jnp.int32, x.shape, 0) >= tm - TAIL
    is_last = pl.program_id(1) == pl.num_programs(1) - 1
    return jnp.where(jnp.logical_and(is_last, in_tail), tail, x)


def _prep_kernel(*refs, has_tail):
    h_ref, tail_ref = (refs[0], refs[1]) if has_tail else (refs[0], None)
    (cos_ref, sin_ref, ng_ref, wa_ref, wuq_ref, wuk_ref, wuv_ref, gt_ref, gcq_ref, gckv_ref,
     qtda_ref, kda_ref, vtda_ref, qtm_ref, km_ref, vtm_ref, qtg_ref, kg_ref, vtg_ref) = refs[2 if has_tail else 1:]
    x = _row_tile(h_ref, tail_ref)
    xn = ((x * _rms_scale(x, D_MODEL)) * ng_ref[...]).astype(BF16)
    xp = jnp.dot(xn, wa_ref[...], preferred_element_type=F32)
    cos = cos_ref[...]
    sin_s = sin_ref[...]
    lane = lax.broadcasted_iota(jnp.int32, cos.shape, 1)
    lo64 = lane < HEAD64
    s64 = (HEAD64 ** -0.5) * LOG2E
    s192 = (MLA_QK ** -0.5) * LOG2E

    first_half = (lane & 32) == 0

    def head_pair(col, col_swapped, gains):
        xc = xp[:, col:col + LANES]
        r = _pair_rms_scale(xc, lo64)
        y = (xc * r) * gains[0:1, :]
        if col_swapped is None:
            ys = jnp.where(first_half, pltpu.roll(y, LANES - 32, 1), pltpu.roll(y, 32, 1))
        else:
            ys = (xp[:, col_swapped:col_swapped + LANES] * r) * gains[1:2, :]
        return _rope(y, ys, cos, sin_s)

    ones_rows = jnp.ones((ONES_ROWS, x.shape[0]), BF16)

    def store_vt(vt_ref, hd, v_slab):
        vt_ref[0, hd * VT_ROWS:hd * VT_ROWS + LANES, :] = v_slab.astype(BF16).T
        vt_ref[0, hd * VT_ROWS + LANES:(hd + 1) * VT_ROWS, :] = ones_rows

    g_q, g_k = gt_ref[0:2, :], gt_ref[2:4, :]
    for c in range(4):
        sl = slice(c * LANES, (c + 1) * LANES)
        qtda_ref[0, sl, :] = (head_pair(A_DAQ + c * LANES, None, g_q) * s64).astype(BF16).T
        kda_ref[0, c] = head_pair(A_DAK + c * LANES, None, g_k).astype(BF16)
        store_vt(vtda_ref, c, xp[:, A_DAV + c * LANES:A_DAV + (c + 1) * LANES])

    cq = xp[:, A_CQ:A_CQ + MLA_Q_RANK]
    cqn = ((cq * _rms_scale(cq, MLA_Q_RANK)) * gcq_ref[...]).astype(BF16)
    qm = jnp.dot(cqn, wuq_ref[...], preferred_element_type=F32)
    ckv = xp[:, A_CKV:A_CKV + MLA_KV_RANK]
    ckvn = ((ckv * _rms_scale(ckv, MLA_KV_RANK)) * gckv_ref[...]).astype(BF16)
    kn_all = jnp.dot(ckvn, wuk_ref[...], preferred_element_type=F32)
    vm_all = jnp.dot(ckvn, wuv_ref[...], preferred_element_type=F32)
    kr = xp[:, A_KR:A_KR + LANES]
    krs = xp[:, A_KRS:A_KRS + LANES]
    kr_ss = jnp.sum(kr * kr, axis=-1, keepdims=True)
    gq_n, gq_r, gq_rs = gt_ref[8:9, :], gt_ref[9:10, :], gt_ref[10:11, :]
    gk_n, gk_r, gk_rs = gt_ref[11:12, :], gt_ref[12:13, :], gt_ref[13:14, :]
    for hd in range(MLA_HEADS):
        b = hd * 2 * LANES
        qb = hd * 3 * LANES
        qn = qm[:, qb:qb + LANES]
        qr = qm[:, qb + LANES:qb + 2 * LANES]
        qrs = qm[:, qb + 2 * LANES:qb + 3 * LANES]
        r = lax.rsqrt((jnp.sum(qn * qn, axis=-1, keepdims=True) + jnp.sum(qr * qr, axis=-1, keepdims=True))
                      * (1.0 / MLA_QK) + EPS)
        qtm_ref[0, b:b + LANES, :] = (((qn * r) * gq_n) * s192).astype(BF16).T
        qtm_ref[0, b + LANES:b + 2 * LANES, :] = (
            _rope((qr * r) * gq_r, (qrs * r) * gq_rs, cos, sin_s) * s192).astype(BF16).T
        kn = kn_all[:, hd * LANES:(hd + 1) * LANES]
        rk = lax.rsqrt((jnp.sum(kn * kn, axis=-1, keepdims=True) + kr_ss) * (1.0 / MLA_QK) + EPS)
        km_ref[0, hd, :, :LANES] = ((kn * rk) * gk_n).astype(BF16)
        km_ref[0, hd, :, LANES:] = _rope((kr * rk) * gk_r, (krs * rk) * gk_rs, cos, sin_s).astype(BF16)
        store_vt(vtm_ref, hd, vm_all[:, hd * LANES:(hd + 1) * LANES])

    g_q, g_k = gt_ref[4:6, :], gt_ref[6:8, :]
    for c in range(4):
        sl = slice(c * LANES, (c + 1) * LANES)
        qtg_ref[0, sl, :] = (head_pair(A_GQ + c * LANES, A_GQS + c * LANES, g_q) * s64).astype(BF16).T
    kg_ref[0] = head_pair(A_GK, A_GKS, g_k).astype(BF16)
    vgt = xp[:, A_GV:A_GV + LANES].astype(BF16).T
    for g in range(GQA_HEADS // GQA_GROUP):
        vtg_ref[0, g * GQA_VT_ROWS:g * GQA_VT_ROWS + HEAD64, :] = vgt[g * HEAD64:(g + 1) * HEAD64]
        vtg_ref[0, g * GQA_VT_ROWS + HEAD64:(g + 1) * GQA_VT_ROWS, :] = ones_rows


def _prep_call(h, tail, cos, sin, ng, wa, wuq, wuk, wuv, gt, gcq, gckv):
    B, _, D = h.shape
    Lp = cos.shape[0]
    tm = _pick(Lp, (384, 128))
    row = lambda b, j: (b, j, 0)
    colt = lambda b, j: (b, 0, j)
    tab = pl.BlockSpec((tm, LANES), lambda b, j: (j, 0))
    sds = lambda *s: jax.ShapeDtypeStruct(s, BF16)
    vt_rows = 4 * VT_ROWS
    out_shape = [sds(B, 512, Lp), sds(B, 4, Lp, LANES), sds(B, vt_rows, Lp),
                 sds(B, 1024, Lp), sds(B, MLA_HEADS, Lp, 2 * LANES), sds(B, vt_rows, Lp),
                 sds(B, 512, Lp), sds(B, Lp, LANES), sds(B, 2 * GQA_VT_ROWS, Lp)]
    hrow = lambda b, j: (b, 0, j, 0)
    out_specs = [pl.BlockSpec((1, 512, tm), colt), pl.BlockSpec((1, 4, tm, LANES), hrow), pl.BlockSpec((1, vt_rows, tm), colt),
                 pl.BlockSpec((1, 1024, tm), colt), pl.BlockSpec((1, MLA_HEADS, tm, 2 * LANES), hrow),
                 pl.BlockSpec((1, vt_rows, tm), colt),
                 pl.BlockSpec((1, 512, tm), colt), pl.BlockSpec((1, tm, LANES), row),
                 pl.BlockSpec((1, 2 * GQA_VT_ROWS, tm), colt)]
    stream = [h] if tail is None else [h, tail]
    in_specs = [_stream_spec(h, tm)] + ([] if tail is None else [_resident(tail.shape)]) + [
        tab, tab, _resident(ng.shape), _resident(wa.shape), _resident(wuq.shape), _resident(wuk.shape),
        _resident(wuv.shape), _resident(gt.shape), _resident(gcq.shape), _resident(gckv.shape)]
    return pl.pallas_call(
        functools.partial(_prep_kernel, has_tail=tail is not None), out_shape=out_shape, grid=(B, Lp // tm),
        in_specs=in_specs, out_specs=out_specs,
        compiler_params=_cparams(("parallel", "parallel")), name="prep",
    )(*stream, cos, sin, ng, wa, wuq, wuk, wuv, gt, gcq, gckv)


def _dense_attn_kernel(*refs, n_real, tk, differential, lam_init, heads_per_step, tiles_per_step, aliased):
    if differential:
        qt_ref, k_ref, vt_ref, lam_ref, sg_ref, o_ref = refs[:6]
    else:
        (qt_ref, k_ref, vt_ref), o_ref, lam_ref, sg_ref = refs[:3], refs[4 if aliased else 3], None, None
    s_refs = refs[len(refs) - heads_per_step * tiles_per_step:]
    dq = qt_ref.shape[1] // heads_per_step
    tq = qt_ref.shape[2] // tiles_per_step
    for hh in range(heads_per_step):
        for u in range(tiles_per_step):
            _attend_head(qt_ref.at[0, hh * dq:(hh + 1) * dq, u * tq:(u + 1) * tq], k_ref.at[0, hh],
                         vt_ref.at[0, hh * VT_ROWS:(hh + 1) * VT_ROWS], lam_ref, sg_ref,
                         o_ref.at[0, u * tq:(u + 1) * tq, hh * LANES:(hh + 1) * LANES], s_refs[hh * tiles_per_step + u],
                         n_real=n_real, tk=tk, differential=differential, lam_init=lam_init)


def _attend_head(qt_ref, k_ref, vt_ref, lam_ref, sg_ref, o_ref, s_ref, *, n_real, tk, differential, lam_init):
    tq = qt_ref.shape[1]
    dv = o_ref.shape[1]
    qt = qt_ref[...]
    if differential:
        z = jnp.zeros((HEAD64, tq), BF16)
        wq = jnp.concatenate([jnp.concatenate([qt[:HEAD64], z], axis=1),
                              jnp.concatenate([z, qt[HEAD64:]], axis=1)], axis=0)
    else:
        wq = qt
    width = wq.shape[1]

    def scores(kc):
        return jnp.dot(kc, wq, preferred_element_type=F32)

    def absorb(m, acc, s, vtc):
        m_new = jnp.maximum(m, jnp.max(s, axis=0, keepdims=True))
        p = jnp.exp2(s - m_new).astype(BF16)
        alpha = jnp.exp2(m - m_new)
        return m_new, alpha * acc + jnp.dot(vtc, p, preferred_element_type=F32)

    def kchunk(c):
        return k_ref[pl.ds(pl.multiple_of(c * tk, tk), tk), :]

    def vchunk(c):
        return vt_ref[:, pl.ds(pl.multiple_of(c * tk, tk), tk)]

    def group(j, carry):
        m, acc = carry
        for g in range(CHUNK_GROUP):
            c = CHUNK_GROUP * j + g
            s_ref[(g + 1) % 2] = scores(kchunk(c + 1))
            m, acc = absorb(m, acc, s_ref[g % 2], vchunk(c))
        return m, acc

    n_chunks = n_real // tk
    n_groups = (n_chunks - 1) // CHUNK_GROUP
    s_ref[0] = scores(kchunk(0))
    carry = (jnp.full((1, width), NEG, F32), jnp.zeros((dv + ONES_ROWS, width), F32))
    m, acc = lax.fori_loop(0, n_groups, group, carry)
    valid = lax.broadcasted_iota(jnp.int32, (TAIL, width), 0) < N_META
    first = CHUNK_GROUP * n_groups
    for c in range(first, n_chunks):
        if c + 1 < n_chunks:
            s_ref[(c - first + 1) % 2] = scores(kchunk(c + 1))
        else:
            s_tail = jnp.where(valid, scores(k_ref[n_real:n_real + TAIL, :]), NEG)
        m, acc = absorb(m, acc, s_ref[(c - first) % 2], vchunk(c))
    _, acc = absorb(m, acc, s_tail, vt_ref[:, n_real:n_real + TAIL])

    ot = acc[:dv] * (1.0 / acc[dv:dv + 1])
    if differential:
        lt = lam_ref[...]
        lam = (jnp.exp(jnp.sum(lt[0:1, :] * lt[1:2, :], axis=-1, keepdims=True))
               - jnp.exp(jnp.sum(lt[2:3, :] * lt[3:4, :], axis=-1, keepdims=True)) + lam_init)
        ot = ot[:, :tq] - lam * ot[:, tq:]
        r = lax.rsqrt(jnp.sum(ot * ot, axis=0, keepdims=True) * (1.0 / dv) + EPS)
        gain = jnp.concatenate([sg_ref[...]] * (tq // LANES), axis=1)
        ot = ((ot * r) * gain) * (1.0 - lam_init)
    o_ref[...] = ot.astype(o_ref.dtype).T


def _dense_attn_call(qt, k, vt, n_real, *, differential, lam_tab=None, sub_g=None, lam_init=0.0, name):
    B, heads, Lp, dq = k.shape
    if differential:
        tq = _pick(Lp, (384, 128))
    else:
        tq = LATENT_TQ if Lp >= LATENT_TQ else _pick(n_real, (512, 256, 128))
    tk = _pick(n_real, (512, 256, 128))
    rest = Lp % tq
    split = rest > 0 and rest % LANES == 0 and (Lp - rest) % rest == 0

    def call(tq, first_tile, n_tiles, prev):
        width = (2 if differential else 1) * tq
        per_head = 2 * 2 * Lp * (dq + VT_ROWS) + 4 * 2 * tk * width
        hps = max(n for n in (4, 2, 1) if heads % n == 0 and (n == 1 or n * per_head <= ATTN_VMEM_BUDGET))
        if hps > 1:
            tps = 1
        elif split:
            tps = max(n for n in (3, 2, 1) if n_tiles % n == 0)
        else:
            tps = 2
        off = first_tile // tps
        in_specs = [pl.BlockSpec((1, hps * dq, tps * tq), lambda b, h, i: (b, h, i + off)),
                    pl.BlockSpec((1, hps, Lp, dq), lambda b, h, i: (b, h, 0, 0)),
                    pl.BlockSpec((1, hps * VT_ROWS, Lp), lambda b, h, i: (b, h, 0))]
        args = [qt, k, vt]
        if differential:
            in_specs += [_resident(lam_tab.shape), _resident(sub_g.shape)]
            args += [lam_tab, sub_g]
        aliases = {}
        if prev is not None:
            in_specs.append(pl.BlockSpec(memory_space=pl.ANY))
            aliases = {len(args): 0}
            args.append(prev)
        kern = functools.partial(_dense_attn_kernel, n_real=n_real, tk=tk, differential=differential, lam_init=lam_init,
                                 heads_per_step=hps, tiles_per_step=tps, aliased=prev is not None)
        return pl.pallas_call(
            kern, out_shape=jax.ShapeDtypeStruct((B, Lp, heads * LANES), BF16),
            grid=(B, heads // hps, pl.cdiv(n_tiles, tps)),
            in_specs=in_specs, out_specs=pl.BlockSpec((1, tps * tq, hps * LANES), lambda b, h, i: (b, i + off, h)),
            scratch_shapes=[pltpu.VMEM((2, tk, width), F32)] * (hps * tps), input_output_aliases=aliases,
            compiler_params=_cparams(("parallel", "parallel", "parallel")), name=name,
        )(*args)

    if not split:
        return call(tq, 0, pl.cdiv(Lp, tq), None)
    out = call(tq, 0, Lp // tq, None)
    return call(rest, (Lp - rest) // rest, 1, out)


def _window_attn_kernel(sink_ref, qt_ref, k_ref, vt_ref, bias_ref, o_ref, *, n_real):
    tq = qt_ref.shape[2]
    t = pl.program_id(1)
    is_meta = t == n_real // tq
    span = tq + 2 * BLOCK
    start = pl.multiple_of(jnp.where(is_meta, 0, jnp.maximum(t * tq - BLOCK, 0)), BLOCK)
    kwin = jnp.concatenate([k_ref[0, pl.ds(start, span), :], k_ref[0, n_real:n_real + TAIL, :]], axis=0)
    vwin = jnp.concatenate([vt_ref[0, :, pl.ds(start, span)], vt_ref[0, :, n_real:n_real + TAIL]], axis=1)
    n_tiles = n_real // tq
    kind = jnp.where(is_meta, 4, (t == 0).astype(jnp.int32) + 2 * (t == n_tiles - 1).astype(jnp.int32))
    bias = jnp.concatenate([bias_ref[kind]] * GQA_GROUP, axis=1)
    qt = qt_ref[0]
    n_groups = GQA_HEADS // GQA_GROUP
    gw = GQA_GROUP * tq
    z = jnp.zeros((HEAD64, gw), BF16)
    rows = []
    for g in range(n_groups):
        qrow = jnp.concatenate([qt[h * HEAD64:(h + 1) * HEAD64] for h in range(g * GQA_GROUP, (g + 1) * GQA_GROUP)], axis=1)
        rows.append(jnp.concatenate([qrow if i == g else z for i in range(n_groups)], axis=1))
    wq = jnp.concatenate(rows, axis=0)
    sink = jnp.concatenate([jnp.full((1, tq), sink_ref[h] * LOG2E, F32) for h in range(GQA_HEADS)], axis=1)
    s = jnp.dot(kwin, wq, preferred_element_type=F32) + jnp.concatenate([bias, bias], axis=1)
    m = jnp.maximum(jnp.max(s, axis=0, keepdims=True), sink)
    p = jnp.exp2(s - m).astype(BF16)
    extra = jnp.exp2(sink - m)
    for g in range(n_groups):
        acc = jnp.dot(vwin[g * GQA_VT_ROWS:(g + 1) * GQA_VT_ROWS], p[:, g * gw:(g + 1) * gw], preferred_element_type=F32)
        ot = acc[:HEAD64] * (1.0 / (acc[HEAD64:HEAD64 + 1] + extra[:, g * gw:(g + 1) * gw]))
        for i in range(GQA_GROUP // 2):
            pair = jnp.concatenate([ot[:, 2 * i * tq:(2 * i + 1) * tq], ot[:, (2 * i + 1) * tq:(2 * i + 2) * tq]], axis=0)
            c = g * (GQA_GROUP // 2) + i
            o_ref[0, :, c * LANES:(c + 1) * LANES] = pair.T.astype(o_ref.dtype)


def _window_bias(n_real, tq):
    span = tq + 2 * BLOCK
    ks = jnp.arange(span + TAIL, dtype=jnp.int32)[:, None]
    qi = jnp.arange(tq, dtype=jnp.int32)[None, :]
    in_span = ks < span
    tail_ok = (ks >= span) & (ks < span + N_META)
    n_tiles = n_real // tq
    tables = []
    for first, last in ((False, False), (True, False), (False, True), (True, True)):
        t = 0 if first else (n_tiles - 1 if last else 1)
        kpos = max(t * tq - BLOCK, 0) + ks
        dist = kpos - (t * tq + qi)
        tables.append((dist <= WINDOW) & (dist >= -WINDOW) & in_span & (kpos < n_real) | tail_ok)
    tables.append(in_span & (ks <= qi + (WINDOW - N_META)) | tail_ok)
    return jnp.where(jnp.stack(tables), 0.0, NEG).astype(F32)


def _window_attn_call(qt, k, vt, sink, n_real):
    B, Lp, _ = k.shape
    tq = 256 if n_real % 256 == 0 and n_real >= 384 else BLOCK
    assert n_real % tq == 0 and tq + 2 * BLOCK <= Lp
    bias = _window_bias(n_real, tq)
    grid_spec = pltpu.PrefetchScalarGridSpec(
        num_scalar_prefetch=1, grid=(B, n_real // tq + 1),
        in_specs=[pl.BlockSpec((1, GQA_HEADS * HEAD64, tq), lambda b, t, s: (b, 0, t)),
                  pl.BlockSpec((1, Lp, LANES), lambda b, t, s: (b, 0, 0)),
                  pl.BlockSpec((1, 2 * GQA_VT_ROWS, Lp), lambda b, t, s: (b, 0, 0)),
                  _resident(bias.shape)],
        out_specs=pl.BlockSpec((1, tq, 512), lambda b, t, s: (b, t, 0)))
    return pl.pallas_call(
        functools.partial(_window_attn_kernel, n_real=n_real),
        out_shape=jax.ShapeDtypeStruct((B, Lp, 512), BF16), grid_spec=grid_spec,
        compiler_params=_cparams(("parallel", "parallel")), name="window_attn",
    )(sink, qt, k, vt, bias)


def _post_kernel(*refs, has_tail):
    h_ref, tail_ref = (refs[0], refs[1]) if has_tail else (refs[0], None)
    oda_ref, om_ref, og_ref, ng_ref, wg_ref, wb_ref, wo_ref, mg_ref, wu_ref, wd_ref, out_ref = refs[2 if has_tail else 1:]
    x = _row_tile(h_ref, tail_ref)
    xn = ((x * _rms_scale(x, D_MODEL)) * ng_ref[...]).astype(BF16)
    gl = jnp.dot(xn, wg_ref[...], preferred_element_type=F32)
    merged = None
    for g, o_ref in enumerate((oda_ref, om_ref, og_ref)):
        proj = jnp.dot(o_ref[0], wb_ref[g], preferred_element_type=F32)
        term = jax.nn.sigmoid(gl[:, g * D_MODEL:(g + 1) * D_MODEL]) * proj
        merged = term if merged is None else merged + term
    h1 = x + jnp.dot(merged.astype(BF16), wo_ref[...], preferred_element_type=F32)
    h1n = ((h1 * _rms_scale(h1, D_MODEL)) * mg_ref[...]).astype(BF16)
    u = jnp.dot(h1n, wu_ref[...], preferred_element_type=F32)
    a = jnp.square(jnp.maximum(u, 0.0)).astype(BF16)
    out_ref[0] = h1 + jnp.dot(a, wd_ref[...], preferred_element_type=F32)


def _post_call(h, tail, oda, om, og, ng, wg, wb, wo, mg, wu, wd, out_rows):
    B, Lp, _ = oda.shape
    D = h.shape[2]
    tm = _pick(Lp, (POST_TM, 128))
    row = lambda b, j: (b, j, 0)
    stream = [h] if tail is None else [h, tail]
    in_specs = [_stream_spec(h, tm)] + ([] if tail is None else [_resident(tail.shape)]) + [
        pl.BlockSpec((1, tm, BRANCH_WIDTH), row)] * 3 + [_resident(a.shape) for a in (ng, wg, wb, wo, mg, wu, wd)]
    return pl.pallas_call(
        functools.partial(_post_kernel, has_tail=tail is not None),
        out_shape=jax.ShapeDtypeStruct((B, out_rows, D), F32), grid=(B, pl.cdiv(out_rows, tm)),
        in_specs=in_specs, out_specs=pl.BlockSpec((1, tm, D), row),
        compiler_params=_cparams(("parallel", "parallel")), name="post",
    )(*stream, oda, om, og, ng, wg, wb, wo, mg, wu, wd)


def _swap_halves(m):
    return jnp.flip(m.reshape(m.shape[:-1] + (m.shape[-1] // HEAD64, 2, HEAD64 // 2)), axis=-2).reshape(m.shape)


def _pair_gain(g):
    g = jnp.tile(g.astype(F32), 2)[None, :]
    return jnp.concatenate([g, _swap_halves(g)], axis=0)


def _pad_gain(g):
    return jnp.concatenate([g.astype(F32), jnp.zeros((LANES - g.shape[0],), F32)])[None, :]


def _layer_params(l, attn_norm_g, w_in, da_q_norm_g, da_k_norm_g, da_lam_q1, da_lam_k1, da_lam_q2, da_lam_k2,
                  da_subln_g, mla_cq_norm_g, mla_ckv_norm_g, mla_w_uq, mla_w_ukv, mla_q_norm_g, mla_k_norm_g,
                  gqa_q_norm_g, gqa_k_norm_g, gqa_sink, w_branch, w_out, mlp_norm_g, w_up, w_down):
    w = w_in[l]
    sec = [w[:, a:b] for a, b in zip(IN_OFF[:-1], IN_OFF[1:])]
    da_q, da_k, da_v, cq, ckv, kr, gq, gk, gv, gate = sec
    pad64 = lambda m: jnp.pad(m, ((0, 0),) * (m.ndim - 1) + ((0, LANES - HEAD64),))
    wa = jnp.concatenate([da_q, da_k, da_v, cq, ckv,
                          pad64(kr), pad64(_swap_halves(kr)), gq, _swap_halves(gq),
                          gk, _swap_halves(gk), gv], axis=1).astype(BF16)
    wuq = mla_w_uq[l].reshape(MLA_Q_RANK, MLA_HEADS, MLA_QK)
    wuq_r = wuq[:, :, MLA_NOPE:]
    wuq = jnp.concatenate([wuq[:, :, :MLA_NOPE], pad64(wuq_r), pad64(_swap_halves(wuq_r))], axis=-1)
    wuq = wuq.reshape(MLA_Q_RANK, MLA_HEADS * 3 * LANES).astype(BF16)
    wukv = mla_w_ukv[l].reshape(MLA_KV_RANK, MLA_HEADS, 2 * LANES)
    wuk = wukv[:, :, :MLA_NOPE].reshape(MLA_KV_RANK, MLA_HEADS * LANES).astype(BF16)
    wuv = wukv[:, :, MLA_NOPE:].reshape(MLA_KV_RANK, MLA_HEADS * LANES).astype(BF16)
    mq, mk = mla_q_norm_g[l], mla_k_norm_g[l]
    gt = jnp.concatenate([_pair_gain(da_q_norm_g[l]), _pair_gain(da_k_norm_g[l]),
                          _pair_gain(gqa_q_norm_g[l]), _pair_gain(gqa_k_norm_g[l]),
                          mq[None, :MLA_NOPE].astype(F32), _pad_gain(mq[MLA_NOPE:]), _pad_gain(_swap_halves(mq[MLA_NOPE:])),
                          mk[None, :MLA_NOPE].astype(F32), _pad_gain(mk[MLA_NOPE:]), _pad_gain(_swap_halves(mk[MLA_NOPE:])),
                          jnp.zeros((2, LANES), F32)], axis=0)
    lam_tab = jnp.concatenate([_pad_gain(v[l]) for v in (da_lam_q1, da_lam_k1, da_lam_q2, da_lam_k2)], axis=0)
    return dict(
        ng=attn_norm_g[l][None, :].astype(F32), wa=wa, wuq=wuq, wuk=wuk, wuv=wuv, gt=gt,
        gcq=mla_cq_norm_g[l][None, :].astype(F32), gckv=mla_ckv_norm_g[l][None, :].astype(F32),
        lam_tab=lam_tab, sub_g=jnp.broadcast_to(da_subln_g[l].astype(F32)[:, None], (LANES, LANES)),
        sink=gqa_sink[l].astype(F32),
        wg=gate.astype(BF16), wb=w_branch[l].astype(BF16), wo=w_out[l].astype(BF16),
        mg=mlp_norm_g[l][None, :].astype(F32), wu=w_up[l].astype(BF16), wd=w_down[l].astype(BF16))


def _rope_tables(n_real):
    r = jnp.arange(n_real + TAIL, dtype=jnp.int32)
    pos = jnp.where(r < n_real, r + N_META, jnp.where(r < n_real + N_META, r - n_real, 0)).astype(F32)
    inv_freq = 1.0 / (ROPE_THETA ** (jnp.arange(0, HEAD64, 2, dtype=F32) / HEAD64))
    ang = pos[:, None] * inv_freq[None, :]
    cos, sin = jnp.cos(ang), jnp.sin(ang)
    return jnp.tile(cos, (1, 4)), jnp.tile(jnp.concatenate([-sin, sin], axis=1), (1, 2))


def _trunk(x, meta_tokens, layers):
    B, S, D = x.shape
    assert D == D_MODEL and S % BLOCK == 0
    tail = jnp.concatenate([meta_tokens.astype(x.dtype), jnp.zeros((TAIL - N_META, D), x.dtype)], axis=0)
    h = x
    cos, sin = _rope_tables(S)
    for l, p in enumerate(layers):
        lam_init = 0.8 - 0.6 * math.exp(-0.3 * l)
        qtda, kda, vtda, qtm, km, vtm, qtg, kg, vtg = _prep_call(
            h, tail, cos, sin, p["ng"], p["wa"], p["wuq"], p["wuk"], p["wuv"], p["gt"], p["gcq"], p["gckv"])
        oda = _dense_attn_call(qtda, kda, vtda, S, differential=True, lam_tab=p["lam_tab"], sub_g=p["sub_g"],
                               lam_init=lam_init, name="diff_attn")
        om = _dense_attn_call(qtm, km, vtm, S, differential=False, name="latent_attn")
        og = _window_attn_call(qtg, kg, vtg, p["sink"], S)
        h = _post_call(h, tail, oda, om, og, p["ng"], p["wg"], p["wb"], p["wo"], p["mg"], p["wu"], p["wd"],
                       out_rows=S if l == len(layers) - 1 else S + TAIL)
        tail = None
    return h


def kernel(x_prompt, x_sample, meta_tokens, attn_norm_g, w_in, da_q_norm_g, da_k_norm_g, da_lam_q1, da_lam_k1, da_lam_q2, da_lam_k2, da_subln_g, mla_cq_norm_g, mla_ckv_norm_g, mla_w_uq, mla_w_ukv, mla_q_norm_g, mla_k_norm_g, gqa_q_norm_g, gqa_k_norm_g, gqa_sink, w_branch, w_out, mlp_norm_g, w_up, w_down):
    depth = w_in.shape[0]
    layers = [_layer_params(l, attn_norm_g, w_in, da_q_norm_g, da_k_norm_g, da_lam_q1, da_lam_k1, da_lam_q2,
                            da_lam_k2, da_subln_g, mla_cq_norm_g, mla_ckv_norm_g, mla_w_uq, mla_w_ukv,
                            mla_q_norm_g, mla_k_norm_g, gqa_q_norm_g, gqa_k_norm_g, gqa_sink, w_branch, w_out,
                            mlp_norm_g, w_up, w_down) for l in range(depth)]
    return (_trunk(x_prompt, meta_tokens, layers), _trunk(x_sample, meta_tokens, layers))
```
